```python
import jax, jax.numpy as jnp
from jax import lax
import numpy as np

D_MODEL = 1024
BATCH = 8
SEQ = 2048
DEPTH = 1

HEAD_DIM = 64
DSA_HEADS = 8
DSA_LATENT = 128
IDX_HEADS = 8
IDX_DIM = 32
DSA_TOPK_MAX = 256
NSA_HEADS = 8
NSA_KV_HEADS = 2
NSA_GROUP = NSA_HEADS // NSA_KV_HEADS
CMP_LEN = 32
CMP_STRIDE = 16
CMP_HIDDEN = 128
SLC_LEN = 64
SLC_TOPN = 16
WINDOW = 512
FORCE_BONUS = 1e6
Q_BLOCK = 128
D_FF = 4 * D_MODEL
RMS_EPS = 1e-6
NEG_INF = -1e30

WIDTH_A = DSA_HEADS * HEAD_DIM
WIDTH_B = NSA_HEADS * HEAD_DIM
NSA_KV_W = NSA_KV_HEADS * HEAD_DIM
COL_SIZES = (WIDTH_A, DSA_LATENT, IDX_HEADS * IDX_DIM, IDX_DIM, IDX_HEADS,
             WIDTH_B, 6 * NSA_KV_W, 3 * NSA_HEADS, 2 * D_MODEL)
D_IN = WIDTH_A + DSA_LATENT + IDX_HEADS * IDX_DIM + IDX_DIM + IDX_HEADS + WIDTH_B + 6 * NSA_KV_W + 3 * NSA_HEADS + 2 * D_MODEL

kernel_name = "hybrid_dsa_nsa_gated_block"


def rms_norm(x, g):
    xf = x.astype(jnp.float32)
    y = xf * lax.rsqrt(jnp.mean(xf * xf, axis=-1, keepdims=True) + RMS_EPS)
    return (y * g.astype(jnp.float32)).astype(x.dtype)


def alibi_slopes(n_heads):
    return jnp.asarray(2.0 ** (-8.0 * np.arange(1, n_heads + 1) / n_heads), dtype=jnp.float32)


def masked_softmax(s, valid):
    p = jax.nn.softmax(jnp.where(valid, s, NEG_INF), axis=-1)
    return jnp.where(valid, p, 0.0)


def split_cols(a, sizes):
    out, start = [], 0
    for n in sizes:
        out.append(a[..., start:start + n])
        start += n
    return out


def to_blocks(a, nb):
    return a.reshape((a.shape[0], nb, Q_BLOCK) + a.shape[2:]).swapaxes(0, 1)


def from_blocks(a):
    a = a.swapaxes(0, 1)
    return a.reshape((a.shape[0], a.shape[1] * a.shape[2]) + a.shape[3:])


def dsa_mixer(q, c_kv, q_idx, k_idx, w_idx, g_q, g_k, g_lat, w_uk, w_uv):
    B, S = q.shape[0], q.shape[1]
    nb = S // Q_BLOCK
    topk = min(DSA_TOPK_MAX, S // 4)
    c = rms_norm(c_kv, g_lat)
    k = rms_norm(c @ w_uk, g_k)
    v = c @ w_uv
    q = rms_norm(q, g_q)
    w_idx = w_idx * (IDX_HEADS ** -0.5 * IDX_DIM ** -0.5)
    slopes = alibi_slopes(DSA_HEADS)
    bi = jnp.arange(B)[:, None, None]
    s_pos = jnp.arange(S)

    def block(args):
        qb, qib, wb, start = args
        t = start + jnp.arange(Q_BLOCK)
        logits = jnp.einsum('bthd,bsd->bths', qib, k_idx).astype(jnp.float32)
        score = jnp.einsum('bths,bth->bts', jax.nn.relu(logits), wb.astype(jnp.float32))
        score = jnp.where((s_pos[None, :] <= t[:, None])[None], score, -jnp.inf)
        _, idx = lax.top_k(score, topk)
        ks = k[bi, idx]
        vs = v[bi, idx]
        dist = (t[None, :, None] - idx).astype(jnp.float32)
        s = jnp.einsum('bthd,btkd->bthk', qb, ks).astype(jnp.float32) * (HEAD_DIM ** -0.5)
        s = s - slopes[None, None, :, None] * dist[:, :, None, :]
        p = masked_softmax(s, (dist >= 0)[:, :, None, :])
        return jnp.einsum('bthk,btkd->bthd', p.astype(vs.dtype), vs)

    starts = jnp.arange(nb, dtype=jnp.int32) * Q_BLOCK
    out = lax.map(block, (to_blocks(q, nb), to_blocks(q_idx, nb), to_blocks(w_idx, nb), starts))
    return from_blocks(out).reshape(B, S, WIDTH_A)


def nsa_mixer(q, kv, gates, g_q, g_kc, g_ks, g_kw, pe_cmp, phi_k1, phi_k2, phi_v1, phi_v2):
    B, S = q.shape[0], q.shape[1]
    KH, G, dh = NSA_KV_HEADS, NSA_GROUP, HEAD_DIM
    nb = S // Q_BLOCK
    scale = dh ** -0.5
    q = rms_norm(q.reshape(B, S, KH, G, dh), g_q)
    k_c, v_c, k_s, v_s, k_w, v_w = [a.reshape(B, S, KH, dh) for a in split_cols(kv, (NSA_KV_W,) * 6)]
    k_s = rms_norm(k_s, g_ks)
    k_w = rms_norm(k_w, g_kw)
    slopes = alibi_slopes(NSA_HEADS).reshape(KH, G)
    pos = jnp.arange(S)

    n_cmp = (S - CMP_LEN) // CMP_STRIDE + 1
    cmp_start = jnp.arange(n_cmp) * CMP_STRIDE
    blk_pos = cmp_start[:, None] + jnp.arange(CMP_LEN)[None, :]

    def compress(a, w1, w2):
        blocks = a[:, blk_pos] + pe_cmp[None, None, :, None, :]
        blocks = blocks.transpose(0, 1, 3, 2, 4).reshape(B, n_cmp, KH, CMP_LEN * dh)
        return jax.nn.gelu(blocks @ w1) @ w2

    kc = rms_norm(compress(k_c, phi_k1, phi_k2), g_kc)
    vc = compress(v_c, phi_v1, phi_v2)
    valid_c = (cmp_start + CMP_LEN - 1)[None, :] <= pos[:, None]
    s_c = jnp.einsum('bsgqd,bcgd->bsgqc', q, kc).astype(jnp.float32) * scale
    p_c = masked_softmax(s_c, valid_c[None, :, None, None, :])
    o_cmp = jnp.einsum('bsgqc,bcgd->bsgqd', p_c.astype(vc.dtype), vc)

    n_slc = S // SLC_LEN
    slc_start = jnp.arange(n_slc) * SLC_LEN
    overlap = ((cmp_start[:, None] < slc_start[None, :] + SLC_LEN)
               & (cmp_start[:, None] + CMP_LEN > slc_start[None, :])).astype(jnp.float32)
    imp = jnp.einsum('bsgqc,cj->bsgj', p_c, overlap)
    cur = pos // SLC_LEN
    blk = jnp.arange(n_slc)
    future = blk[None, :] > cur[:, None]
    forced = (blk[None, :] == 0) | (blk[None, :] == cur[:, None]) | (blk[None, :] == cur[:, None] - 1)
    imp = jnp.where(future[None, :, None, :], -jnp.inf,
                    imp + jnp.where(forced, FORCE_BONUS, 0.0)[None, :, None, :])
    n_top = min(SLC_TOPN, n_slc)
    _, sel = lax.top_k(imp, n_top)

    kb = k_s.reshape(B, n_slc, SLC_LEN, KH, dh).transpose(0, 3, 1, 2, 4)
    vb = v_s.reshape(B, n_slc, SLC_LEN, KH, dh).transpose(0, 3, 1, 2, 4)
    kw_pad = jnp.pad(k_w, ((0, 0), (WINDOW, 0), (0, 0), (0, 0)))
    vw_pad = jnp.pad(v_w, ((0, 0), (WINDOW, 0), (0, 0), (0, 0)))
    bi = jnp.arange(B)[:, None, None, None]
    gi = jnp.arange(KH)[None, None, :, None]
    win_len = WINDOW + Q_BLOCK

    def block(args):
        qb, selb, start = args
        t = start + jnp.arange(Q_BLOCK)
        ks = kb[bi, gi, selb]
        vs = vb[bi, gi, selb]
        kpos = selb[..., None] * SLC_LEN + jnp.arange(SLC_LEN)
        dist = (t[None, :, None, None, None] - kpos).astype(jnp.float32)
        s = jnp.einsum('btgqd,btgnld->btgqnl', qb, ks).astype(jnp.float32) * scale
        s = s - slopes[None, None, :, :, None, None] * dist[:, :, :, None]
        s = s.reshape(B, Q_BLOCK, KH, G, n_top * SLC_LEN)
        valid = (dist >= 0).reshape(B, Q_BLOCK, KH, 1, n_top * SLC_LEN)
        p = masked_softmax(s, valid).reshape(B, Q_BLOCK, KH, G, n_top, SLC_LEN)
        o_slc = jnp.einsum('btgqnl,btgnld->btgqd', p.astype(vs.dtype), vs)
        kw = lax.dynamic_slice_in_dim(kw_pad, start, win_len, axis=1)
        vw = lax.dynamic_slice_in_dim(vw_pad, start, win_len, axis=1)
        wpos = start - WINDOW + jnp.arange(win_len)
        wd = t[:, None] - wpos[None, :]
        wvalid = (wd >= 0) & (wd < WINDOW) & (wpos[None, :] >= 0)
        s_w = jnp.einsum('btgqd,bsgd->btgqs', qb, kw).astype(jnp.float32) * scale
        s_w = s_w - slopes[None, None, :, :, None] * wd.astype(jnp.float32)[None, :, None, None, :]
        p_w = masked_softmax(s_w, wvalid[None, :, None, None, :])
        o_win = jnp.einsum('btgqs,bsgd->btgqd', p_w.astype(vw.dtype), vw)
        return o_slc, o_win

    starts = jnp.arange(nb, dtype=jnp.int32) * Q_BLOCK
    o_slc, o_win = lax.map(block, (to_blocks(q, nb), to_blocks(sel, nb), starts))
    o_slc = from_blocks(o_slc)
    o_win = from_blocks(o_win)
    g = jax.nn.sigmoid(gates.reshape(B, S, KH, G, 3))
    o = g[..., 0:1] * o_cmp + g[..., 1:2] * o_slc + g[..., 2:3] * o_win
    return o.reshape(B, S, WIDTH_B)


def setup_inputs(seed: int = 0) -> dict:
    key = jax.random.key(seed)
    ks = jax.random.split(key, 24)
    f32 = jnp.float32
    L, dh = DEPTH, HEAD_DIM

    def nrm(k, shape, scale):
        return jax.random.normal(k, shape, f32) * scale

    def gain(k, n):
        return 1.0 + 0.05 * jax.random.normal(k, (DEPTH, n), f32)

    return {
        "x": nrm(ks[0], (BATCH, SEQ, D_MODEL), 1.0),
        "g_mix": gain(ks[1], D_MODEL),
        "w_in": nrm(ks[2], (L, D_MODEL, D_IN), D_MODEL ** -0.5),
        "g_q_a": gain(ks[3], dh),
        "g_k_a": gain(ks[4], dh),
        "g_lat_a": gain(ks[5], DSA_LATENT),
        "w_uk_a": nrm(ks[6], (L, DSA_LATENT, dh), DSA_LATENT ** -0.5),
        "w_uv_a": nrm(ks[7], (L, DSA_LATENT, dh), DSA_LATENT ** -0.5),
        "g_q_b": gain(ks[8], dh),
        "g_kc_b": gain(ks[9], dh),
        "g_ks_b": gain(ks[10], dh),
        "g_kw_b": gain(ks[11], dh),
        "pe_cmp_b": nrm(ks[12], (L, CMP_LEN, dh), 0.1),
        "phi_k1_b": nrm(ks[13], (L, CMP_LEN * dh, CMP_HIDDEN), (CMP_LEN * dh) ** -0.5),
        "phi_k2_b": nrm(ks[14], (L, CMP_HIDDEN, dh), CMP_HIDDEN ** -0.5),
        "phi_v1_b": nrm(ks[15], (L, CMP_LEN * dh, CMP_HIDDEN), (CMP_LEN * dh) ** -0.5),
        "phi_v2_b": nrm(ks[16], (L, CMP_HIDDEN, dh), CMP_HIDDEN ** -0.5),
        "w_up_a": nrm(ks[17], (L, WIDTH_A, D_MODEL), WIDTH_A ** -0.5),
        "w_up_b": nrm(ks[18], (L, WIDTH_B, D_MODEL), WIDTH_B ** -0.5),
        "w_out": nrm(ks[19], (L, D_MODEL, D_MODEL), D_MODEL ** -0.5),
        "g_mlp": gain(ks[20], D_MODEL),
        "w_ff1": nrm(ks[21], (L, D_MODEL, D_FF), D_MODEL ** -0.5),
        "w_ff2": nrm(ks[22], (L, D_FF, D_MODEL), D_FF ** -0.5),
    }


def reference(x, g_mix, w_in, g_q_a, g_k_a, g_lat_a, w_uk_a, w_uv_a, g_q_b, g_kc_b, g_ks_b, g_kw_b,
              pe_cmp_b, phi_k1_b, phi_k2_b, phi_v1_b, phi_v2_b, w_up_a, w_up_b, w_out, g_mlp, w_ff1, w_ff2):
    B, S = x.shape[0], x.shape[1]
    for l in range(DEPTH):
        h = rms_norm(x, g_mix[l])
        proj = h @ w_in[l]
        q_a, c_a, qi_a, ki_a, wi_a, q_b, kv_b, gate_b, gate_m = split_cols(proj, COL_SIZES)
        o_a = dsa_mixer(q_a.reshape(B, S, DSA_HEADS, HEAD_DIM), c_a,
                        qi_a.reshape(B, S, IDX_HEADS, IDX_DIM), ki_a, wi_a,
                        g_q_a[l], g_k_a[l], g_lat_a[l], w_uk_a[l], w_uv_a[l])
        o_b = nsa_mixer(q_b, kv_b, gate_b, g_q_b[l], g_kc_b[l], g_ks_b[l], g_kw_b[l],
                        pe_cmp_b[l], phi_k1_b[l], phi_k2_b[l], phi_v1_b[l], phi_v2_b[l])
        gate_a_m, gate_b_m = split_cols(gate_m, (D_MODEL, D_MODEL))
        merged = jax.nn.sigmoid(gate_a_m) * (o_a @ w_up_a[l]) + jax.nn.sigmoid(gate_b_m) * (o_b @ w_up_b[l])
        x = x + merged @ w_out[l]
        h2 = rms_norm(x, g_mlp[l])
        x = x + jnp.square(jax.nn.relu(h2 @ w_ff1[l])) @ w_ff2[l]
    return x
```

```python
import functools

import numpy as np
import jax
import jax.numpy as jnp
from jax import lax
from jax.experimental import pallas as pl
from jax.experimental.pallas import tpu as pltpu

HEAD_DIM = 64
DSA_HEADS = 8
DSA_LATENT = 128
IDX_HEADS = 8
IDX_DIM = 32
DSA_TOPK_MAX = 256
NSA_HEADS = 8
NSA_KV_HEADS = 2
NSA_GROUP = NSA_HEADS // NSA_KV_HEADS
CMP_LEN = 32
CMP_STRIDE = 16
CMP_HIDDEN = 128
SLC_LEN = 64
SLC_TOPN = 16
WINDOW = 512
FORCE_BONUS = 1e6
Q_BLOCK = 128
RMS_EPS = 1e-6
NEG_INF = -1e30

WIDTH_A = DSA_HEADS * HEAD_DIM
WIDTH_B = NSA_HEADS * HEAD_DIM
NSA_KV_W = NSA_KV_HEADS * HEAD_DIM

LANES = 128
GATE_LANES = 3 * NSA_HEADS
MXU_DTYPE = jnp.bfloat16
VMEM_LIMIT = 52 * 1024 * 1024

_NT = (((1,), (1,)), ((), ()))


def _alibi_slopes(n_heads):
    return [float(v) for v in np.asarray(
        2.0 ** (-8.0 * np.arange(1, n_heads + 1) / n_heads), dtype=np.float32)]


def _dot(a, b):
    return jnp.dot(a, b, preferred_element_type=jnp.float32)


def _dot_nt(a, b):
    return lax.dot_general(a, b, _NT, preferred_element_type=jnp.float32)


def _rms(x, g):
    ms = jnp.mean(x * x, axis=-1, keepdims=True)
    return x * lax.rsqrt(ms + RMS_EPS) * g


def _pair_rms(x, g2):
    lane = lax.broadcasted_iota(jnp.int32, x.shape, 1)
    lo = lane < HEAD_DIM
    xx = x * x
    s_lo = jnp.sum(jnp.where(lo, xx, 0.0), axis=-1, keepdims=True)
    s_hi = jnp.sum(jnp.where(lo, 0.0, xx), axis=-1, keepdims=True)
    ms = jnp.where(lo, s_lo, s_hi) * (1.0 / HEAD_DIM)
    return x * lax.rsqrt(ms + RMS_EPS) * g2


def _params(sem):
    return pltpu.CompilerParams(dimension_semantics=sem, vmem_limit_bytes=VMEM_LIMIT)


_C_QA = (0, 512)
_C_CA = (512, 640)
_C_QI = (640, 896)
_C_K8 = (896, 1152)
_C_QB = (1152, 1664)
_C_KVB = (1664, 2432)
_C_GW = (2432, 2560)
_W_MAIN = 2560


def _proj_kernel(x_ref, gmix_ref, w_ref, glat_ref, wukv_ref, gka_ref, gks_ref, gkw_ref,
                 qa_ref, kva_ref, qi_ref, k8_ref, qb_ref, kvb_ref, gw_ref, *, w_idx_scale):
    x = x_ref[...]
    h = _rms(x, gmix_ref[...]).astype(MXU_DTYPE)

    def proj(c):
        return _dot(h, w_ref[:, c[0]:c[1]])

    qa_ref[...] = proj(_C_QA)
    c = _rms(proj(_C_CA), glat_ref[...]).astype(MXU_DTYPE)
    kv = _dot(c, wukv_ref[...])
    lane = lax.broadcasted_iota(jnp.int32, kv.shape, 1)
    kva_ref[...] = jnp.where(lane < HEAD_DIM, _pair_rms(kv, gka_ref[...]), kv)
    qi_ref[...] = proj(_C_QI).astype(qi_ref.dtype)
    k8_ref[...] = proj(_C_K8).astype(k8_ref.dtype)
    qb_ref[...] = proj(_C_QB)
    for j in range(6):
        slab = proj((_C_KVB[0] + j * LANES, _C_KVB[0] + (j + 1) * LANES))
        if j == 2:
            slab = _pair_rms(slab, gks_ref[...])
        elif j == 4:
            slab = _pair_rms(slab, gkw_ref[...])
        kvb_ref[:, j * LANES:(j + 1) * LANES] = slab
    gw = proj(_C_GW)
    gw_ref[...] = jnp.where(lane < GATE_LANES, jax.nn.sigmoid(gw), gw * w_idx_scale)


def _proj(x2, g_mix, w_main, g_lat, w_ukv, gk_a, gks2, gkw2, tm):
    T, D = x2.shape
    row = lambda w: pl.BlockSpec((tm, w), lambda i: (i, 0))
    full = lambda a: pl.BlockSpec(a.shape, lambda i: (0,) * a.ndim)
    f32 = jnp.float32
    out_shape = (
        jax.ShapeDtypeStruct((T, WIDTH_A), f32),
        jax.ShapeDtypeStruct((T, LANES), f32),
        jax.ShapeDtypeStruct((T, IDX_HEADS * IDX_DIM), MXU_DTYPE),
        jax.ShapeDtypeStruct((T, IDX_HEADS * IDX_DIM), MXU_DTYPE),
        jax.ShapeDtypeStruct((T, WIDTH_B), f32),
        jax.ShapeDtypeStruct((T, 6 * NSA_KV_W), f32),
        jax.ShapeDtypeStruct((T, LANES), f32),
    )
    return pl.pallas_call(
        functools.partial(_proj_kernel, w_idx_scale=IDX_HEADS ** -0.5 * IDX_DIM ** -0.5),
        grid=(T // tm,),
        in_specs=[row(D), full(g_mix), full(w_main), full(g_lat), full(w_ukv),
                  full(gk_a), full(gks2), full(gkw2)],
        out_specs=tuple(row(s.shape[1]) for s in out_shape),
        out_shape=out_shape,
        compiler_params=_params(("parallel",)),
        name="proj",
    )(x2, g_mix, w_main, g_lat, w_ukv, gk_a, gks2, gkw2)


def _compress_kernel(x_ref, pe1_ref, pe2_ref, w1_ref, w2_ref, gkc_ref, o_ref):
    i = pl.program_id(1)
    x = x_ref[...]
    n = x.shape[0]
    half = x.shape[1]
    a = _dot((x + pe1_ref[...]).astype(MXU_DTYPE), w1_ref[:half, :])
    b = _dot((x + pe2_ref[...]).astype(MXU_DTYPE), w1_ref[half:, :])
    h1 = a + pltpu.roll(b, n - 1, 0)
    y = _dot(jax.nn.gelu(h1).astype(MXU_DTYPE), w2_ref[...])
    o_ref[...] = jnp.where(i < NSA_KV_HEADS, _rms(y, gkc_ref[...]), y)


def _compress(xc, pe1, pe2, phi1, phi2, g_kc):
    B, four, n, w = xc.shape
    return pl.pallas_call(
        _compress_kernel,
        grid=(B, four),
        in_specs=[
            pl.BlockSpec((None, None, n, w), lambda b, i: (b, i, 0, 0)),
            pl.BlockSpec(pe1.shape, lambda b, i: (0, 0)),
            pl.BlockSpec(pe2.shape, lambda b, i: (0, 0)),
            pl.BlockSpec((None,) + phi1.shape[1:], lambda b, i: (i // NSA_KV_HEADS, 0, 0)),
            pl.BlockSpec((None,) + phi2.shape[1:], lambda b, i: (i // NSA_KV_HEADS, 0, 0)),
            pl.BlockSpec(g_kc.shape, lambda b, i: (0, 0)),
        ],
        out_specs=pl.BlockSpec((None, None, n, HEAD_DIM), lambda b, i: (b, i, 0, 0)),
        out_shape=jax.ShapeDtypeStruct((B, four, n, HEAD_DIM), jnp.float32),
        compiler_params=_params(("parallel", "parallel")),
        name="compress",
    )(xc, pe1, pe2, phi1, phi2, g_kc)


_INT_MIN = -2 ** 31


def _key_to_float(u):
    key = u ^ jnp.int32(_INT_MIN)
    bits = jnp.where(key >= 0, key, key ^ jnp.int32(0x7FFFFFFF))
    return lax.bitcast_convert_type(bits, jnp.float32)


def _kth_largest(score, k):
    rows = score.shape[0]

    def body(it, u):
        u_try = u | lax.shift_left(jnp.int32(1), 31 - it)
        thr = _key_to_float(u_try)
        cnt = jnp.sum(jnp.where(score >= thr, 1.0, 0.0), axis=-1, keepdims=True)
        return jnp.where(cnt >= k, u_try, u)

    u = lax.fori_loop(0, 32, body, jnp.zeros((rows, 1), jnp.int32))
    thr = _key_to_float(u)
    return jnp.where(thr != thr, -jnp.inf, thr)


def _dsa_kernel(qa_ref, qi_ref, gw_ref, k8_ref, kva_ref, gq_ref, o_ref, sel_ref, *, topk, slopes):
    qblk = pl.program_id(1)
    S = k8_ref.shape[0]
    f32 = jnp.float32
    t = qblk * Q_BLOCK + lax.broadcasted_iota(jnp.int32, (Q_BLOCK, 1), 0)
    spos = lax.broadcasted_iota(jnp.int32, (Q_BLOCK, S), 1)
    causal = spos <= t

    qi = qi_ref[...]
    k8 = k8_ref[...]
    gw = gw_ref[...]
    lane = lax.broadcasted_iota(jnp.int32, qi.shape, 1)
    score = jnp.zeros((Q_BLOCK, S), f32)
    for h in range(IDX_HEADS):
        qm = jnp.where((lane >= h * IDX_DIM) & (lane < (h + 1) * IDX_DIM), qi, jnp.zeros_like(qi))
        logits = _dot_nt(qm, k8)
        score = score + jnp.maximum(logits, 0.0) * gw[:, GATE_LANES + h:GATE_LANES + h + 1]
    score = jnp.where(causal, score, -jnp.inf)

    thr = _kth_largest(score, float(topk))
    gt = score > thr
    eq = (score == thr) & causal
    need = float(topk) - jnp.sum(jnp.where(gt, 1.0, 0.0), axis=-1, keepdims=True)
    n_eq = jnp.sum(jnp.where(eq, 1.0, 0.0), axis=-1, keepdims=True)
    sel_ref[...] = jnp.where(gt | eq, 1.0, 0.0)

    @pl.when(jnp.max(n_eq - need) > 0.0)
    def _():
        cw = 256
        r = lax.broadcasted_iota(jnp.int32, (cw, cw), 0)
        c = lax.broadcasted_iota(jnp.int32, (cw, cw), 1)
        upper = jnp.where(r <= c, 1.0, 0.0).astype(MXU_DTYPE)
        carry = jnp.zeros((Q_BLOCK, 1), f32)
        for j in range(S // cw):
            sl = slice(j * cw, (j + 1) * cw)
            eq_j = jnp.where(eq[:, sl], 1.0, 0.0)
            prefix = _dot(eq_j.astype(MXU_DTYPE), upper) + carry
            keep = gt[:, sl] | (eq[:, sl] & (prefix <= need))
            sel_ref[:, sl] = jnp.where(keep, 1.0, 0.0)
            carry = carry + jnp.sum(eq_j, axis=-1, keepdims=True)

    sel = sel_ref[...] > 0.5
    dist = (t - spos).astype(f32)
    kv = kva_ref[...]
    k = kv[:, :HEAD_DIM].astype(MXU_DTYPE)
    v = kv[:, HEAD_DIM:].astype(MXU_DTYPE)
    qa = qa_ref[...]
    gq = gq_ref[...]
    scale = HEAD_DIM ** -0.5
    outs = []
    for hg in range(DSA_HEADS // 4):
        qs = []
        for g in range(4):
            h = hg * 4 + g
            qh = _rms(qa[:, h * HEAD_DIM:(h + 1) * HEAD_DIM], gq) * scale
            qs.append(qh.astype(MXU_DTYPE))
        sc = _dot_nt(jnp.concatenate(qs, axis=0), k)
        for g in range(4):
            h = hg * 4 + g
            s_h = sc[g * Q_BLOCK:(g + 1) * Q_BLOCK] - slopes[h] * dist
            s_h = jnp.where(sel, s_h, NEG_INF)
            m = jnp.max(s_h, axis=-1, keepdims=True)
            e = jnp.exp(s_h - m)
            l = jnp.sum(e, axis=-1, keepdims=True)
            outs.append(_dot(e.astype(MXU_DTYPE), v) / l)
    o_ref[...] = jnp.concatenate(outs, axis=-1).astype(o_ref.dtype)


def _dsa(qa, qi, gw, k8, kva, g_q):
    B, S, _ = qa.shape
    topk = min(DSA_TOPK_MAX, S // 4)
    blk = lambda w: pl.BlockSpec((None, Q_BLOCK, w), lambda b, i: (b, i, 0))
    seq = lambda w: pl.BlockSpec((None, S, w), lambda b, i: (b, 0, 0))
    return pl.pallas_call(
        functools.partial(_dsa_kernel, topk=topk, slopes=_alibi_slopes(DSA_HEADS)),
        grid=(B, S // Q_BLOCK),
        in_specs=[blk(WIDTH_A), blk(IDX_HEADS * IDX_DIM), blk(LANES),
                  seq(IDX_HEADS * IDX_DIM), seq(LANES),
                  pl.BlockSpec(g_q.shape, lambda b, i: (0, 0))],
        out_specs=blk(WIDTH_A),
        out_shape=jax.ShapeDtypeStruct((B, S, WIDTH_A), MXU_DTYPE),
        scratch_shapes=[pltpu.VMEM((Q_BLOCK, S), jnp.float32)],
        compiler_params=_params(("parallel", "arbitrary")),
        name="dsa",
    )(qa, qi, gw, k8, kva, g_q)


def _softmax_pv(s, valid, v):
    s = jnp.where(valid, s, NEG_INF)
    m = jnp.max(s, axis=-1, keepdims=True)
    e = jnp.where(valid, jnp.exp(s - m), 0.0)
    l = jnp.sum(e, axis=-1, keepdims=True)
    inv = 1.0 / jnp.where(l > 0.0, l, 1.0)
    return e, inv


def _nsa_kernel(qb_ref, gw_ref, kvb_ref, kcv_ref, gq_ref, o_ref, *, slopes, n_cmp):
    qblk = pl.program_id(1)
    S = kvb_ref.shape[0]
    nc = kcv_ref.shape[1]
    n_slc = S // SLC_LEN
    n_top = min(SLC_TOPN, n_slc)
    win_len = WINDOW + Q_BLOCK
    f32 = jnp.float32
    QB, G = Q_BLOCK, NSA_GROUP
    scale = HEAD_DIM ** -0.5

    t0 = qblk * QB
    t = t0 + lax.broadcasted_iota(jnp.int32, (QB, 1), 0)
    t4 = jnp.concatenate([t] * G, axis=0)

    qb = qb_ref[...]
    gw = gw_ref[...]
    gq = gq_ref[...]

    c_i = lax.broadcasted_iota(jnp.int32, (nc, n_slc), 0) * CMP_STRIDE
    j_i = lax.broadcasted_iota(jnp.int32, (nc, n_slc), 1) * SLC_LEN
    overlap = jnp.where((c_i < j_i + SLC_LEN) & (c_i + CMP_LEN > j_i) & (c_i < n_cmp * CMP_STRIDE), 1.0, 0.0)
    overlap = overlap.astype(MXU_DTYPE)
    e_j = lax.broadcasted_iota(jnp.int32, (n_slc, S), 0)
    e_s = lax.broadcasted_iota(jnp.int32, (n_slc, S), 1)
    expand = jnp.where(lax.shift_right_logical(e_s, 6) == e_j, 1.0, 0.0).astype(MXU_DTYPE)

    cpos = lax.broadcasted_iota(jnp.int32, (G * QB, nc), 1) * CMP_STRIDE + (CMP_LEN - 1)
    valid_c = cpos <= t4
    spos = lax.broadcasted_iota(jnp.int32, (QB, S), 1)
    causal = spos <= t
    dist = (t - spos).astype(f32)
    w0 = pl.multiple_of(jnp.maximum(t0 - WINDOW, 0), QB)
    wd = t - (w0 + lax.broadcasted_iota(jnp.int32, (QB, win_len), 1))
    wvalid = (wd >= 0) & (wd < WINDOW)
    wdist = wd.astype(f32)
    jblk = lax.broadcasted_iota(jnp.int32, (QB, n_slc), 1)
    cur = lax.shift_right_logical(t, 6)
    future = jblk > cur
    forced = (jblk == 0) | (jblk == cur) | (jblk == cur - 1)

    outs = []
    for kh in range(NSA_KV_HEADS):
        hs = [kh * G + g for g in range(G)]
        q = jnp.concatenate(
            [(_rms(qb[:, h * HEAD_DIM:(h + 1) * HEAD_DIM], gq) * scale).astype(MXU_DTYPE) for h in hs], axis=0)

        kc = kcv_ref[kh].astype(MXU_DTYPE)
        vc = kcv_ref[NSA_KV_HEADS + kh].astype(MXU_DTYPE)
        e_c, inv_c = _softmax_pv(_dot_nt(q, kc), valid_c, vc)
        p_c = e_c * inv_c
        o_cmp = _dot(p_c.astype(MXU_DTYPE), vc)

        p_sum = p_c[0:QB] + p_c[QB:2 * QB] + p_c[2 * QB:3 * QB] + p_c[3 * QB:4 * QB]
        p_hi = p_sum.astype(MXU_DTYPE)
        p_lo = (p_sum - p_hi.astype(f32)).astype(MXU_DTYPE)
        imp = _dot(p_hi, overlap) + _dot(p_lo, overlap)
        val = jnp.where(future, -jnp.inf, imp + jnp.where(forced, FORCE_BONUS, 0.0))
        rank = jnp.zeros((QB, n_slc), f32)
        for i in range(n_slc):
            vi = val[:, i:i + 1]
            rank = rank + jnp.where((vi > val) | ((vi == val) & (jblk > i)), 1.0, 0.0)
        chosen = jnp.where(rank < n_top, 1.0, 0.0).astype(MXU_DTYPE)
        sel = (_dot(chosen, expand) > 0.5) & causal

        k_s = kvb_ref[:, 2 * LANES + kh * HEAD_DIM:2 * LANES + (kh + 1) * HEAD_DIM].astype(MXU_DTYPE)
        v_s = kvb_ref[:, 3 * LANES + kh * HEAD_DIM:3 * LANES + (kh + 1) * HEAD_DIM].astype(MXU_DTYPE)
        sc = _dot_nt(q, k_s)
        o_slc = []
        for g in range(G):
            s_g = sc[g * QB:(g + 1) * QB] - slopes[hs[g]] * dist
            e, inv = _softmax_pv(s_g, sel, v_s)
            o_slc.append(_dot(e.astype(MXU_DTYPE), v_s) * inv)

        k_w = kvb_ref[pl.ds(w0, win_len), 4 * LANES + kh * HEAD_DIM:4 * LANES + (kh + 1) * HEAD_DIM]
        v_w = kvb_ref[pl.ds(w0, win_len), 5 * LANES + kh * HEAD_DIM:5 * LANES + (kh + 1) * HEAD_DIM]
        k_w = k_w.astype(MXU_DTYPE)
        v_w = v_w.astype(MXU_DTYPE)
        sw = _dot_nt(q, k_w)
        for g in range(G):
            h = hs[g]
            s_g = sw[g * QB:(g + 1) * QB] - slopes[h] * wdist
            e, inv = _softmax_pv(s_g, wvalid, v_w)
            o_win = _dot(e.astype(MXU_DTYPE), v_w) * inv
            g0 = gw[:, 3 * h + 0:3 * h + 1]
            g1 = gw[:, 3 * h + 1:3 * h + 2]
            g2 = gw[:, 3 * h + 2:3 * h + 3]
            outs.append(g0 * o_cmp[g * QB:(g + 1) * QB] + g1 * o_slc[g] + g2 * o_win)
    o_ref[...] = jnp.concatenate(outs, axis=-1).astype(o_ref.dtype)


def _nsa(qb, gw, kvb, kcv, g_q, n_cmp):
    B, S, _ = qb.shape
    nc = kcv.shape[2]
    blk = lambda w: pl.BlockSpec((None, Q_BLOCK, w), lambda b, i: (b, i, 0))
    return pl.pallas_call(
        functools.partial(_nsa_kernel, slopes=_alibi_slopes(NSA_HEADS), n_cmp=n_cmp),
        grid=(B, S // Q_BLOCK),
        in_specs=[blk(WIDTH_B), blk(LANES),
                  pl.BlockSpec((None, S, 6 * NSA_KV_W), lambda b, i: (b, 0, 0)),
                  pl.BlockSpec((None, 2 * NSA_KV_HEADS, nc, HEAD_DIM), lambda b, i: (b, 0, 0, 0)),
                  pl.BlockSpec(g_q.shape, lambda b, i: (0, 0))],
        out_specs=blk(WIDTH_B),
        out_shape=jax.ShapeDtypeStruct((B, S, WIDTH_B), MXU_DTYPE),
        compiler_params=_params(("parallel", "arbitrary")),
        name="nsa",
    )(qb, gw, kvb, kcv, g_q)


def _merge_kernel(x_ref, oa_ref, ob_ref, gmix_ref, wg_ref, wua_ref, wub_ref, wo_ref, o_ref):
    x = x_ref[...]
    D = x.shape[1]
    h = _rms(x, gmix_ref[...]).astype(MXU_DTYPE)
    ga = jax.nn.sigmoid(_dot(h, wg_ref[:, :D]))
    gb = jax.nn.sigmoid(_dot(h, wg_ref[:, D:]))
    merged = ga * _dot(oa_ref[...], wua_ref[...]) + gb * _dot(ob_ref[...], wub_ref[...])
    o_ref[...] = x + _dot(merged.astype(MXU_DTYPE), wo_ref[...])


def _merge(x2, oa, ob, g_mix, w_gate, w_up_a, w_up_b, w_out, tm):
    T, D = x2.shape
    row = lambda w: pl.BlockSpec((tm, w), lambda i: (i, 0))
    full = lambda a: pl.BlockSpec(a.shape, lambda i: (0,) * a.ndim)
    return pl.pallas_call(
        _merge_kernel,
        grid=(T // tm,),
        in_specs=[row(D), row(WIDTH_A), row(WIDTH_B), full(g_mix), full(w_gate),
                  full(w_up_a), full(w_up_b), full(w_out)],
        out_specs=row(D),
        out_shape=jax.ShapeDtypeStruct((T, D), jnp.float32),
        compiler_params=_params(("parallel",)),
        name="merge",
    )(x2, oa, ob, g_mix, w_gate, w_up_a, w_up_b, w_out)


def _ffn_kernel(x_ref, g_ref, w1_ref, w2_ref, o_ref, *, chunk):
    x = x_ref[...]
    h = _rms(x, g_ref[...]).astype(MXU_DTYPE)
    acc = x
    for c in range(w1_ref.shape[1] // chunk):
        u = jnp.maximum(_dot(h, w1_ref[:, c * chunk:(c + 1) * chunk]), 0.0)
        acc = acc + _dot((u * u).astype(MXU_DTYPE), w2_ref[c * chunk:(c + 1) * chunk, :])
    o_ref[...] = acc


def _ffn(x2, g_mlp, w1, w2, tm):
    T, D = x2.shape
    row = pl.BlockSpec((tm, D), lambda i: (i, 0))
    full = lambda a: pl.BlockSpec(a.shape, lambda i: (0,) * a.ndim)
    return pl.pallas_call(
        functools.partial(_ffn_kernel, chunk=1024),
        grid=(T // tm,),
        in_specs=[row, full(g_mlp), full(w1), full(w2)],
        out_specs=row,
        out_shape=jax.ShapeDtypeStruct((T, D), jnp.float32),
        compiler_params=_params(("parallel",)),
        name="ffn",
    )(x2, g_mlp, w1, w2)


def _layer(x, g_mix, w_in, g_q_a, g_k_a, g_lat_a, w_uk_a, w_uv_a, g_q_b, g_kc_b, g_ks_b, g_kw_b,
           pe_cmp_b, phi_k1_b, phi_k2_b, phi_v1_b, phi_v2_b, w_up_a, w_up_b, w_out, g_mlp, w_ff1, w_ff2):
    B, S, D = x.shape
    T = B * S
    f32 = jnp.float32
    col_sizes = (WIDTH_A, DSA_LATENT, IDX_HEADS * IDX_DIM, IDX_DIM, IDX_HEADS,
                 WIDTH_B, 6 * NSA_KV_W, 3 * NSA_HEADS, 2 * D)
    offs = np.cumsum((0,) + col_sizes)
    w_qa, w_ca, w_qi, w_ki, w_wi, w_qb, w_kvb, w_gb, w_gm = [
        w_in[:, offs[i]:offs[i + 1]] for i in range(len(col_sizes))]
    pad = jnp.zeros((D, LANES - GATE_LANES - IDX_HEADS), f32)
    w_main = jnp.concatenate(
        [w_qa, w_ca, w_qi, jnp.tile(w_ki, (1, IDX_HEADS)), w_qb, w_kvb, w_gb, w_wi, pad], axis=1).astype(MXU_DTYPE)
    assert w_main.shape[1] == _W_MAIN
    w_gate = w_gm.astype(MXU_DTYPE)
    w_ukv = jnp.concatenate([w_uk_a, w_uv_a], axis=1).astype(MXU_DTYPE)
    row = lambda g: g.reshape(1, -1).astype(f32)
    gk_a = jnp.concatenate([row(g_k_a), jnp.ones((1, HEAD_DIM), f32)], axis=1)
    gks2 = jnp.tile(row(g_ks_b), (1, NSA_KV_HEADS))
    gkw2 = jnp.tile(row(g_kw_b), (1, NSA_KV_HEADS))

    x2 = x.reshape(T, D)
    qa, kva, qi, k8, qb, kvb, gw = _proj(x2, row(g_mix), w_main, row(g_lat_a), w_ukv, gk_a, gks2, gkw2, tm=256)
    seq = lambda a: a.reshape(B, S, a.shape[-1])

    n_chunk = S // CMP_STRIDE
    n_cmp = (S - CMP_LEN) // CMP_STRIDE + 1
    xc = seq(kvb)[:, :, :2 * NSA_KV_W].reshape(B, n_chunk, CMP_STRIDE, 2 * NSA_KV_HEADS, HEAD_DIM)
    xc = xc.transpose(0, 3, 1, 2, 4).reshape(B, 2 * NSA_KV_HEADS, n_chunk, CMP_STRIDE * HEAD_DIM)
    pe1 = pe_cmp_b[:CMP_STRIDE].reshape(1, -1)
    pe2 = pe_cmp_b[CMP_STRIDE:].reshape(1, -1)
    phi1 = jnp.stack([phi_k1_b, phi_v1_b]).astype(MXU_DTYPE)
    phi2 = jnp.stack([phi_k2_b, phi_v2_b]).astype(MXU_DTYPE)
    kcv = _compress(xc, pe1, pe2, phi1, phi2, row(g_kc_b))

    o_a = _dsa(seq(qa), seq(qi), seq(gw), seq(k8), seq(kva), row(g_q_a))
    o_b = _nsa(seq(qb), seq(gw), seq(kvb), kcv, row(g_q_b), n_cmp)

    x1 = _merge(x2, o_a.reshape(T, WIDTH_A), o_b.reshape(T, WIDTH_B), row(g_mix), w_gate,
                w_up_a.astype(MXU_DTYPE), w_up_b.astype(MXU_DTYPE), w_out.astype(MXU_DTYPE), tm=256)
    out = _ffn(x1, row(g_mlp), w_ff1.astype(MXU_DTYPE), w_ff2.astype(MXU_DTYPE), tm=256)
    return out.reshape(B, S, D)


def kernel(x, g_mix, w_in, g_q_a, g_k_a, g_lat_a, w_uk_a, w_uv_a, g_q_b, g_kc_b, g_ks_b, g_kw_b, pe_cmp_b,
           phi_k1_b, phi_k2_b, phi_v1_b, phi_v2_b, w_up_a, w_up_b, w_out, g_mlp, w_ff1, w_ff2):
    params = (g_mix, w_in, g_q_a, g_k_a, g_lat_a, w_uk_a, w_uv_a, g_q_b, g_kc_b, g_ks_b, g_kw_b, pe_cmp_b,
              phi_k1_b, phi_k2_b, phi_v1_b, phi_v2_b, w_up_a, w_up_b, w_out, g_mlp, w_ff1, w_ff2)
    for l in range(g_mix.shape[0]):
        x = _layer(x, *[p[l] for p in params])
    return x
```

```python
import functools
import math

import numpy as np
import jax
import jax.numpy as jnp
from jax import lax
from jax.experimental import pallas as pl
from jax.experimental.pallas import tpu as pltpu

HEAD_DIM = 64
DSA_HEADS = 8
DSA_LATENT = 128
IDX_HEADS = 8
IDX_DIM = 32
DSA_TOPK_MAX = 256
NSA_HEADS = 8
NSA_KV_HEADS = 2
NSA_GROUP = NSA_HEADS // NSA_KV_HEADS
CMP_LEN = 32
CMP_STRIDE = 16
CMP_HIDDEN = 128
SLC_LEN = 64
SLC_TOPN = 16
WINDOW = 512
FORCE_BONUS = 1e6
Q_BLOCK = 128
RMS_EPS = 1e-6
NEG_INF = -1e30

WIDTH_A = DSA_HEADS * HEAD_DIM
WIDTH_B = NSA_HEADS * HEAD_DIM
NSA_KV_W = NSA_KV_HEADS * HEAD_DIM

LANES = 128
GATE_ROWS = 3 * NSA_HEADS
ALIBI_COL = HEAD_DIM
BIAS_COL = 96
MAX_SLC_BLOCKS = LANES - BIAS_COL
BLOCK_BIAS = 2.0 ** 100
LOG2E = math.log2(math.e)
MXU_DTYPE = jnp.bfloat16
VMEM_LIMIT = 52 * 1024 * 1024
N_WIDTH_CLASSES = 4
TIE_CHUNK = 256
REDUCE_ROWS = 64

_NT = (((1,), (1,)), ((), ()))


def _alibi_slopes(n_heads):
    return [float(v) for v in np.asarray(
        2.0 ** (-8.0 * np.arange(1, n_heads + 1) / n_heads), dtype=np.float32)]


def _dot(a, b):
    return jnp.dot(a, b, preferred_element_type=jnp.float32)


def _dot_nt(a, b):
    return lax.dot_general(a, b, _NT, preferred_element_type=jnp.float32)


def _rms(x, g):
    ms = jnp.mean(x * x, axis=-1, keepdims=True)
    return x * lax.rsqrt(ms + RMS_EPS) * g


def _params(sem):
    return pltpu.CompilerParams(dimension_semantics=sem, vmem_limit_bytes=VMEM_LIMIT)


def _split3(c):
    c1 = c.astype(MXU_DTYPE).astype(jnp.float32)
    c2 = (c - c1).astype(MXU_DTYPE).astype(jnp.float32)
    c3 = (c - c1 - c2).astype(MXU_DTYPE).astype(jnp.float32)
    return c1, c2, c3


_T_KI = (0, 128)
_T_CA = (128, 256)
_T_KCV = (256, 768)
_T_KS = (768, 1024)
_T_KW = (1024, 1280)
_T_COLS = 1280
_F_QI = (0, 256)
_F_GW = (256, 384)
_F_QA = (384, 896)
_F_QB = (896, 1408)
_F_VS = (1408, 1664)
_F_VW = (1664, 1920)
_F_ROWS = 1920


def _proj_kernel(x_ref, gmix_ref, wtok_ref, wfeat_ref, glat_ref, wuk_ref, wuvt_ref, gka_ref, gks_ref, gkw_ref,
                 gqa_ref, gqb_ref,
                 ki_ref, ka_ref, vat_ref, kcv_ref, ks_ref, kw_ref, vst_ref, vwt_ref, qit_ref, gwt_ref,
                 qat_ref, qbt_ref, *, seq_len, w_idx_scale, slopes_a, slopes_b):
    f32 = jnp.float32
    x = x_ref[...]
    tm = x.shape[0]
    h = _rms(x, gmix_ref[...]).astype(MXU_DTYPE)

    def tok(c):
        return _dot(h, wtok_ref[:, c[0]:c[1]])

    def feat(r):
        return _dot_nt(wfeat_ref[r[0]:r[1], :], h)

    s0 = (pl.program_id(0) % (seq_len // tm)) * tm
    spos = s0 + lax.broadcasted_iota(jnp.int32, (tm, LANES), 0)
    lane = lax.broadcasted_iota(jnp.int32, (tm, LANES), 1)
    c1, c2, c3 = _split3(spos.astype(f32) * LOG2E)
    alibi_cols = jnp.where(lane == ALIBI_COL, c1,
                           jnp.where(lane == ALIBI_COL + 1, c2, jnp.where(lane == ALIBI_COL + 2, c3, 0.0)))
    block_onehot = jnp.where(lane - BIAS_COL == lax.shift_right_logical(spos, 6), 1.0, 0.0)

    def key_slab(raw, g):
        ms = jnp.sum(raw * raw, axis=-1, keepdims=True) * (1.0 / HEAD_DIM)
        return raw * lax.rsqrt(ms + RMS_EPS) * g

    ki_ref[...] = tok(_T_KI).astype(ki_ref.dtype)
    c = _rms(tok(_T_CA), glat_ref[...]).astype(MXU_DTYPE)
    ka_ref[...] = (key_slab(_dot(c, wuk_ref[...]), gka_ref[...]) + alibi_cols).astype(ka_ref.dtype)
    va_t = _dot_nt(wuvt_ref[...], c)
    frow = lax.broadcasted_iota(jnp.int32, va_t.shape, 0)
    vat_ref[...] = jnp.where(frow == HEAD_DIM, 1.0, va_t).astype(vat_ref.dtype)
    for j in range(2 * NSA_KV_HEADS):
        kcv_ref[j] = tok((_T_KCV[0] + j * LANES, _T_KCV[0] + (j + 1) * LANES))[:, :HEAD_DIM]
    for kh in range(NSA_KV_HEADS):
        sl = slice(kh * LANES, (kh + 1) * LANES)
        ks = key_slab(tok((_T_KS[0] + kh * LANES, _T_KS[0] + (kh + 1) * LANES)), gks_ref[...])
        ks_ref[:, sl] = (ks + alibi_cols + block_onehot).astype(ks_ref.dtype)
        kw = key_slab(tok((_T_KW[0] + kh * LANES, _T_KW[0] + (kh + 1) * LANES)), gkw_ref[...])
        kw_ref[:, sl] = (kw + alibi_cols).astype(kw_ref.dtype)

    qit_ref[...] = feat(_F_QI).astype(qit_ref.dtype)
    gw = feat(_F_GW)
    grow = lax.broadcasted_iota(jnp.int32, gw.shape, 0)
    gwt_ref[...] = jnp.where(grow < GATE_ROWS, jax.nn.sigmoid(gw), gw * w_idx_scale)
    erow = lax.broadcasted_iota(jnp.int32, (HEAD_DIM, tm), 0)
    q_scale = HEAD_DIM ** -0.5 * LOG2E
    for rows, g_ref, out_ref, slopes in ((_F_QA, gqa_ref, qat_ref, slopes_a), (_F_QB, gqb_ref, qbt_ref, slopes_b)):
        q_all = feat(rows)
        for hd in range(len(slopes)):
            q = q_all[hd * HEAD_DIM:(hd + 1) * HEAD_DIM]
            ms = jnp.mean(q * q, axis=0, keepdims=True)
            q = q * lax.rsqrt(ms + RMS_EPS) * g_ref[...] * q_scale
            out_ref[hd * LANES:hd * LANES + HEAD_DIM, :] = q.astype(out_ref.dtype)
            extra = jnp.where(erow < 3, slopes[hd], 0.0)
            out_ref[hd * LANES + HEAD_DIM:(hd + 1) * LANES, :] = extra.astype(out_ref.dtype)
    for rows, out_ref in ((_F_VS, vst_ref), (_F_VW, vwt_ref)):
        v_t = feat(rows)
        vrow = lax.broadcasted_iota(jnp.int32, v_t.shape, 0)
        out_ref[...] = jnp.where((vrow & (LANES - 1)) == HEAD_DIM, 1.0, v_t).astype(out_ref.dtype)


def _proj(x2, seq_len, g_mix, w_tok, w_feat, g_lat, w_uk, w_uv_t, gk_a, gks, gkw, gq_a, gq_b, tm):
    T, D = x2.shape
    tokm = lambda w: pl.BlockSpec((tm, w), lambda i: (i, 0))
    featm = lambda r: pl.BlockSpec((r, tm), lambda i: (0, i))
    full = lambda a: pl.BlockSpec(a.shape, lambda i: (0,) * a.ndim)
    f32 = jnp.float32
    mx = MXU_DTYPE
    out_shape = (
        jax.ShapeDtypeStruct((T, LANES), mx),
        jax.ShapeDtypeStruct((T, LANES), mx),
        jax.ShapeDtypeStruct((LANES, T), mx),
        jax.ShapeDtypeStruct((2 * NSA_KV_HEADS, T, HEAD_DIM), f32),
        jax.ShapeDtypeStruct((T, NSA_KV_HEADS * LANES), mx),
        jax.ShapeDtypeStruct((T, NSA_KV_HEADS * LANES), mx),
        jax.ShapeDtypeStruct((NSA_KV_HEADS * LANES, T), mx),
        jax.ShapeDtypeStruct((NSA_KV_HEADS * LANES, T), mx),
        jax.ShapeDtypeStruct((IDX_HEADS * IDX_DIM, T), mx),
        jax.ShapeDtypeStruct((LANES, T), f32),
        jax.ShapeDtypeStruct((DSA_HEADS * LANES, T), mx),
        jax.ShapeDtypeStruct((NSA_HEADS * LANES, T), mx),
    )
    out_specs = (
        tokm(LANES), tokm(LANES), featm(LANES),
        pl.BlockSpec((2 * NSA_KV_HEADS, tm, HEAD_DIM), lambda i: (0, i, 0)),
        tokm(NSA_KV_HEADS * LANES), tokm(NSA_KV_HEADS * LANES),
        featm(NSA_KV_HEADS * LANES), featm(NSA_KV_HEADS * LANES),
        featm(IDX_HEADS * IDX_DIM), featm(LANES), featm(DSA_HEADS * LANES), featm(NSA_HEADS * LANES),
    )
    kern = functools.partial(
        _proj_kernel, seq_len=seq_len, w_idx_scale=IDX_HEADS ** -0.5 * IDX_DIM ** -0.5,
        slopes_a=_alibi_slopes(DSA_HEADS), slopes_b=_alibi_slopes(NSA_HEADS))
    ins = (x2, g_mix, w_tok, w_feat, g_lat, w_uk, w_uv_t, gk_a, gks, gkw, gq_a, gq_b)
    return pl.pallas_call(
        kern,
        grid=(T // tm,),
        in_specs=[tokm(D)] + [full(a) for a in ins[1:]],
        out_specs=out_specs,
        out_shape=out_shape,
        compiler_params=_params(("parallel",)),
        name="proj",
    )(*ins)


def _compress_kernel(xk_ref, xv_ref, pe1_ref, pe2_ref, w1_ref, w2k_ref, w2vt_ref, gkc_ref, kc_ref, vct_ref):
    def hidden(x, w1):
        n, half = x.shape
        a = _dot((x + pe1_ref[...]).astype(MXU_DTYPE), w1[:half, :])
        b = _dot((x + pe2_ref[...]).astype(MXU_DTYPE), w1[half:, :])
        return jax.nn.gelu(a + pltpu.roll(b, n - 1, 0)).astype(MXU_DTYPE)

    yk = _dot(hidden(xk_ref[...], w1_ref[0]), w2k_ref[...])
    ms = jnp.sum(yk * yk, axis=-1, keepdims=True) * (1.0 / HEAD_DIM)
    kc_ref[...] = (yk * lax.rsqrt(ms + RMS_EPS) * gkc_ref[...]).astype(kc_ref.dtype)
    vct_ref[...] = _dot_nt(w2vt_ref[...], hidden(xv_ref[...], w1_ref[1])).astype(vct_ref.dtype)


def _compress(xc, pe1, pe2, phi1, w2k, w2v_t, g_kc):
    _, B, n, w = xc.shape
    full = lambda a: pl.BlockSpec(a.shape, lambda b, kh: (0,) * a.ndim)
    return pl.pallas_call(
        _compress_kernel,
        grid=(B, NSA_KV_HEADS),
        in_specs=[
            pl.BlockSpec((None, None, n, w), lambda b, kh: (kh, b, 0, 0)),
            pl.BlockSpec((None, None, n, w), lambda b, kh: (NSA_KV_HEADS + kh, b, 0, 0)),
            full(pe1), full(pe2), full(phi1), full(w2k), full(w2v_t), full(g_kc),
        ],
        out_specs=(pl.BlockSpec((None, None, n, LANES), lambda b, kh: (b, kh, 0, 0)),
                   pl.BlockSpec((None, None, LANES, n), lambda b, kh: (b, kh, 0, 0))),
        out_shape=(jax.ShapeDtypeStruct((B, NSA_KV_HEADS, n, LANES), MXU_DTYPE),
                   jax.ShapeDtypeStruct((B, NSA_KV_HEADS, LANES, n), MXU_DTYPE)),
        compiler_params=_params(("parallel", "parallel")),
        name="compress",
    )(xc, xc, pe1, pe2, phi1, w2k, w2v_t, g_kc)


_INT_MIN = -2 ** 31


def _key_to_float(u):
    key = u ^ jnp.int32(_INT_MIN)
    bits = jnp.where(key >= 0, key, key ^ jnp.int32(0x7FFFFFFF))
    return lax.bitcast_convert_type(bits, jnp.float32)


def _col_reduce(x, op, final):
    rows = REDUCE_ROWS
    if x.shape[0] <= rows or x.shape[0] % rows:
        return final(x, axis=0, keepdims=True)
    acc = x[:rows]
    for j in range(1, x.shape[0] // rows):
        acc = op(acc, x[j * rows:(j + 1) * rows])
    return final(acc, axis=0, keepdims=True)


def _col_sum(x):
    return _col_reduce(x, jnp.add, jnp.sum)


def _col_max(x):
    return _col_reduce(x, jnp.maximum, jnp.max)


def _kth_largest(score, k):
    def body(it, u):
        u_try = u | lax.shift_left(jnp.int32(1), 31 - it)
        thr = _key_to_float(u_try)
        cnt = _col_sum(jnp.where(score >= thr, 1.0, 0.0))
        return jnp.where(cnt >= k, u_try, u)

    u = lax.fori_loop(0, 32, body, jnp.zeros((1, score.shape[1]), jnp.int32))
    thr = _key_to_float(u)
    return jnp.where(thr != thr, -jnp.inf, thr)


def _lane_tile(x, n):
    return jnp.concatenate([x] * n, axis=1)


def _softmax_pv(s, valid, v_t):
    s = jnp.where(valid, s, NEG_INF)
    e = jnp.exp2(s - _col_max(s))
    return _dot(v_t, e.astype(MXU_DTYPE))


def _width_classes(n_qblk, seq_len):
    n_cls = N_WIDTH_CLASSES if n_qblk % N_WIDTH_CLASSES == 0 else 1
    per = n_qblk // n_cls
    return [(c * per, (c + 1) * per, (c + 1) * per * Q_BLOCK) for c in range(n_cls)]


def _dsa_body(W, qit_ref, gwt_ref, qat_ref, ki_ref, ka_ref, vat_ref, o_ref, sel_ref, *, topk):
    f32 = jnp.float32
    QB = Q_BLOCK
    t = pl.program_id(1) * QB + lax.broadcasted_iota(jnp.int32, (1, QB), 1)
    spos = lax.broadcasted_iota(jnp.int32, (W, QB), 0)
    causal = spos <= t

    ki = ki_ref[:W, :]
    G = 4
    kpad = jnp.zeros((LANES - IDX_DIM, G * QB), MXU_DTYPE)
    score = jnp.zeros((W, QB), f32)
    for hg in range(IDX_HEADS // G):
        q_idx = jnp.concatenate(
            [qit_ref[(hg * G + g) * IDX_DIM:(hg * G + g + 1) * IDX_DIM, :] for g in range(G)], axis=1)
        logits = _dot(ki, jnp.concatenate([q_idx, kpad], axis=0))
        for g in range(G):
            w_h = gwt_ref[GATE_ROWS + hg * G + g:GATE_ROWS + hg * G + g + 1, :]
            score = score + jnp.maximum(logits[:, g * QB:(g + 1) * QB], 0.0) * w_h
    score = jnp.where(causal, score, -jnp.inf)

    thr = _kth_largest(score, float(topk))
    gt = score > thr
    eq = (score == thr) & causal
    need = float(topk) - _col_sum(jnp.where(gt, 1.0, 0.0))
    n_eq = _col_sum(jnp.where(eq, 1.0, 0.0))
    sel_ref[:W, :] = jnp.where(gt | eq, 1.0, 0.0)

    @pl.when(jnp.max(n_eq - need) > 0.0)
    def _():
        cw = TIE_CHUNK
        r = lax.broadcasted_iota(jnp.int32, (cw, cw), 0)
        c = lax.broadcasted_iota(jnp.int32, (cw, cw), 1)
        lower = jnp.where(c <= r, 1.0, 0.0).astype(MXU_DTYPE)
        carry = jnp.zeros((1, QB), f32)
        for j in range(W // cw):
            sl = slice(j * cw, (j + 1) * cw)
            eq_j = jnp.where(eq[sl], 1.0, 0.0)
            prefix = _dot(lower, eq_j.astype(MXU_DTYPE)) + carry
            keep = gt[sl] | (eq[sl] & (prefix <= need))
            sel_ref[sl, :] = jnp.where(keep, 1.0, 0.0)
            carry = carry + jnp.sum(eq_j, axis=0, keepdims=True)

    G = 4
    sel = _lane_tile(sel_ref[:W, :] > 0.5, G)
    ka = ka_ref[:W, :]
    va_t = vat_ref[:, :W]
    for hg in range(DSA_HEADS // G):
        q = jnp.concatenate([qat_ref[(hg * G + g) * LANES:(hg * G + g + 1) * LANES, :] for g in range(G)], axis=1)
        o_t = _softmax_pv(_dot(ka, q), sel, va_t)
        o_t = o_t[:HEAD_DIM] * (1.0 / o_t[HEAD_DIM:HEAD_DIM + 1])
        for g2 in range(G // 2):
            pair = jnp.concatenate([o_t[:, (2 * g2) * QB:(2 * g2 + 1) * QB],
                                    o_t[:, (2 * g2 + 1) * QB:(2 * g2 + 2) * QB]], axis=0)
            col = (hg * G + 2 * g2) * HEAD_DIM
            o_ref[:, col:col + 2 * HEAD_DIM] = pair.T.astype(o_ref.dtype)


def _dsa_kernel(qit_ref, gwt_ref, qat_ref, ki_ref, ka_ref, vat_ref, o_ref, sel_ref, *, topk, classes):
    qblk = pl.program_id(1)
    for lo, hi, W in classes:
        @pl.when((qblk >= lo) & (qblk < hi))
        def _(W=W):
            _dsa_body(W, qit_ref, gwt_ref, qat_ref, ki_ref, ka_ref, vat_ref, o_ref, sel_ref, topk=topk)


def _dsa(B, S, qi_t, gw_t, qa_t, ki, ka, va_t):
    nq = S // Q_BLOCK
    topk = min(DSA_TOPK_MAX, S // 4)
    fblk = lambda r: pl.BlockSpec((r, Q_BLOCK), lambda b, i: (0, b * nq + i))
    tseq = lambda w: pl.BlockSpec((S, w), lambda b, i: (b, 0))
    fseq = lambda r: pl.BlockSpec((r, S), lambda b, i: (0, b))
    return pl.pallas_call(
        functools.partial(_dsa_kernel, topk=topk, classes=_width_classes(nq, S)),
        grid=(B, nq),
        in_specs=[fblk(IDX_HEADS * IDX_DIM), fblk(LANES), fblk(DSA_HEADS * LANES),
                  tseq(LANES), tseq(LANES), fseq(LANES)],
        out_specs=pl.BlockSpec((Q_BLOCK, WIDTH_A), lambda b, i: (b * nq + i, 0)),
        out_shape=jax.ShapeDtypeStruct((B * S, WIDTH_A), MXU_DTYPE),
        scratch_shapes=[pltpu.VMEM((S, Q_BLOCK), jnp.float32)],
        compiler_params=_params(("parallel", "arbitrary")),
        name="dsa",
    )(qi_t, gw_t, qa_t, ki, ka, va_t)


def _nsa_body(W, gwt_ref, qbt_ref, ks_ref, kw_ref, vst_ref, vwt_ref, kc_ref, vct_ref, ovl_ref, o_ref,
              *, seq_len):
    f32 = jnp.float32
    QB, G = Q_BLOCK, NSA_GROUP
    n_slc = seq_len // SLC_LEN
    n_top = min(SLC_TOPN, n_slc)
    win_len = WINDOW + QB
    nc = kc_ref.shape[1]

    t0 = pl.program_id(1) * QB
    t = t0 + lax.broadcasted_iota(jnp.int32, (1, QB), 1)
    t4 = _lane_tile(t, G)
    valid_c = lax.broadcasted_iota(jnp.int32, (nc, G * QB), 0) * CMP_STRIDE + (CMP_LEN - 1) <= t4
    causal = lax.broadcasted_iota(jnp.int32, (W, G * QB), 0) <= t4
    w0 = pl.multiple_of(jnp.maximum(t0 - WINDOW, 0), QB)
    wd = t4 - (w0 + lax.broadcasted_iota(jnp.int32, (win_len, G * QB), 0))
    wvalid = (wd >= 0) & (wd < WINDOW)
    jblk = lax.broadcasted_iota(jnp.int32, (MAX_SLC_BLOCKS, QB), 0)
    cur = lax.shift_right_logical(t, 6)
    future = jblk > cur
    forced = (jblk == 0) | (jblk == cur) | (jblk == cur - 1)
    ovl = ovl_ref[...]

    for kh in range(NSA_KV_HEADS):
        hs = [kh * G + g for g in range(G)]
        q = jnp.concatenate([qbt_ref[h * LANES:(h + 1) * LANES, :] for h in hs], axis=1)

        s_c = jnp.where(valid_c, _dot(kc_ref[kh], q), NEG_INF)
        m_c = jnp.max(s_c, axis=0, keepdims=True)
        e_c = jnp.where(valid_c, jnp.exp2(s_c - m_c), 0.0)
        l_c = jnp.sum(e_c, axis=0, keepdims=True)
        p_c = e_c * (1.0 / jnp.where(l_c > 0.0, l_c, 1.0))
        o_cmp = _dot(vct_ref[kh], p_c.astype(MXU_DTYPE))

        p_sum = p_c[:, 0:QB] + p_c[:, QB:2 * QB] + p_c[:, 2 * QB:3 * QB] + p_c[:, 3 * QB:4 * QB]
        p_hi = p_sum.astype(MXU_DTYPE)
        p_lo = (p_sum - p_hi.astype(f32)).astype(MXU_DTYPE)
        imp = _dot(ovl, p_hi) + _dot(ovl, p_lo)
        val = jnp.where(future, -jnp.inf, imp + jnp.where(forced, FORCE_BONUS, 0.0))
        rank = jnp.zeros(val.shape, f32)
        for i in range(n_slc):
            vi = val[i:i + 1, :]
            rank = rank + jnp.where((vi > val) | ((vi == val) & (jblk > i)), 1.0, 0.0)
        bias = jnp.where((rank < n_top) & (jblk < n_slc), 0.0, -BLOCK_BIAS).astype(MXU_DTYPE)
        q_sel = jnp.concatenate([q[:BIAS_COL], _lane_tile(bias, G)], axis=0)

        ksl = slice(kh * LANES, (kh + 1) * LANES)
        o_slc = _softmax_pv(_dot(ks_ref[:W, ksl], q_sel), causal, vst_ref[ksl, :W])

        k_w = kw_ref[pl.ds(w0, win_len), ksl]
        v_w = vwt_ref[ksl, pl.ds(w0, win_len)]
        o_win = _softmax_pv(_dot(k_w, q), wvalid, v_w)

        inv_s = 1.0 / o_slc[HEAD_DIM:HEAD_DIM + 1]
        inv_w = 1.0 / o_win[HEAD_DIM:HEAD_DIM + 1]
        heads = []
        for g in range(G):
            h = hs[g]
            ls = slice(g * QB, (g + 1) * QB)
            g0 = gwt_ref[3 * h + 0:3 * h + 1, :]
            g1 = gwt_ref[3 * h + 1:3 * h + 2, :] * inv_s[:, ls]
            g2 = gwt_ref[3 * h + 2:3 * h + 3, :] * inv_w[:, ls]
            heads.append(g0 * o_cmp[:HEAD_DIM, ls] + g1 * o_slc[:HEAD_DIM, ls] + g2 * o_win[:HEAD_DIM, ls])
        for g2_ in range(G // 2):
            pair = jnp.concatenate([heads[2 * g2_], heads[2 * g2_ + 1]], axis=0)
            col = (kh * G + 2 * g2_) * HEAD_DIM
            o_ref[:, col:col + 2 * HEAD_DIM] = pair.T.astype(o_ref.dtype)


def _nsa_kernel(gwt_ref, qbt_ref, ks_ref, kw_ref, vst_ref, vwt_ref, kc_ref, vct_ref, ovl_ref, o_ref,
                *, seq_len, classes):
    qblk = pl.program_id(1)
    for lo, hi, W in classes:
        @pl.when((qblk >= lo) & (qblk < hi))
        def _(W=W):
            _nsa_body(W, gwt_ref, qbt_ref, ks_ref, kw_ref, vst_ref, vwt_ref, kc_ref, vct_ref, ovl_ref, o_ref,
                      seq_len=seq_len)


def _nsa(B, S, gw_t, qb_t, ks, kw, vs_t, vw_t, kc, vc_t, ovl):
    nq = S // Q_BLOCK
    nc = kc.shape[2]
    kvw = NSA_KV_HEADS * LANES
    fblk = lambda r: pl.BlockSpec((r, Q_BLOCK), lambda b, i: (0, b * nq + i))
    tseq = lambda w: pl.BlockSpec((S, w), lambda b, i: (b, 0))
    fseq = lambda r: pl.BlockSpec((r, S), lambda b, i: (0, b))
    return pl.pallas_call(
        functools.partial(_nsa_kernel, seq_len=S, classes=_width_classes(nq, S)),
        grid=(B, nq),
        in_specs=[fblk(LANES), fblk(NSA_HEADS * LANES), tseq(kvw), tseq(kvw), fseq(kvw), fseq(kvw),
                  pl.BlockSpec((None, NSA_KV_HEADS, nc, LANES), lambda b, i: (b, 0, 0, 0)),
                  pl.BlockSpec((None, NSA_KV_HEADS, LANES, nc), lambda b, i: (b, 0, 0, 0)),
                  pl.BlockSpec(ovl.shape, lambda b, i: (0, 0))],
        out_specs=pl.BlockSpec((Q_BLOCK, WIDTH_B), lambda b, i: (b * nq + i, 0)),
        out_shape=jax.ShapeDtypeStruct((B * S, WIDTH_B), MXU_DTYPE),
        compiler_params=_params(("parallel", "arbitrary")),
        name="nsa",
    )(gw_t, qb_t, ks, kw, vs_t, vw_t, kc, vc_t, ovl)


def _merge_kernel(x_ref, oa_ref, ob_ref, gmix_ref, wg_ref, wua_ref, wub_ref, wo_ref, o_ref):
    x = x_ref[...]
    D = x.shape[1]
    h = _rms(x, gmix_ref[...]).astype(MXU_DTYPE)
    ga = jax.nn.sigmoid(_dot(h, wg_ref[:, :D]))
    gb = jax.nn.sigmoid(_dot(h, wg_ref[:, D:]))
    merged = ga * _dot(oa_ref[...], wua_ref[...]) + gb * _dot(ob_ref[...], wub_ref[...])
    o_ref[...] = x + _dot(merged.astype(MXU_DTYPE), wo_ref[...])


def _merge(x2, oa, ob, g_mix, w_gate, w_up_a, w_up_b, w_out, tm):
    T, D = x2.shape
    row = lambda w: pl.BlockSpec((tm, w), lambda i: (i, 0))
    full = lambda a: pl.BlockSpec(a.shape, lambda i: (0,) * a.ndim)
    return pl.pallas_call(
        _merge_kernel,
        grid=(T // tm,),
        in_specs=[row(D), row(WIDTH_A), row(WIDTH_B), full(g_mix), full(w_gate),
                  full(w_up_a), full(w_up_b), full(w_out)],
        out_specs=row(D),
        out_shape=jax.ShapeDtypeStruct((T, D), jnp.float32),
        compiler_params=_params(("parallel",)),
        name="merge",
    )(x2, oa, ob, g_mix, w_gate, w_up_a, w_up_b, w_out)


def _ffn_kernel(x_ref, g_ref, w1_ref, w2_ref, o_ref, *, chunk):
    x = x_ref[...]
    h = _rms(x, g_ref[...]).astype(MXU_DTYPE)
    acc = x
    for c in range(w1_ref.shape[1] // chunk):
        u = jnp.maximum(_dot(h, w1_ref[:, c * chunk:(c + 1) * chunk]), 0.0)
        acc = acc + _dot((u * u).astype(MXU_DTYPE), w2_ref[c * chunk:(c + 1) * chunk, :])
    o_ref[...] = acc


def _ffn(x2, g_mlp, w1, w2, tm):
    T, D = x2.shape
    row = pl.BlockSpec((tm, D), lambda i: (i, 0))
    full = lambda a: pl.BlockSpec(a.shape, lambda i: (0,) * a.ndim)
    return pl.pallas_call(
        functools.partial(_ffn_kernel, chunk=1024),
        grid=(T // tm,),
        in_specs=[row, full(g_mlp), full(w1), full(w2)],
        out_specs=row,
        out_shape=jax.ShapeDtypeStruct((T, D), jnp.float32),
        compiler_params=_params(("parallel",)),
        name="ffn",
    )(x2, g_mlp, w1, w2)


def _pad_heads(w, n_heads, axis):
    shape = list(w.shape)
    shape[axis:axis + 1] = [n_heads, HEAD_DIM]
    w = w.reshape(shape)
    pad = [(0, 0)] * w.ndim
    pad[axis + 1] = (0, LANES - HEAD_DIM)
    w = jnp.pad(w, pad)
    shape[axis:axis + 2] = [n_heads * LANES]
    return w.reshape(shape)


def _layer(x, g_mix, w_in, g_q_a, g_k_a, g_lat_a, w_uk_a, w_uv_a, g_q_b, g_kc_b, g_ks_b, g_kw_b,
           pe_cmp_b, phi_k1_b, phi_k2_b, phi_v1_b, phi_v2_b, w_up_a, w_up_b, w_out, g_mlp, w_ff1, w_ff2):
    B, S, D = x.shape
    T = B * S
    f32 = jnp.float32
    mx = MXU_DTYPE
    n_slc = S // SLC_LEN
    assert n_slc <= MAX_SLC_BLOCKS and S % (N_WIDTH_CLASSES * TIE_CHUNK) == 0 and S >= WINDOW + Q_BLOCK
    col_sizes = (WIDTH_A, DSA_LATENT, IDX_HEADS * IDX_DIM, IDX_DIM, IDX_HEADS,
                 WIDTH_B, 6 * NSA_KV_W, 3 * NSA_HEADS, 2 * D)
    offs = np.cumsum((0,) + col_sizes)
    w_qa, w_ca, w_qi, w_ki, w_wi, w_qb, w_kvb, w_gb, w_gm = [
        w_in[:, offs[i]:offs[i + 1]] for i in range(len(col_sizes))]
    w_kc, w_vc, w_ks, w_vs, w_kw, w_vw = [w_kvb[:, j * NSA_KV_W:(j + 1) * NSA_KV_W] for j in range(6)]
    padh = lambda w: _pad_heads(w, NSA_KV_HEADS, 1)
    w_tok = jnp.concatenate(
        [w_ki, jnp.zeros((D, LANES - IDX_DIM), f32), w_ca, padh(w_kc), padh(w_vc), padh(w_ks), padh(w_kw)], axis=1).astype(mx)
    assert w_tok.shape[1] == _T_COLS
    gw_pad = jnp.zeros((D, LANES - GATE_ROWS - IDX_HEADS), f32)
    w_feat = jnp.concatenate(
        [w_qi, w_gb, w_wi, gw_pad, w_qa, w_qb, padh(w_vs), padh(w_vw)], axis=1).T.astype(mx)
    assert w_feat.shape[0] == _F_ROWS
    w_gate = w_gm.astype(mx)
    zpad = jnp.zeros((DSA_LATENT, LANES - HEAD_DIM), f32)
    w_uk = jnp.concatenate([w_uk_a, zpad], axis=1).astype(mx)
    w_uv_t = jnp.concatenate([w_uv_a, zpad], axis=1).T.astype(mx)
    row = lambda g: g.reshape(1, -1).astype(f32)
    rpad = lambda g: jnp.concatenate([row(g), jnp.zeros((1, LANES - HEAD_DIM), f32)], axis=1)
    colv = lambda g: g.reshape(-1, 1).astype(f32)

    x2 = x.reshape(T, D)
    (ki, ka, va_t, kcv, ks, kw, vs_t, vw_t, qi_t, gw_t, qa_t, qb_t) = _proj(
        x2, S, row(g_mix), w_tok, w_feat, row(g_lat_a), w_uk, w_uv_t, rpad(g_k_a), rpad(g_ks_b), rpad(g_kw_b),
        colv(g_q_a), colv(g_q_b), tm=256)

    n_chunk = S // CMP_STRIDE
    n_cmp = (S - CMP_LEN) // CMP_STRIDE + 1
    xc = kcv.reshape(2 * NSA_KV_HEADS, B, n_chunk, CMP_STRIDE * HEAD_DIM)
    pe1 = pe_cmp_b[:CMP_STRIDE].reshape(1, -1)
    pe2 = pe_cmp_b[CMP_STRIDE:].reshape(1, -1)
    phi1 = jnp.stack([phi_k1_b, phi_v1_b]).astype(mx)
    hpad = jnp.zeros((CMP_HIDDEN, LANES - HEAD_DIM), f32)
    w2k = jnp.concatenate([phi_k2_b, hpad], axis=1).astype(mx)
    w2v_t = jnp.concatenate([phi_v2_b, hpad], axis=1).T.astype(mx)
    kc, vc_t = _compress(xc, pe1, pe2, phi1, w2k, w2v_t, rpad(g_kc_b))

    c_i = np.arange(n_chunk)[None, :] * CMP_STRIDE
    j_i = np.arange(MAX_SLC_BLOCKS)[:, None] * SLC_LEN
    ovl = (c_i < j_i + SLC_LEN) & (c_i + CMP_LEN > j_i) & (c_i < n_cmp * CMP_STRIDE) & (j_i < S)
    ovl = jnp.asarray(ovl, f32).astype(mx)

    o_a = _dsa(B, S, qi_t, gw_t, qa_t, ki, ka, va_t)
    o_b = _nsa(B, S, gw_t, qb_t, ks, kw, vs_t, vw_t, kc, vc_t, ovl)

    x1 = _merge(x2, o_a, o_b, row(g_mix), w_gate,
                w_up_a.astype(mx), w_up_b.astype(mx), w_out.astype(mx), tm=256)
    out = _ffn(x1, row(g_mlp), w_ff1.astype(mx), w_ff2.astype(mx), tm=256)
    return out.reshape(B, S, D)


def kernel(x, g_mix, w_in, g_q_a, g_k_a, g_lat_a, w_uk_a, w_uv_a, g_q_b, g_kc_b, g_ks_b, g_kw_b, pe_cmp_b,
           phi_k1_b, phi_k2_b, phi_v1_b, phi_v2_b, w_up_a, w_up_b, w_out, g_mlp, w_ff1, w_ff2):
    params = (g_mix, w_in, g_q_a, g_k_a, g_lat_a, w_uk_a, w_uv_a, g_q_b, g_kc_b, g_ks_b, g_kw_b, pe_cmp_b,
              phi_k1_b, phi_k2_b, phi_v1_b, phi_v2_b, w_up_a, w_up_b, w_out, g_mlp, w_ff1, w_ff2)
    for l in range(g_mix.shape[0]):
        x = _layer(x, *[p[l] for p in params])
    return x
```

```python
import functools
import math

import numpy as np
import jax
import jax.numpy as jnp
from jax import lax
from jax.experimental import pallas as pl
from jax.experimental.pallas import tpu as pltpu

HEAD_DIM = 64
DSA_HEADS = 8
DSA_LATENT = 128
IDX_HEADS = 8
IDX_DIM = 32
DSA_TOPK_MAX = 256
NSA_HEADS = 8
NSA_KV_HEADS = 2
NSA_GROUP = NSA_HEADS // NSA_KV_HEADS
CMP_LEN = 32
CMP_STRIDE = 16
CMP_HIDDEN = 128
SLC_LEN = 64
SLC_TOPN = 16
WINDOW = 512
FORCE_BONUS = 1e6
Q_BLOCK = 128
RMS_EPS = 1e-6
NEG_INF = -1e30

WIDTH_A = DSA_HEADS * HEAD_DIM
WIDTH_B = NSA_HEADS * HEAD_DIM
NSA_KV_W = NSA_KV_HEADS * HEAD_DIM

LANES = 128
GATE_ROWS = 3 * NSA_HEADS
ALIBI_COL = HEAD_DIM
BIAS_COL = 96
MAX_SLC_BLOCKS = LANES - BIAS_COL
BLOCK_BIAS = 2.0 ** 100
LOG2E = math.log2(math.e)
MXU_DTYPE = jnp.bfloat16
VMEM_LIMIT = 52 * 1024 * 1024
N_WIDTH_CLASSES = 8
TIE_CHUNK = 256
REDUCE_ROWS = 64

_NT = (((1,), (1,)), ((), ()))


def _alibi_slopes(n_heads):
    return [float(v) for v in np.asarray(
        2.0 ** (-8.0 * np.arange(1, n_heads + 1) / n_heads), dtype=np.float32)]


def _dot(a, b):
    return jnp.dot(a, b, preferred_element_type=jnp.float32)


def _dot_nt(a, b):
    return lax.dot_general(a, b, _NT, preferred_element_type=jnp.float32)


def _rms(x, g):
    ms = jnp.mean(x * x, axis=-1, keepdims=True)
    return x * lax.rsqrt(ms + RMS_EPS) * g


def _params(sem):
    return pltpu.CompilerParams(dimension_semantics=sem, vmem_limit_bytes=VMEM_LIMIT)


def _split3(c):
    c1 = c.astype(MXU_DTYPE).astype(jnp.float32)
    c2 = (c - c1).astype(MXU_DTYPE).astype(jnp.float32)
    c3 = (c - c1 - c2).astype(MXU_DTYPE).astype(jnp.float32)
    return c1, c2, c3


_T_KI = (0, 128)
_T_CA = (128, 256)
_T_KCV = (256, 768)
_T_KS = (768, 1024)
_T_KW = (1024, 1280)
_T_COLS = 1280
_F_QI = (0, 256)
_F_GW = (256, 384)
_F_QA = (384, 896)
_F_QB = (896, 1408)
_F_VS = (1408, 1536)
_F_VW = (1536, 1664)
_F_ROWS = 1664


def _proj_kernel(x_ref, gmix_ref, wtok_ref, wfeat_ref, glat_ref, wuk_ref, wuvt_ref, gka_ref, gks_ref, gkw_ref,
                 gqa_ref, gqb_ref,
                 ki_ref, ka_ref, vat_ref, kcv_ref, ks_ref, kw_ref, vst_ref, vwt_ref, qit_ref, gwt_ref,
                 qat_ref, qbt_ref, *, seq_len, w_idx_scale, slopes_a, slopes_b):
    f32 = jnp.float32
    x = x_ref[...]
    tm = x.shape[0]
    h = _rms(x, gmix_ref[...]).astype(MXU_DTYPE)

    def tok(c):
        return _dot(h, wtok_ref[:, c[0]:c[1]])

    def feat(r):
        return _dot_nt(wfeat_ref[r[0]:r[1], :], h)

    s0 = (pl.program_id(0) % (seq_len // tm)) * tm
    spos = s0 + lax.broadcasted_iota(jnp.int32, (tm, LANES), 0)
    lane = lax.broadcasted_iota(jnp.int32, (tm, LANES), 1)
    c1, c2, c3 = _split3(spos.astype(f32) * LOG2E)
    alibi_cols = jnp.where(lane == ALIBI_COL, c1,
                           jnp.where(lane == ALIBI_COL + 1, c2, jnp.where(lane == ALIBI_COL + 2, c3, 0.0)))
    block_onehot = jnp.where(lane - BIAS_COL == lax.shift_right_logical(spos, 6), 1.0, 0.0)

    def key_slab(raw, g):
        ms = jnp.sum(raw * raw, axis=-1, keepdims=True) * (1.0 / HEAD_DIM)
        return raw * lax.rsqrt(ms + RMS_EPS) * g

    ki_ref[...] = tok(_T_KI).astype(ki_ref.dtype)
    c = _rms(tok(_T_CA), glat_ref[...]).astype(MXU_DTYPE)
    ka_ref[...] = (key_slab(_dot(c, wuk_ref[...]), gka_ref[...]) + alibi_cols).astype(ka_ref.dtype)
    va_t = _dot_nt(wuvt_ref[...], c)
    frow = lax.broadcasted_iota(jnp.int32, va_t.shape, 0)
    vat_ref[...] = jnp.where(frow == HEAD_DIM, 1.0, va_t).astype(vat_ref.dtype)
    for j in range(2 * NSA_KV_HEADS):
        kcv_ref[j] = tok((_T_KCV[0] + j * LANES, _T_KCV[0] + (j + 1) * LANES))[:, :HEAD_DIM]
    for kh in range(NSA_KV_HEADS):
        sl = slice(kh * LANES, (kh + 1) * LANES)
        ks = key_slab(tok((_T_KS[0] + kh * LANES, _T_KS[0] + (kh + 1) * LANES)), gks_ref[...])
        ks_ref[:, sl] = (ks + alibi_cols + block_onehot).astype(ks_ref.dtype)
        kw = key_slab(tok((_T_KW[0] + kh * LANES, _T_KW[0] + (kh + 1) * LANES)), gkw_ref[...])
        kw_ref[:, sl] = (kw + alibi_cols).astype(kw_ref.dtype)

    qit_ref[...] = feat(_F_QI).astype(qit_ref.dtype)
    gw = feat(_F_GW)
    grow = lax.broadcasted_iota(jnp.int32, gw.shape, 0)
    gwt_ref[...] = jnp.where(grow < GATE_ROWS, jax.nn.sigmoid(gw), gw * w_idx_scale)
    erow = lax.broadcasted_iota(jnp.int32, (HEAD_DIM, tm), 0)
    q_scale = HEAD_DIM ** -0.5 * LOG2E
    for rows, g_ref, out_ref, slopes in ((_F_QA, gqa_ref, qat_ref, slopes_a), (_F_QB, gqb_ref, qbt_ref, slopes_b)):
        q_all = feat(rows)
        for hd in range(len(slopes)):
            q = q_all[hd * HEAD_DIM:(hd + 1) * HEAD_DIM]
            ms = jnp.mean(q * q, axis=0, keepdims=True)
            q = q * lax.rsqrt(ms + RMS_EPS) * g_ref[...] * q_scale
            out_ref[hd * LANES:hd * LANES + HEAD_DIM, :] = q.astype(out_ref.dtype)
            extra = jnp.where(erow < 3, slopes[hd], 0.0)
            out_ref[hd * LANES + HEAD_DIM:(hd + 1) * LANES, :] = extra.astype(out_ref.dtype)
    ones_row = jnp.where(erow == 0, 1.0, 0.0)
    for rows, out_ref in ((_F_VS, vst_ref), (_F_VW, vwt_ref)):
        v_t = feat(rows)
        for kh in range(NSA_KV_HEADS):
            out_ref[kh * LANES:kh * LANES + HEAD_DIM, :] = v_t[kh * HEAD_DIM:(kh + 1) * HEAD_DIM].astype(out_ref.dtype)
            out_ref[kh * LANES + HEAD_DIM:(kh + 1) * LANES, :] = ones_row.astype(out_ref.dtype)


def _proj(x2, seq_len, g_mix, w_tok, w_feat, g_lat, w_uk, w_uv_t, gk_a, gks, gkw, gq_a, gq_b, tm):
    T, D = x2.shape
    tokm = lambda w: pl.BlockSpec((tm, w), lambda i: (i, 0))
    featm = lambda r: pl.BlockSpec((r, tm), lambda i: (0, i))
    full = lambda a: pl.BlockSpec(a.shape, lambda i: (0,) * a.ndim)
    f32 = jnp.float32
    mx = MXU_DTYPE
    out_shape = (
        jax.ShapeDtypeStruct((T, LANES), mx),
        jax.ShapeDtypeStruct((T, LANES), mx),
        jax.ShapeDtypeStruct((LANES, T), mx),
        jax.ShapeDtypeStruct((2 * NSA_KV_HEADS, T, HEAD_DIM), f32),
        jax.ShapeDtypeStruct((T, NSA_KV_HEADS * LANES), mx),
        jax.ShapeDtypeStruct((T, NSA_KV_HEADS * LANES), mx),
        jax.ShapeDtypeStruct((NSA_KV_HEADS * LANES, T), mx),
        jax.ShapeDtypeStruct((NSA_KV_HEADS * LANES, T), mx),
        jax.ShapeDtypeStruct((IDX_HEADS * IDX_DIM, T), mx),
        jax.ShapeDtypeStruct((LANES, T), f32),
        jax.ShapeDtypeStruct((DSA_HEADS * LANES, T), mx),
        jax.ShapeDtypeStruct((NSA_HEADS * LANES, T), mx),
    )
    out_specs = (
        tokm(LANES), tokm(LANES), featm(LANES),
        pl.BlockSpec((2 * NSA_KV_HEADS, tm, HEAD_DIM), lambda i: (0, i, 0)),
        tokm(NSA_KV_HEADS * LANES), tokm(NSA_KV_HEADS * LANES),
        featm(NSA_KV_HEADS * LANES), featm(NSA_KV_HEADS * LANES),
        featm(IDX_HEADS * IDX_DIM), featm(LANES), featm(DSA_HEADS * LANES), featm(NSA_HEADS * LANES),
    )
    kern = functools.partial(
        _proj_kernel, seq_len=seq_len, w_idx_scale=IDX_HEADS ** -0.5 * IDX_DIM ** -0.5,
        slopes_a=_alibi_slopes(DSA_HEADS), slopes_b=_alibi_slopes(NSA_HEADS))
    ins = (x2, g_mix, w_tok, w_feat, g_lat, w_uk, w_uv_t, gk_a, gks, gkw, gq_a, gq_b)
    return pl.pallas_call(
        kern,
        grid=(T // tm,),
        in_specs=[tokm(D)] + [full(a) for a in ins[1:]],
        out_specs=out_specs,
        out_shape=out_shape,
        compiler_params=_params(("parallel",)),
        name="proj",
    )(*ins)


def _compress_kernel(xk_ref, xv_ref, pe1_ref, pe2_ref, w1_ref, w2k_ref, w2vt_ref, gkc_ref, kc_ref, vct_ref):
    def hidden(x, w1):
        n, half = x.shape
        a = _dot((x + pe1_ref[...]).astype(MXU_DTYPE), w1[:half, :])
        b = _dot((x + pe2_ref[...]).astype(MXU_DTYPE), w1[half:, :])
        return jax.nn.gelu(a + pltpu.roll(b, n - 1, 0)).astype(MXU_DTYPE)

    yk = _dot(hidden(xk_ref[...], w1_ref[0]), w2k_ref[...])
    ms = jnp.sum(yk * yk, axis=-1, keepdims=True) * (1.0 / HEAD_DIM)
    kc_ref[...] = (yk * lax.rsqrt(ms + RMS_EPS) * gkc_ref[...]).astype(kc_ref.dtype)
    vct_ref[...] = _dot_nt(w2vt_ref[...], hidden(xv_ref[...], w1_ref[1])).astype(vct_ref.dtype)


def _compress(xc, pe1, pe2, phi1, w2k, w2v_t, g_kc):
    _, B, n, w = xc.shape
    full = lambda a: pl.BlockSpec(a.shape, lambda b, kh: (0,) * a.ndim)
    return pl.pallas_call(
        _compress_kernel,
        grid=(B, NSA_KV_HEADS),
        in_specs=[
            pl.BlockSpec((None, None, n, w), lambda b, kh: (kh, b, 0, 0)),
            pl.BlockSpec((None, None, n, w), lambda b, kh: (NSA_KV_HEADS + kh, b, 0, 0)),
            full(pe1), full(pe2), full(phi1), full(w2k), full(w2v_t), full(g_kc),
        ],
        out_specs=(pl.BlockSpec((None, None, n, LANES), lambda b, kh: (b, kh, 0, 0)),
                   pl.BlockSpec((None, None, LANES, n), lambda b, kh: (b, kh, 0, 0))),
        out_shape=(jax.ShapeDtypeStruct((B, NSA_KV_HEADS, n, LANES), MXU_DTYPE),
                   jax.ShapeDtypeStruct((B, NSA_KV_HEADS, LANES, n), MXU_DTYPE)),
        compiler_params=_params(("parallel", "parallel")),
        name="compress",
    )(xc, xc, pe1, pe2, phi1, w2k, w2v_t, g_kc)


_INT_MIN = -2 ** 31


def _key_to_float(u):
    key = u ^ jnp.int32(_INT_MIN)
    bits = jnp.where(key >= 0, key, key ^ jnp.int32(0x7FFFFFFF))
    return lax.bitcast_convert_type(bits, jnp.float32)


def _col_reduce(x, op, final):
    rows = REDUCE_ROWS
    acc = x
    if x.shape[0] > rows and x.shape[0] % rows == 0:
        acc = x[:rows]
        for j in range(1, x.shape[0] // rows):
            acc = op(acc, x[j * rows:(j + 1) * rows])
    return final(acc.astype(jnp.float32), axis=0, keepdims=True)


def _col_sum(x):
    return _col_reduce(x, jnp.add, jnp.sum)


def _col_max(x):
    return _col_reduce(x, jnp.maximum, jnp.max)


def _kth_largest(score, k):
    def body(it, u):
        u_try = u | lax.shift_left(jnp.int32(1), 31 - it)
        cnt = _col_sum(jnp.where(score >= _key_to_float(u_try), 1.0, 0.0))
        return jnp.where(cnt >= k, u_try, u)

    u = lax.fori_loop(0, 32, body, jnp.zeros((1, score.shape[1]), jnp.int32))
    thr = _key_to_float(u)
    return jnp.where(thr != thr, -jnp.inf, thr)


def _lane_tile(x, n):
    return jnp.concatenate([x] * n, axis=1)


def _softmax_pv(parts):
    m = None
    for s, _ in parts:
        m_p = _col_max(s)
        m = m_p if m is None else jnp.maximum(m, m_p)
    out = None
    for s, v_t in parts:
        o_p = _dot(v_t, jnp.exp2(s - m).astype(MXU_DTYPE))
        out = o_p if out is None else out + o_p
    return out


def _width_classes(n_qblk, seq_len):
    n_cls = N_WIDTH_CLASSES if n_qblk % N_WIDTH_CLASSES == 0 else 1
    per = n_qblk // n_cls
    return [(c * per, (c + 1) * per, (c + 1) * per * Q_BLOCK) for c in range(n_cls)]


def _dsa_body(W, qit_ref, gwt_ref, qat_ref, ki_ref, ka_ref, vat_ref, o_ref, sel_ref, *, topk):
    f32 = jnp.float32
    QB = Q_BLOCK
    t = pl.program_id(1) * QB + lax.broadcasted_iota(jnp.int32, (1, QB), 1)
    spos = lax.broadcasted_iota(jnp.int32, (W, QB), 0)
    causal = spos <= t

    ki = ki_ref[:W, :]
    G = 4
    kpad = jnp.zeros((LANES - IDX_DIM, G * QB), MXU_DTYPE)
    score = jnp.zeros((W, QB), f32)
    for hg in range(IDX_HEADS // G):
        q_idx = jnp.concatenate(
            [qit_ref[(hg * G + g) * IDX_DIM:(hg * G + g + 1) * IDX_DIM, :] for g in range(G)], axis=1)
        logits = _dot(ki, jnp.concatenate([q_idx, kpad], axis=0))
        for g in range(G):
            w_h = gwt_ref[GATE_ROWS + hg * G + g:GATE_ROWS + hg * G + g + 1, :]
            score = score + jnp.maximum(logits[:, g * QB:(g + 1) * QB], 0.0) * w_h
    score = jnp.where(causal, score, -jnp.inf)

    thr = _kth_largest(score, float(topk))
    gt = score > thr
    eq = (score == thr) & causal
    need = float(topk) - _col_sum(jnp.where(gt, 1.0, 0.0))
    n_eq = _col_sum(jnp.where(eq, 1.0, 0.0))
    sel_ref[:W, :] = jnp.where(gt | eq, 0.0, NEG_INF)

    @pl.when(jnp.max(n_eq - need) > 0.0)
    def _():
        cw = TIE_CHUNK
        r = lax.broadcasted_iota(jnp.int32, (cw, cw), 0)
        c = lax.broadcasted_iota(jnp.int32, (cw, cw), 1)
        lower = jnp.where(c <= r, 1.0, 0.0).astype(MXU_DTYPE)
        carry = jnp.zeros((1, QB), f32)
        for j in range(W // cw):
            sl = slice(j * cw, (j + 1) * cw)
            eq_j = jnp.where(eq[sl], 1.0, 0.0)
            prefix = _dot(lower, eq_j.astype(MXU_DTYPE)) + carry
            keep = gt[sl] | (eq[sl] & (prefix <= need))
            sel_ref[sl, :] = jnp.where(keep, 0.0, NEG_INF)
            carry = carry + jnp.sum(eq_j, axis=0, keepdims=True)

    G = 4
    ka = ka_ref[:W, :]
    va_t = vat_ref[:, :W]
    for hg in range(DSA_HEADS // G):
        q = jnp.concatenate([qat_ref[(hg * G + g) * LANES:(hg * G + g + 1) * LANES, :] for g in range(G)], axis=1)
        s = _dot(ka, q) + _lane_tile(sel_ref[:W, :], G)
        o_t = _softmax_pv([(s, va_t)])
        o_t = o_t[:HEAD_DIM] * (1.0 / o_t[HEAD_DIM:HEAD_DIM + 1])
        for g2 in range(G // 2):
            pair = jnp.concatenate([o_t[:, (2 * g2) * QB:(2 * g2 + 1) * QB],
                                    o_t[:, (2 * g2 + 1) * QB:(2 * g2 + 2) * QB]], axis=0)
            col = (hg * G + 2 * g2) * HEAD_DIM
            o_ref[:, col:col + 2 * HEAD_DIM] = pair.T.astype(o_ref.dtype)


def _dsa_kernel(qit_ref, gwt_ref, qat_ref, ki_ref, ka_ref, vat_ref, o_ref, sel_ref, *, topk, classes):
    qblk = pl.program_id(1)
    for lo, hi, W in classes:
        @pl.when((qblk >= lo) & (qblk < hi))
        def _(W=W):
            _dsa_body(W, qit_ref, gwt_ref, qat_ref, ki_ref, ka_ref, vat_ref, o_ref, sel_ref, topk=topk)


def _dsa(B, S, qi_t, gw_t, qa_t, ki, ka, va_t):
    nq = S // Q_BLOCK
    topk = min(DSA_TOPK_MAX, S // 4)
    fblk = lambda r: pl.BlockSpec((r, Q_BLOCK), lambda b, i: (0, b * nq + i))
    tseq = lambda w: pl.BlockSpec((S, w), lambda b, i: (b, 0))
    fseq = lambda r: pl.BlockSpec((r, S), lambda b, i: (0, b))
    return pl.pallas_call(
        functools.partial(_dsa_kernel, topk=topk, classes=_width_classes(nq, S)),
        grid=(B, nq),
        in_specs=[fblk(IDX_HEADS * IDX_DIM), fblk(LANES), fblk(DSA_HEADS * LANES),
                  tseq(LANES), tseq(LANES), fseq(LANES)],
        out_specs=pl.BlockSpec((Q_BLOCK, WIDTH_A), lambda b, i: (b * nq + i, 0)),
        out_shape=jax.ShapeDtypeStruct((B * S, WIDTH_A), MXU_DTYPE),
        scratch_shapes=[pltpu.VMEM((S, Q_BLOCK), jnp.float32)],
        compiler_params=_params(("parallel", "arbitrary")),
        name="dsa",
    )(qi_t, gw_t, qa_t, ki, ka, va_t)


def _nsa_body(W, first_blk, gwt_ref, qbt_ref, ks_ref, kw_ref, vst_ref, vwt_ref, kc_ref, vct_ref, ovl_ref, o_ref,
              *, seq_len):
    f32 = jnp.float32
    QB, G = Q_BLOCK, NSA_GROUP
    n_slc = seq_len // SLC_LEN
    n_top = min(SLC_TOPN, n_slc)
    win_len = WINDOW + QB
    nc = kc_ref.shape[1]

    t0 = pl.program_id(1) * QB
    t = t0 + lax.broadcasted_iota(jnp.int32, (1, QB), 1)
    t4 = _lane_tile(t, G)
    valid_c = lax.broadcasted_iota(jnp.int32, (nc, G * QB), 0) * CMP_STRIDE + (CMP_LEN - 1) <= t4
    past = first_blk * QB
    causal_tail = past + lax.broadcasted_iota(jnp.int32, (W - past, G * QB), 0) <= t4
    w0 = pl.multiple_of(jnp.maximum(t0 - WINDOW, 0), QB)
    full_window = past >= WINDOW
    if full_window:
        r_minus_lane = lax.broadcasted_iota(jnp.int32, (QB, G * QB), 0) - (t4 - t0)
        w_head_ok = r_minus_lane > 0
        w_tail_ok = r_minus_lane <= 0
    else:
        wd = t4 - (w0 + lax.broadcasted_iota(jnp.int32, (win_len, G * QB), 0))
        wvalid = (wd >= 0) & (wd < WINDOW)
    jblk = lax.broadcasted_iota(jnp.int32, (MAX_SLC_BLOCKS, QB), 0)
    cur = lax.shift_right_logical(t, 6)
    future = jblk > cur
    forced = (jblk == 0) | (jblk == cur) | (jblk == cur - 1)
    ovl = ovl_ref[...]

    for kh in range(NSA_KV_HEADS):
        hs = [kh * G + g for g in range(G)]
        q = jnp.concatenate([qbt_ref[h * LANES:(h + 1) * LANES, :] for h in hs], axis=1)

        s_c = jnp.where(valid_c, _dot(kc_ref[kh], q), NEG_INF)
        m_c = jnp.max(s_c, axis=0, keepdims=True)
        e_c = jnp.where(valid_c, jnp.exp2(s_c - m_c), 0.0)
        l_c = jnp.sum(e_c, axis=0, keepdims=True)
        p_c = e_c * (1.0 / jnp.where(l_c > 0.0, l_c, 1.0))
        o_cmp = _dot(vct_ref[kh], p_c.astype(MXU_DTYPE))

        p_sum = p_c[:, 0:QB] + p_c[:, QB:2 * QB] + p_c[:, 2 * QB:3 * QB] + p_c[:, 3 * QB:4 * QB]
        p_hi = p_sum.astype(MXU_DTYPE)
        p_lo = (p_sum - p_hi.astype(f32)).astype(MXU_DTYPE)
        imp = _dot(ovl, p_hi) + _dot(ovl, p_lo)
        val = jnp.where(future, -jnp.inf, imp + jnp.where(forced, FORCE_BONUS, 0.0))
        rank = jnp.zeros(val.shape, f32)
        for i in range(n_slc):
            vi = val[i:i + 1, :]
            rank = rank + jnp.where((vi > val) | ((vi == val) & (jblk > i)), 1.0, 0.0)
        bias = jnp.where((rank < n_top) & (jblk < n_slc), 0.0, -BLOCK_BIAS).astype(MXU_DTYPE)
        q_sel = jnp.concatenate([q[:BIAS_COL], _lane_tile(bias, G)], axis=0)

        ksl = slice(kh * LANES, (kh + 1) * LANES)
        s_s = _dot(ks_ref[:W, ksl], q_sel)
        parts = [(jnp.where(causal_tail, s_s[past:], NEG_INF), vst_ref[ksl, past:W])]
        if past:
            parts.append((s_s[:past], vst_ref[ksl, :past]))
        o_slc = _softmax_pv(parts)

        s_w = _dot(kw_ref[pl.ds(w0, win_len), ksl], q)
        v_w = vwt_ref[ksl, pl.ds(w0, win_len)]
        if full_window:
            parts = [(jnp.where(w_head_ok, s_w[:QB], NEG_INF), v_w[:, :QB]),
                     (s_w[QB:WINDOW], v_w[:, QB:WINDOW]),
                     (jnp.where(w_tail_ok, s_w[WINDOW:], NEG_INF), v_w[:, WINDOW:])]
        else:
            parts = [(jnp.where(wvalid, s_w, NEG_INF), v_w)]
        o_win = _softmax_pv(parts)

        inv_s = 1.0 / o_slc[HEAD_DIM:HEAD_DIM + 1]
        inv_w = 1.0 / o_win[HEAD_DIM:HEAD_DIM + 1]
        heads = []
        for g in range(G):
            h = hs[g]
            ls = slice(g * QB, (g + 1) * QB)
            g0 = gwt_ref[3 * h + 0:3 * h + 1, :]
            g1 = gwt_ref[3 * h + 1:3 * h + 2, :] * inv_s[:, ls]
            g2 = gwt_ref[3 * h + 2:3 * h + 3, :] * inv_w[:, ls]
            heads.append(g0 * o_cmp[:HEAD_DIM, ls] + g1 * o_slc[:HEAD_DIM, ls] + g2 * o_win[:HEAD_DIM, ls])
        for g2_ in range(G // 2):
            pair = jnp.concatenate([heads[2 * g2_], heads[2 * g2_ + 1]], axis=0)
            col = (kh * G + 2 * g2_) * HEAD_DIM
            o_ref[:, col:col + 2 * HEAD_DIM] = pair.T.astype(o_ref.dtype)


def _nsa_kernel(gwt_ref, qbt_ref, ks_ref, kw_ref, vst_ref, vwt_ref, kc_ref, vct_ref, ovl_ref, o_ref,
                *, seq_len, classes):
    qblk = pl.program_id(1)
    for lo, hi, W in classes:
        @pl.when((qblk >= lo) & (qblk < hi))
        def _(W=W, lo=lo):
            _nsa_body(W, lo, gwt_ref, qbt_ref, ks_ref, kw_ref, vst_ref, vwt_ref, kc_ref, vct_ref, ovl_ref, o_ref,
                      seq_len=seq_len)


def _nsa(B, S, gw_t, qb_t, ks, kw, vs_t, vw_t, kc, vc_t, ovl):
    nq = S // Q_BLOCK
    nc = kc.shape[2]
    kvw = NSA_KV_HEADS * LANES
    fblk = lambda r: pl.BlockSpec((r, Q_BLOCK), lambda b, i: (0, b * nq + i))
    tseq = lambda w: pl.BlockSpec((S, w), lambda b, i: (b, 0))
    fseq = lambda r: pl.BlockSpec((r, S), lambda b, i: (0, b))
    return pl.pallas_call(
        functools.partial(_nsa_kernel, seq_len=S, classes=_width_classes(nq, S)),
        grid=(B, nq),
        in_specs=[fblk(LANES), fblk(NSA_HEADS * LANES), tseq(kvw), tseq(kvw), fseq(kvw), fseq(kvw),
                  pl.BlockSpec((None, NSA_KV_HEADS, nc, LANES), lambda b, i: (b, 0, 0, 0)),
                  pl.BlockSpec((None, NSA_KV_HEADS, LANES, nc), lambda b, i: (b, 0, 0, 0)),
                  pl.BlockSpec(ovl.shape, lambda b, i: (0, 0))],
        out_specs=pl.BlockSpec((Q_BLOCK, WIDTH_B), lambda b, i: (b * nq + i, 0)),
        out_shape=jax.ShapeDtypeStruct((B * S, WIDTH_B), MXU_DTYPE),
        compiler_params=_params(("parallel", "arbitrary")),
        name="nsa",
    )(gw_t, qb_t, ks, kw, vs_t, vw_t, kc, vc_t, ovl)


def _merge_ffn_kernel(x_ref, oa_ref, ob_ref, gmix_ref, wg_ref, wua_ref, wub_ref, wo_ref,
                      gmlp_ref, w1_ref, w2_ref, o_ref, *, chunk):
    x = x_ref[...]
    D = x.shape[1]
    h = _rms(x, gmix_ref[...]).astype(MXU_DTYPE)
    ga = jax.nn.sigmoid(_dot(h, wg_ref[:, :D]))
    gb = jax.nn.sigmoid(_dot(h, wg_ref[:, D:]))
    merged = ga * _dot(oa_ref[...], wua_ref[...]) + gb * _dot(ob_ref[...], wub_ref[...])
    x1 = x + _dot(merged.astype(MXU_DTYPE), wo_ref[...])
    h2 = _rms(x1, gmlp_ref[...]).astype(MXU_DTYPE)
    acc = x1
    for c in range(w1_ref.shape[1] // chunk):
        u = jnp.maximum(_dot(h2, w1_ref[:, c * chunk:(c + 1) * chunk]), 0.0)
        acc = acc + _dot((u * u).astype(MXU_DTYPE), w2_ref[c * chunk:(c + 1) * chunk, :])
    o_ref[...] = acc


def _merge_ffn(x2, oa, ob, g_mix, w_gate, w_up_a, w_up_b, w_out, g_mlp, w1, w2, tm):
    T, D = x2.shape
    row = lambda w: pl.BlockSpec((tm, w), lambda i: (i, 0))
    full = lambda a: pl.BlockSpec(a.shape, lambda i: (0,) * a.ndim, pipeline_mode=pl.Buffered(1))
    ins = (x2, oa, ob, g_mix, w_gate, w_up_a, w_up_b, w_out, g_mlp, w1, w2)
    return pl.pallas_call(
        functools.partial(_merge_ffn_kernel, chunk=1024),
        grid=(T // tm,),
        in_specs=[row(D), row(WIDTH_A), row(WIDTH_B)] + [full(a) for a in ins[3:]],
        out_specs=row(D),
        out_shape=jax.ShapeDtypeStruct((T, D), jnp.float32),
        compiler_params=_params(("parallel",)),
        name="merge_ffn",
    )(*ins)


def _pad_heads(w, n_heads, axis):
    shape = list(w.shape)
    shape[axis:axis + 1] = [n_heads, HEAD_DIM]
    w = w.reshape(shape)
    pad = [(0, 0)] * w.ndim
    pad[axis + 1] = (0, LANES - HEAD_DIM)
    w = jnp.pad(w, pad)
    shape[axis:axis + 2] = [n_heads * LANES]
    return w.reshape(shape)


def _layer(x, g_mix, w_in, g_q_a, g_k_a, g_lat_a, w_uk_a, w_uv_a, g_q_b, g_kc_b, g_ks_b, g_kw_b,
           pe_cmp_b, phi_k1_b, phi_k2_b, phi_v1_b, phi_v2_b, w_up_a, w_up_b, w_out, g_mlp, w_ff1, w_ff2):
    B, S, D = x.shape
    T = B * S
    f32 = jnp.float32
    mx = MXU_DTYPE
    n_slc = S // SLC_LEN
    assert n_slc <= MAX_SLC_BLOCKS and S % (N_WIDTH_CLASSES * TIE_CHUNK) == 0 and S >= WINDOW + Q_BLOCK
    col_sizes = (WIDTH_A, DSA_LATENT, IDX_HEADS * IDX_DIM, IDX_DIM, IDX_HEADS,
                 WIDTH_B, 6 * NSA_KV_W, 3 * NSA_HEADS, 2 * D)
    offs = np.cumsum((0,) + col_sizes)
    w_qa, w_ca, w_qi, w_ki, w_wi, w_qb, w_kvb, w_gb, w_gm = [
        w_in[:, offs[i]:offs[i + 1]] for i in range(len(col_sizes))]
    w_kc, w_vc, w_ks, w_vs, w_kw, w_vw = [w_kvb[:, j * NSA_KV_W:(j + 1) * NSA_KV_W] for j in range(6)]
    padh = lambda w: _pad_heads(w, NSA_KV_HEADS, 1)
    w_tok = jnp.concatenate(
        [w_ki, jnp.zeros((D, LANES - IDX_DIM), f32), w_ca, padh(w_kc), padh(w_vc), padh(w_ks), padh(w_kw)], axis=1).astype(mx)
    assert w_tok.shape[1] == _T_COLS
    gw_pad = jnp.zeros((D, LANES - GATE_ROWS - IDX_HEADS), f32)
    w_feat = jnp.concatenate(
        [w_qi, w_gb, w_wi, gw_pad, w_qa, w_qb, w_vs, w_vw], axis=1).T.astype(mx)
    assert w_feat.shape[0] == _F_ROWS
    w_gate = w_gm.astype(mx)
    zpad = jnp.zeros((DSA_LATENT, LANES - HEAD_DIM), f32)
    w_uk = jnp.concatenate([w_uk_a, zpad], axis=1).astype(mx)
    w_uv_t = jnp.concatenate([w_uv_a, zpad], axis=1).T.astype(mx)
    row = lambda g: g.reshape(1, -1).astype(f32)
    rpad = lambda g: jnp.concatenate([row(g), jnp.zeros((1, LANES - HEAD_DIM), f32)], axis=1)
    colv = lambda g: g.reshape(-1, 1).astype(f32)

    x2 = x.reshape(T, D)
    (ki, ka, va_t, kcv, ks, kw, vs_t, vw_t, qi_t, gw_t, qa_t, qb_t) = _proj(
        x2, S, row(g_mix), w_tok, w_feat, row(g_lat_a), w_uk, w_uv_t, rpad(g_k_a), rpad(g_ks_b), rpad(g_kw_b),
        colv(g_q_a), colv(g_q_b), tm=256)

    n_chunk = S // CMP_STRIDE
    n_cmp = (S - CMP_LEN) // CMP_STRIDE + 1
    xc = kcv.reshape(2 * NSA_KV_HEADS, B, n_chunk, CMP_STRIDE * HEAD_DIM)
    pe1 = pe_cmp_b[:CMP_STRIDE].reshape(1, -1)
    pe2 = pe_cmp_b[CMP_STRIDE:].reshape(1, -1)
    phi1 = jnp.stack([phi_k1_b, phi_v1_b]).astype(mx)
    hpad = jnp.zeros((CMP_HIDDEN, LANES - HEAD_DIM), f32)
    w2k = jnp.concatenate([phi_k2_b, hpad], axis=1).astype(mx)
    w2v_t = jnp.concatenate([phi_v2_b, hpad], axis=1).T.astype(mx)
    kc, vc_t = _compress(xc, pe1, pe2, phi1, w2k, w2v_t, rpad(g_kc_b))

    c_i = np.arange(n_chunk)[None, :] * CMP_STRIDE
    j_i = np.arange(MAX_SLC_BLOCKS)[:, None] * SLC_LEN
    ovl = (c_i < j_i + SLC_LEN) & (c_i + CMP_LEN > j_i) & (c_i < n_cmp * CMP_STRIDE) & (j_i < S)
    ovl = jnp.asarray(ovl, f32).astype(mx)

    o_a = _dsa(B, S, qi_t, gw_t, qa_t, ki, ka, va_t)
    o_b = _nsa(B, S, gw_t, qb_t, ks, kw, vs_t, vw_t, kc, vc_t, ovl)

    out = _merge_ffn(x2, o_a, o_b, row(g_mix), w_gate, w_up_a.astype(mx), w_up_b.astype(mx), w_out.astype(mx),
                     row(g_mlp), w_ff1.astype(mx), w_ff2.astype(mx), tm=256)
    return out.reshape(B, S, D)


def kernel(x, g_mix, w_in, g_q_a, g_k_a, g_lat_a, w_uk_a, w_uv_a, g_q_b, g_kc_b, g_ks_b, g_kw_b, pe_cmp_b,
           phi_k1_b, phi_k2_b, phi_v1_b, phi_v2_b, w_up_a, w_up_b, w_out, g_mlp, w_ff1, w_ff2):
    params = (g_mix, w_in, g_q_a, g_k_a, g_lat_a, w_uk_a, w_uv_a, g_q_b, g_kc_b, g_ks_b, g_kw_b, pe_cmp_b,
              phi_k1_b, phi_k2_b, phi_v1_b, phi_v2_b, w_up_a, w_up_b, w_out, g_mlp, w_ff1, w_ff2)
    for l in range(g_mix.shape[0]):
        x = _layer(x, *[p[l] for p in params])
    return x
```

```python
import functools
import math

import numpy as np
import jax
import jax.numpy as jnp
from jax import lax
from jax.experimental import pallas as pl
from jax.experimental.pallas import tpu as pltpu

HEAD_DIM = 64
DSA_HEADS = 8
DSA_LATENT = 128
IDX_HEADS = 8
IDX_DIM = 32
DSA_TOPK_MAX = 256
NSA_HEADS = 8
NSA_KV_HEADS = 2
NSA_GROUP = NSA_HEADS // NSA_KV_HEADS
CMP_LEN = 32
CMP_STRIDE = 16
CMP_HIDDEN = 128
SLC_LEN = 64
SLC_TOPN = 16
WINDOW = 512
FORCE_BONUS = 1e6
Q_BLOCK = 128
RMS_EPS = 1e-6
NEG_INF = -1e30

WIDTH_A = DSA_HEADS * HEAD_DIM
WIDTH_B = NSA_HEADS * HEAD_DIM
NSA_KV_W = NSA_KV_HEADS * HEAD_DIM

LANES = 128
GATE_ROWS = 3 * NSA_HEADS
ALIBI_COL = HEAD_DIM
BIAS_COL = 96
MAX_SLC_BLOCKS = LANES - BIAS_COL
BLOCK_BIAS = 2.0 ** 100
LOG2E = math.log2(math.e)
MXU_DTYPE = jnp.bfloat16
VMEM_LIMIT = 52 * 1024 * 1024
N_WIDTH_CLASSES = 8
TIE_CHUNK = 256
REDUCE_ROWS = 64
QK_SIDE_STEPS = 8

_NT = (((1,), (1,)), ((), ()))


def _alibi_slopes(n_heads):
    return [float(v) for v in np.asarray(
        2.0 ** (-8.0 * np.arange(1, n_heads + 1) / n_heads), dtype=np.float32)]


def _dot(a, b):
    return jnp.dot(a, b, preferred_element_type=jnp.float32)


def _dot_nt(a, b):
    return lax.dot_general(a, b, _NT, preferred_element_type=jnp.float32)


def _rms(x, g):
    ms = jnp.mean(x * x, axis=-1, keepdims=True)
    return x * lax.rsqrt(ms + RMS_EPS) * g


def _params(sem):
    return pltpu.CompilerParams(dimension_semantics=sem, vmem_limit_bytes=VMEM_LIMIT)


def _split3(c):
    c1 = c.astype(MXU_DTYPE).astype(jnp.float32)
    c2 = (c - c1).astype(MXU_DTYPE).astype(jnp.float32)
    c3 = (c - c1 - c2).astype(MXU_DTYPE).astype(jnp.float32)
    return c1, c2, c3


_T_KI = (0, 128)
_T_CA = (128, 256)
_T_KCV = (256, 768)
_T_KS = (768, 1024)
_T_KW = (1024, 1280)
_T_COLS = 1280
_F_QI = (0, 256)
_F_GW = (256, 384)
_F_QA = (384, 896)
_F_QB = (896, 1408)
_F_VS = (1408, 1536)
_F_VW = (1536, 1664)
_F_ROWS = 1664


def _proj_kernel(x_ref, gmix_ref, wtok_ref, wfeat_ref, glat_ref, wuk_ref, wuvt_ref, gka_ref, gks_ref, gkw_ref,
                 gqa_ref, gqb_ref,
                 ki_ref, ka_ref, vat_ref, kcv_ref, ks_ref, kw_ref, vst_ref, vwt_ref, qit_ref, gwt_ref,
                 qat_ref, qbt_ref, *, seq_len, w_idx_scale, slopes_a, slopes_b):
    f32 = jnp.float32
    x = x_ref[...]
    tm = x.shape[0]
    h = _rms(x, gmix_ref[...]).astype(MXU_DTYPE)

    def tok(c):
        return _dot(h, wtok_ref[:, c[0]:c[1]])

    def feat(r):
        return _dot_nt(wfeat_ref[r[0]:r[1], :], h)

    s0 = (pl.program_id(0) % (seq_len // tm)) * tm
    spos = s0 + lax.broadcasted_iota(jnp.int32, (tm, LANES), 0)
    lane = lax.broadcasted_iota(jnp.int32, (tm, LANES), 1)
    c1, c2, c3 = _split3(spos.astype(f32) * LOG2E)
    alibi_cols = jnp.where(lane == ALIBI_COL, c1,
                           jnp.where(lane == ALIBI_COL + 1, c2, jnp.where(lane == ALIBI_COL + 2, c3, 0.0)))
    block_onehot = jnp.where(lane - BIAS_COL == lax.shift_right_logical(spos, 6), 1.0, 0.0)

    def key_slab(raw, g):
        ms = jnp.sum(raw * raw, axis=-1, keepdims=True) * (1.0 / HEAD_DIM)
        return raw * lax.rsqrt(ms + RMS_EPS) * g

    ki_ref[...] = tok(_T_KI).astype(ki_ref.dtype)
    c = _rms(tok(_T_CA), glat_ref[...]).astype(MXU_DTYPE)
    ka_ref[...] = (key_slab(_dot(c, wuk_ref[...]), gka_ref[...]) + alibi_cols).astype(ka_ref.dtype)
    va_t = _dot_nt(wuvt_ref[...], c)
    frow = lax.broadcasted_iota(jnp.int32, va_t.shape, 0)
    vat_ref[...] = jnp.where(frow == HEAD_DIM, 1.0, va_t).astype(vat_ref.dtype)
    for j in range(2 * NSA_KV_HEADS):
        kcv_ref[j] = tok((_T_KCV[0] + j * LANES, _T_KCV[0] + (j + 1) * LANES))[:, :HEAD_DIM]
    for kh in range(NSA_KV_HEADS):
        sl = slice(kh * LANES, (kh + 1) * LANES)
        ks = key_slab(tok((_T_KS[0] + kh * LANES, _T_KS[0] + (kh + 1) * LANES)), gks_ref[...])
        ks_ref[:, sl] = (ks + alibi_cols + block_onehot).astype(ks_ref.dtype)
        kw = key_slab(tok((_T_KW[0] + kh * LANES, _T_KW[0] + (kh + 1) * LANES)), gkw_ref[...])
        kw_ref[:, sl] = (kw + alibi_cols).astype(kw_ref.dtype)

    qit_ref[...] = feat(_F_QI).astype(qit_ref.dtype)
    gw = feat(_F_GW)
    grow = lax.broadcasted_iota(jnp.int32, gw.shape, 0)
    gwt_ref[...] = jnp.where(grow < GATE_ROWS, jax.nn.sigmoid(gw), gw * w_idx_scale)
    erow = lax.broadcasted_iota(jnp.int32, (HEAD_DIM, tm), 0)
    q_scale = HEAD_DIM ** -0.5 * LOG2E
    for rows, g_ref, out_ref, slopes in ((_F_QA, gqa_ref, qat_ref, slopes_a), (_F_QB, gqb_ref, qbt_ref, slopes_b)):
        q_all = feat(rows)
        for hd in range(len(slopes)):
            q = q_all[hd * HEAD_DIM:(hd + 1) * HEAD_DIM]
            ms = jnp.mean(q * q, axis=0, keepdims=True)
            q = q * lax.rsqrt(ms + RMS_EPS) * g_ref[...] * q_scale
            out_ref[hd * LANES:hd * LANES + HEAD_DIM, :] = q.astype(out_ref.dtype)
            extra = jnp.where(erow < 3, slopes[hd], 0.0)
            out_ref[hd * LANES + HEAD_DIM:(hd + 1) * LANES, :] = extra.astype(out_ref.dtype)
    ones_row = jnp.where(erow == 0, 1.0, 0.0)
    for rows, out_ref in ((_F_VS, vst_ref), (_F_VW, vwt_ref)):
        v_t = feat(rows)
        for kh in range(NSA_KV_HEADS):
            out_ref[kh * LANES:kh * LANES + HEAD_DIM, :] = v_t[kh * HEAD_DIM:(kh + 1) * HEAD_DIM].astype(out_ref.dtype)
            out_ref[kh * LANES + HEAD_DIM:(kh + 1) * LANES, :] = ones_row.astype(out_ref.dtype)


def _proj(x2, seq_len, g_mix, w_tok, w_feat, g_lat, w_uk, w_uv_t, gk_a, gks, gkw, gq_a, gq_b, tm):
    T, D = x2.shape
    tokm = lambda w: pl.BlockSpec((tm, w), lambda i: (i, 0))
    featm = lambda r: pl.BlockSpec((r, tm), lambda i: (0, i))
    full = lambda a: pl.BlockSpec(a.shape, lambda i: (0,) * a.ndim)
    f32 = jnp.float32
    mx = MXU_DTYPE
    out_shape = (
        jax.ShapeDtypeStruct((T, LANES), mx),
        jax.ShapeDtypeStruct((T, LANES), mx),
        jax.ShapeDtypeStruct((LANES, T), mx),
        jax.ShapeDtypeStruct((2 * NSA_KV_HEADS, T, HEAD_DIM), f32),
        jax.ShapeDtypeStruct((T, NSA_KV_HEADS * LANES), mx),
        jax.ShapeDtypeStruct((T, NSA_KV_HEADS * LANES), mx),
        jax.ShapeDtypeStruct((NSA_KV_HEADS * LANES, T), mx),
        jax.ShapeDtypeStruct((NSA_KV_HEADS * LANES, T), mx),
        jax.ShapeDtypeStruct((IDX_HEADS * IDX_DIM, T), mx),
        jax.ShapeDtypeStruct((LANES, T), f32),
        jax.ShapeDtypeStruct((DSA_HEADS * LANES, T), mx),
        jax.ShapeDtypeStruct((NSA_HEADS * LANES, T), mx),
    )
    out_specs = (
        tokm(LANES), tokm(LANES), featm(LANES),
        pl.BlockSpec((2 * NSA_KV_HEADS, tm, HEAD_DIM), lambda i: (0, i, 0)),
        tokm(NSA_KV_HEADS * LANES), tokm(NSA_KV_HEADS * LANES),
        featm(NSA_KV_HEADS * LANES), featm(NSA_KV_HEADS * LANES),
        featm(IDX_HEADS * IDX_DIM), featm(LANES), featm(DSA_HEADS * LANES), featm(NSA_HEADS * LANES),
    )
    kern = functools.partial(
        _proj_kernel, seq_len=seq_len, w_idx_scale=IDX_HEADS ** -0.5 * IDX_DIM ** -0.5,
        slopes_a=_alibi_slopes(DSA_HEADS), slopes_b=_alibi_slopes(NSA_HEADS))
    ins = (x2, g_mix, w_tok, w_feat, g_lat, w_uk, w_uv_t, gk_a, gks, gkw, gq_a, gq_b)
    return pl.pallas_call(
        kern,
        grid=(T // tm,),
        in_specs=[tokm(D)] + [full(a) for a in ins[1:]],
        out_specs=out_specs,
        out_shape=out_shape,
        compiler_params=_params(("parallel",)),
        name="proj",
    )(*ins)


def _compress_kernel(xk_ref, xv_ref, pe1_ref, pe2_ref, w1_ref, w2k_ref, w2vt_ref, gkc_ref, kc_ref, vct_ref):
    def hidden(x, w1):
        n, half = x.shape
        a = _dot((x + pe1_ref[...]).astype(MXU_DTYPE), w1[:half, :])
        b = _dot((x + pe2_ref[...]).astype(MXU_DTYPE), w1[half:, :])
        return jax.nn.gelu(a + pltpu.roll(b, n - 1, 0)).astype(MXU_DTYPE)

    yk = _dot(hidden(xk_ref[...], w1_ref[0]), w2k_ref[...])
    ms = jnp.sum(yk * yk, axis=-1, keepdims=True) * (1.0 / HEAD_DIM)
    kc_ref[...] = (yk * lax.rsqrt(ms + RMS_EPS) * gkc_ref[...]).astype(kc_ref.dtype)
    vct_ref[...] = _dot_nt(w2vt_ref[...], hidden(xv_ref[...], w1_ref[1])).astype(vct_ref.dtype)


def _compress(xc, pe1, pe2, phi1, w2k, w2v_t, g_kc):
    _, B, n, w = xc.shape
    full = lambda a: pl.BlockSpec(a.shape, lambda b, kh: (0,) * a.ndim)
    return pl.pallas_call(
        _compress_kernel,
        grid=(B, NSA_KV_HEADS),
        in_specs=[
            pl.BlockSpec((None, None, n, w), lambda b, kh: (kh, b, 0, 0)),
            pl.BlockSpec((None, None, n, w), lambda b, kh: (NSA_KV_HEADS + kh, b, 0, 0)),
            full(pe1), full(pe2), full(phi1), full(w2k), full(w2v_t), full(g_kc),
        ],
        out_specs=(pl.BlockSpec((None, None, n, LANES), lambda b, kh: (b, kh, 0, 0)),
                   pl.BlockSpec((None, None, LANES, n), lambda b, kh: (b, kh, 0, 0))),
        out_shape=(jax.ShapeDtypeStruct((B, NSA_KV_HEADS, n, LANES), MXU_DTYPE),
                   jax.ShapeDtypeStruct((B, NSA_KV_HEADS, LANES, n), MXU_DTYPE)),
        compiler_params=_params(("parallel", "parallel")),
        name="compress",
    )(xc, xc, pe1, pe2, phi1, w2k, w2v_t, g_kc)


_INT_MIN = -2 ** 31


def _key_to_float(u):
    key = u ^ jnp.int32(_INT_MIN)
    bits = jnp.where(key >= 0, key, key ^ jnp.int32(0x7FFFFFFF))
    return lax.bitcast_convert_type(bits, jnp.float32)


def _col_reduce(x, op, final):
    rows = REDUCE_ROWS
    acc = x
    if x.shape[0] > rows and x.shape[0] % rows == 0:
        acc = x[:rows]
        for j in range(1, x.shape[0] // rows):
            acc = op(acc, x[j * rows:(j + 1) * rows])
    return final(acc.astype(jnp.float32), axis=0, keepdims=True)


def _col_sum(x):
    return _col_reduce(x, jnp.add, jnp.sum)


def _col_max(x):
    return _col_reduce(x, jnp.maximum, jnp.max)


def _kth_largest(score, k, side_steps=0, side_work=None):
    def body(it, u):
        u_try = u | lax.shift_left(jnp.int32(1), 31 - it)
        cnt = _col_sum(jnp.where(score >= _key_to_float(u_try), 1.0, 0.0))
        return jnp.where(cnt >= k, u_try, u)

    def body_with_side_work(j, u):
        side_work(j)
        for i in range(32 // side_steps):
            u = body(j * (32 // side_steps) + i, u)
        return u

    u = jnp.zeros((1, score.shape[1]), jnp.int32)
    if side_steps:
        u = lax.fori_loop(0, side_steps, body_with_side_work, u)
    else:
        u = lax.fori_loop(0, 32, body, u)
    thr = _key_to_float(u)
    return jnp.where(thr != thr, -jnp.inf, thr)


def _lane_tile(x, n):
    return jnp.concatenate([x] * n, axis=1)


def _softmax_pv(parts):
    m = None
    for s, _ in parts:
        m_p = _col_max(s)
        m = m_p if m is None else jnp.maximum(m, m_p)
    out = None
    for s, v_t in parts:
        o_p = _dot(v_t, jnp.exp2(s - m).astype(MXU_DTYPE))
        out = o_p if out is None else out + o_p
    return out


def _width_classes(n_qblk, seq_len):
    n_cls = N_WIDTH_CLASSES if n_qblk % N_WIDTH_CLASSES == 0 else 1
    per = n_qblk // n_cls
    return [(c * per, (c + 1) * per, (c + 1) * per * Q_BLOCK) for c in range(n_cls)]


def _dsa_body(W, qit_ref, gwt_ref, qat_ref, ki_ref, ka_ref, vat_ref, o_ref, sel_ref, qk_ref,*, topk):
    f32 = jnp.float32
    QB = Q_BLOCK
    t = pl.program_id(1) * QB + lax.broadcasted_iota(jnp.int32, (1, QB), 1)
    spos = lax.broadcasted_iota(jnp.int32, (W, QB), 0)
    causal = spos <= t

    ki = ki_ref[:W, :]
    G = 4
    kpad = jnp.zeros((LANES - IDX_DIM, G * QB), MXU_DTYPE)
    score = jnp.zeros((W, QB), f32)
    for hg in range(IDX_HEADS // G):
        q_idx = jnp.concatenate(
            [qit_ref[(hg * G + g) * IDX_DIM:(hg * G + g + 1) * IDX_DIM, :] for g in range(G)], axis=1)
        logits = _dot(ki, jnp.concatenate([q_idx, kpad], axis=0))
        for g in range(G):
            w_h = gwt_ref[GATE_ROWS + hg * G + g:GATE_ROWS + hg * G + g + 1, :]
            score = score + jnp.maximum(logits[:, g * QB:(g + 1) * QB], 0.0) * w_h
    score = jnp.where(causal, score, -jnp.inf)

    q_all = jnp.concatenate([qat_ref[h * LANES:(h + 1) * LANES, :] for h in range(DSA_HEADS)], axis=1)
    chunk = W // QK_SIDE_STEPS

    def qk_chunk(it):
        r0 = pl.multiple_of(it * chunk, chunk)
        qk_ref[pl.ds(r0, chunk), :] = _dot(ka_ref[pl.ds(r0, chunk), :], q_all)

    thr = _kth_largest(score, float(topk), QK_SIDE_STEPS, qk_chunk)
    gt = score > thr
    eq = (score == thr) & causal
    need = float(topk) - _col_sum(jnp.where(gt, 1.0, 0.0))
    n_eq = _col_sum(jnp.where(eq, 1.0, 0.0))
    sel_ref[:W, :] = jnp.where(gt | eq, 0.0, NEG_INF)

    @pl.when(jnp.max(n_eq - need) > 0.0)
    def _():
        cw = TIE_CHUNK
        r = lax.broadcasted_iota(jnp.int32, (cw, cw), 0)
        c = lax.broadcasted_iota(jnp.int32, (cw, cw), 1)
        lower = jnp.where(c <= r, 1.0, 0.0).astype(MXU_DTYPE)
        carry = jnp.zeros((1, QB), f32)
        for j in range(W // cw):
            sl = slice(j * cw, (j + 1) * cw)
            eq_j = jnp.where(eq[sl], 1.0, 0.0)
            prefix = _dot(lower, eq_j.astype(MXU_DTYPE)) + carry
            keep = gt[sl] | (eq[sl] & (prefix <= need))
            sel_ref[sl, :] = jnp.where(keep, 0.0, NEG_INF)
            carry = carry + jnp.sum(eq_j, axis=0, keepdims=True)

    G = 4
    va_t = vat_ref[:, :W]
    for hg in range(DSA_HEADS // G):
        s = qk_ref[:W, hg * G * QB:(hg + 1) * G * QB] + _lane_tile(sel_ref[:W, :], G)
        o_t = _softmax_pv([(s, va_t)])
        o_t = o_t[:HEAD_DIM] * (1.0 / o_t[HEAD_DIM:HEAD_DIM + 1])
        for g2 in range(G // 2):
            pair = jnp.concatenate([o_t[:, (2 * g2) * QB:(2 * g2 + 1) * QB],
                                    o_t[:, (2 * g2 + 1) * QB:(2 * g2 + 2) * QB]], axis=0)
            col = (hg * G + 2 * g2) * HEAD_DIM
            o_ref[:, col:col + 2 * HEAD_DIM] = pair.T.astype(o_ref.dtype)


def _dsa_kernel(qit_ref, gwt_ref, qat_ref, ki_ref, ka_ref, vat_ref, o_ref, sel_ref, qk_ref,*, topk, classes):
    qblk = pl.program_id(1)
    for lo, hi, W in classes:
        @pl.when((qblk >= lo) & (qblk < hi))
        def _(W=W):
            _dsa_body(W, qit_ref, gwt_ref, qat_ref, ki_ref, ka_ref, vat_ref, o_ref, sel_ref, qk_ref,topk=topk)


def _dsa(B, S, qi_t, gw_t, qa_t, ki, ka, va_t):
    nq = S // Q_BLOCK
    topk = min(DSA_TOPK_MAX, S // 4)
    fblk = lambda r: pl.BlockSpec((r, Q_BLOCK), lambda b, i: (0, b * nq + i))
    tseq = lambda w: pl.BlockSpec((S, w), lambda b, i: (b, 0))
    fseq = lambda r: pl.BlockSpec((r, S), lambda b, i: (0, b))
    return pl.pallas_call(
        functools.partial(_dsa_kernel, topk=topk, classes=_width_classes(nq, S)),
        grid=(B, nq),
        in_specs=[fblk(IDX_HEADS * IDX_DIM), fblk(LANES), fblk(DSA_HEADS * LANES),
                  tseq(LANES), tseq(LANES), fseq(LANES)],
        out_specs=pl.BlockSpec((Q_BLOCK, WIDTH_A), lambda b, i: (b * nq + i, 0)),
        out_shape=jax.ShapeDtypeStruct((B * S, WIDTH_A), MXU_DTYPE),
        scratch_shapes=[pltpu.VMEM((S, Q_BLOCK), jnp.float32),
                        pltpu.VMEM((S, DSA_HEADS * Q_BLOCK), jnp.float32)],
        compiler_params=_params(("parallel", "arbitrary")),
        name="dsa",
    )(qi_t, gw_t, qa_t, ki, ka, va_t)


def _nsa_body(W, first_blk, gwt_ref, qbt_ref, ks_ref, kw_ref, vst_ref, vwt_ref, kc_ref, vct_ref, ovl_ref, o_ref,
              *, seq_len):
    f32 = jnp.float32
    QB, G = Q_BLOCK, NSA_GROUP
    n_slc = seq_len // SLC_LEN
    n_top = min(SLC_TOPN, n_slc)
    win_len = WINDOW + QB
    nc = kc_ref.shape[1]

    t0 = pl.program_id(1) * QB
    t = t0 + lax.broadcasted_iota(jnp.int32, (1, QB), 1)
    t4 = _lane_tile(t, G)
    valid_c = lax.broadcasted_iota(jnp.int32, (nc, G * QB), 0) * CMP_STRIDE + (CMP_LEN - 1) <= t4
    past = first_blk * QB
    causal_tail = past + lax.broadcasted_iota(jnp.int32, (W - past, G * QB), 0) <= t4
    w0 = pl.multiple_of(jnp.maximum(t0 - WINDOW, 0), QB)
    full_window = past >= WINDOW
    if full_window:
        r_minus_lane = lax.broadcasted_iota(jnp.int32, (QB, G * QB), 0) - (t4 - t0)
        w_head_ok = r_minus_lane > 0
        w_tail_ok = r_minus_lane <= 0
    else:
        wd = t4 - (w0 + lax.broadcasted_iota(jnp.int32, (win_len, G * QB), 0))
        wvalid = (wd >= 0) & (wd < WINDOW)
    jblk = lax.broadcasted_iota(jnp.int32, (MAX_SLC_BLOCKS, QB), 0)
    cur = lax.shift_right_logical(t, 6)
    future = jblk > cur
    forced = (jblk == 0) | (jblk == cur) | (jblk == cur - 1)
    ovl = ovl_ref[...]

    for kh in range(NSA_KV_HEADS):
        hs = [kh * G + g for g in range(G)]
        q = jnp.concatenate([qbt_ref[h * LANES:(h + 1) * LANES, :] for h in hs], axis=1)

        s_c = jnp.where(valid_c, _dot(kc_ref[kh], q), NEG_INF)
        m_c = jnp.max(s_c, axis=0, keepdims=True)
        e_c = jnp.where(valid_c, jnp.exp2(s_c - m_c), 0.0)
        l_c = jnp.sum(e_c, axis=0, keepdims=True)
        p_c = e_c * (1.0 / jnp.where(l_c > 0.0, l_c, 1.0))
        o_cmp = _dot(vct_ref[kh], p_c.astype(MXU_DTYPE))

        p_sum = p_c[:, 0:QB] + p_c[:, QB:2 * QB] + p_c[:, 2 * QB:3 * QB] + p_c[:, 3 * QB:4 * QB]
        p_hi = p_sum.astype(MXU_DTYPE)
        p_lo = (p_sum - p_hi.astype(f32)).astype(MXU_DTYPE)
        imp = _dot(ovl, p_hi) + _dot(ovl, p_lo)
        val = jnp.where(future, -jnp.inf, imp + jnp.where(forced, FORCE_BONUS, 0.0))
        rank = jnp.zeros(val.shape, f32)
        for i in range(n_slc):
            vi = val[i:i + 1, :]
            rank = rank + jnp.where((vi > val) | ((vi == val) & (jblk > i)), 1.0, 0.0)
        bias = jnp.where((rank < n_top) & (jblk < n_slc), 0.0, -BLOCK_BIAS).astype(MXU_DTYPE)
        q_sel = jnp.concatenate([q[:BIAS_COL], _lane_tile(bias, G)], axis=0)

        ksl = slice(kh * LANES, (kh + 1) * LANES)
        s_s = _dot(ks_ref[:W, ksl], q_sel)
        parts = [(jnp.where(causal_tail, s_s[past:], NEG_INF), vst_ref[ksl, past:W])]
        if past:
            parts.append((s_s[:past], vst_ref[ksl, :past]))
        o_slc = _softmax_pv(parts)

        s_w = _dot(kw_ref[pl.ds(w0, win_len), ksl], q)
        v_w = vwt_ref[ksl, pl.ds(w0, win_len)]
        if full_window:
            parts = [(jnp.where(w_head_ok, s_w[:QB], NEG_INF), v_w[:, :QB]),
                     (s_w[QB:WINDOW], v_w[:, QB:WINDOW]),
                     (jnp.where(w_tail_ok, s_w[WINDOW:], NEG_INF), v_w[:, WINDOW:])]
        else:
            parts = [(jnp.where(wvalid, s_w, NEG_INF), v_w)]
        o_win = _softmax_pv(parts)

        inv_s = 1.0 / o_slc[HEAD_DIM:HEAD_DIM + 1]
        inv_w = 1.0 / o_win[HEAD_DIM:HEAD_DIM + 1]
        heads = []
        for g in range(G):
            h = hs[g]
            ls = slice(g * QB, (g + 1) * QB)
            g0 = gwt_ref[3 * h + 0:3 * h + 1, :]
            g1 = gwt_ref[3 * h + 1:3 * h + 2, :] * inv_s[:, ls]
            g2 = gwt_ref[3 * h + 2:3 * h + 3, :] * inv_w[:, ls]
            heads.append(g0 * o_cmp[:HEAD_DIM, ls] + g1 * o_slc[:HEAD_DIM, ls] + g2 * o_win[:HEAD_DIM, ls])
        for g2_ in range(G // 2):
            pair = jnp.concatenate([heads[2 * g2_], heads[2 * g2_ + 1]], axis=0)
            col = (kh * G + 2 * g2_) * HEAD_DIM
            o_ref[:, col:col + 2 * HEAD_DIM] = pair.T.astype(o_ref.dtype)


def _nsa_kernel(gwt_ref, qbt_ref, ks_ref, kw_ref, vst_ref, vwt_ref, kc_ref, vct_ref, ovl_ref, o_ref,
                *, seq_len, classes):
    qblk = pl.program_id(1)
    for lo, hi, W in classes:
        @pl.when((qblk >= lo) & (qblk < hi))
        def _(W=W, lo=lo):
            _nsa_body(W, lo, gwt_ref, qbt_ref, ks_ref, kw_ref, vst_ref, vwt_ref, kc_ref, vct_ref, ovl_ref, o_ref,
                      seq_len=seq_len)


def _nsa(B, S, gw_t, qb_t, ks, kw, vs_t, vw_t, kc, vc_t, ovl):
    nq = S // Q_BLOCK
    nc = kc.shape[2]
    kvw = NSA_KV_HEADS * LANES
    fblk = lambda r: pl.BlockSpec((r, Q_BLOCK), lambda b, i: (0, b * nq + i))
    tseq = lambda w: pl.BlockSpec((S, w), lambda b, i: (b, 0))
    fseq = lambda r: pl.BlockSpec((r, S), lambda b, i: (0, b))
    return pl.pallas_call(
        functools.partial(_nsa_kernel, seq_len=S, classes=_width_classes(nq, S)),
        grid=(B, nq),
        in_specs=[fblk(LANES), fblk(NSA_HEADS * LANES), tseq(kvw), tseq(kvw), fseq(kvw), fseq(kvw),
                  pl.BlockSpec((None, NSA_KV_HEADS, nc, LANES), lambda b, i: (b, 0, 0, 0)),
                  pl.BlockSpec((None, NSA_KV_HEADS, LANES, nc), lambda b, i: (b, 0, 0, 0)),
                  pl.BlockSpec(ovl.shape, lambda b, i: (0, 0))],
        out_specs=pl.BlockSpec((Q_BLOCK, WIDTH_B), lambda b, i: (b * nq + i, 0)),
        out_shape=jax.ShapeDtypeStruct((B * S, WIDTH_B), MXU_DTYPE),
        compiler_params=_params(("parallel", "arbitrary")),
        name="nsa",
    )(gw_t, qb_t, ks, kw, vs_t, vw_t, kc, vc_t, ovl)


def _merge_ffn_kernel(x_ref, oa_ref, ob_ref, gmix_ref, wg_ref, wua_ref, wub_ref, wo_ref,
                      gmlp_ref, w1_ref, w2_ref, o_ref, *, chunk):
    x = x_ref[...]
    D = x.shape[1]
    h = _rms(x, gmix_ref[...]).astype(MXU_DTYPE)
    ga = jax.nn.sigmoid(_dot(h, wg_ref[:, :D]))
    gb = jax.nn.sigmoid(_dot(h, wg_ref[:, D:]))
    merged = ga * _dot(oa_ref[...], wua_ref[...]) + gb * _dot(ob_ref[...], wub_ref[...])
    x1 = x + _dot(merged.astype(MXU_DTYPE), wo_ref[...])
    h2 = _rms(x1, gmlp_ref[...]).astype(MXU_DTYPE)
    acc = x1
    for c in range(w1_ref.shape[1] // chunk):
        u = jnp.maximum(_dot(h2, w1_ref[:, c * chunk:(c + 1) * chunk]), 0.0)
        acc = acc + _dot((u * u).astype(MXU_DTYPE), w2_ref[c * chunk:(c + 1) * chunk, :])
    o_ref[...] = acc


def _merge_ffn(x2, oa, ob, g_mix, w_gate, w_up_a, w_up_b, w_out, g_mlp, w1, w2, tm):
    T, D = x2.shape
    row = lambda w: pl.BlockSpec((tm, w), lambda i: (i, 0))
    full = lambda a: pl.BlockSpec(a.shape, lambda i: (0,) * a.ndim, pipeline_mode=pl.Buffered(1))
    ins = (x2, oa, ob, g_mix, w_gate, w_up_a, w_up_b, w_out, g_mlp, w1, w2)
    return pl.pallas_call(
        functools.partial(_merge_ffn_kernel, chunk=1024),
        grid=(T // tm,),
        in_specs=[row(D), row(WIDTH_A), row(WIDTH_B)] + [full(a) for a in ins[3:]],
        out_specs=row(D),
        out_shape=jax.ShapeDtypeStruct((T, D), jnp.float32),
        compiler_params=_params(("parallel",)),
        name="merge_ffn",
    )(*ins)


def _pad_heads(w, n_heads, axis):
    shape = list(w.shape)
    shape[axis:axis + 1] = [n_heads, HEAD_DIM]
    w = w.reshape(shape)
    pad = [(0, 0)] * w.ndim
    pad[axis + 1] = (0, LANES - HEAD_DIM)
    w = jnp.pad(w, pad)
    shape[axis:axis + 2] = [n_heads * LANES]
    return w.reshape(shape)


def _layer(x, g_mix, w_in, g_q_a, g_k_a, g_lat_a, w_uk_a, w_uv_a, g_q_b, g_kc_b, g_ks_b, g_kw_b,
           pe_cmp_b, phi_k1_b, phi_k2_b, phi_v1_b, phi_v2_b, w_up_a, w_up_b, w_out, g_mlp, w_ff1, w_ff2):
    B, S, D = x.shape
    T = B * S
    f32 = jnp.float32
    mx = MXU_DTYPE
    n_slc = S // SLC_LEN
    assert n_slc <= MAX_SLC_BLOCKS and S % (N_WIDTH_CLASSES * TIE_CHUNK) == 0 and S >= WINDOW + Q_BLOCK
    col_sizes = (WIDTH_A, DSA_LATENT, IDX_HEADS * IDX_DIM, IDX_DIM, IDX_HEADS,
                 WIDTH_B, 6 * NSA_KV_W, 3 * NSA_HEADS, 2 * D)
    offs = np.cumsum((0,) + col_sizes)
    w_qa, w_ca, w_qi, w_ki, w_wi, w_qb, w_kvb, w_gb, w_gm = [
        w_in[:, offs[i]:offs[i + 1]] for i in range(len(col_sizes))]
    w_kc, w_vc, w_ks, w_vs, w_kw, w_vw = [w_kvb[:, j * NSA_KV_W:(j + 1) * NSA_KV_W] for j in range(6)]
    padh = lambda w: _pad_heads(w, NSA_KV_HEADS, 1)
    w_tok = jnp.concatenate(
        [w_ki, jnp.zeros((D, LANES - IDX_DIM), f32), w_ca, padh(w_kc), padh(w_vc), padh(w_ks), padh(w_kw)], axis=1).astype(mx)
    assert w_tok.shape[1] == _T_COLS
    gw_pad = jnp.zeros((D, LANES - GATE_ROWS - IDX_HEADS), f32)
    w_feat = jnp.concatenate(
        [w_qi, w_gb, w_wi, gw_pad, w_qa, w_qb, w_vs, w_vw], axis=1).T.astype(mx)
    assert w_feat.shape[0] == _F_ROWS
    w_gate = w_gm.astype(mx)
    zpad = jnp.zeros((DSA_LATENT, LANES - HEAD_DIM), f32)
    w_uk = jnp.concatenate([w_uk_a, zpad], axis=1).astype(mx)
    w_uv_t = jnp.concatenate([w_uv_a, zpad], axis=1).T.astype(mx)
    row = lambda g: g.reshape(1, -1).astype(f32)
    rpad = lambda g: jnp.concatenate([row(g), jnp.zeros((1, LANES - HEAD_DIM), f32)], axis=1)
    colv = lambda g: g.reshape(-1, 1).astype(f32)

    x2 = x.reshape(T, D)
    (ki, ka, va_t, kcv, ks, kw, vs_t, vw_t, qi_t, gw_t, qa_t, qb_t) = _proj(
        x2, S, row(g_mix), w_tok, w_feat, row(g_lat_a), w_uk, w_uv_t, rpad(g_k_a), rpad(g_ks_b), rpad(g_kw_b),
        colv(g_q_a), colv(g_q_b), tm=256)

    n_chunk = S // CMP_STRIDE
    n_cmp = (S - CMP_LEN) // CMP_STRIDE + 1
    xc = kcv.reshape(2 * NSA_KV_HEADS, B, n_chunk, CMP_STRIDE * HEAD_DIM)
    pe1 = pe_cmp_b[:CMP_STRIDE].reshape(1, -1)
    pe2 = pe_cmp_b[CMP_STRIDE:].reshape(1, -1)
    phi1 = jnp.stack([phi_k1_b, phi_v1_b]).astype(mx)
    hpad = jnp.zeros((CMP_HIDDEN, LANES - HEAD_DIM), f32)
    w2k = jnp.concatenate([phi_k2_b, hpad], axis=1).astype(mx)
    w2v_t = jnp.concatenate([phi_v2_b, hpad], axis=1).T.astype(mx)
    kc, vc_t = _compress(xc, pe1, pe2, phi1, w2k, w2v_t, rpad(g_kc_b))

    c_i = np.arange(n_chunk)[None, :] * CMP_STRIDE
    j_i = np.arange(MAX_SLC_BLOCKS)[:, None] * SLC_LEN
    ovl = (c_i < j_i + SLC_LEN) & (c_i + CMP_LEN > j_i) & (c_i < n_cmp * CMP_STRIDE) & (j_i < S)
    ovl = jnp.asarray(ovl, f32).astype(mx)

    o_a = _dsa(B, S, qi_t, gw_t, qa_t, ki, ka, va_t)
    o_b = _nsa(B, S, gw_t, qb_t, ks, kw, vs_t, vw_t, kc, vc_t, ovl)

    out = _merge_ffn(x2, o_a, o_b, row(g_mix), w_gate, w_up_a.astype(mx), w_up_b.astype(mx), w_out.astype(mx),
                     row(g_mlp), w_ff1.astype(mx), w_ff2.astype(mx), tm=256)
    return out.reshape(B, S, D)


def kernel(x, g_mix, w_in, g_q_a, g_k_a, g_lat_a, w_uk_a, w_uv_a, g_q_b, g_kc_b, g_ks_b, g_kw_b, pe_cmp_b,
           phi_k1_b, phi_k2_b, phi_v1_b, phi_v2_b, w_up_a, w_up_b, w_out, g_mlp, w_ff1, w_ff2):
    params = (g_mix, w_in, g_q_a, g_k_a, g_lat_a, w_uk_a, w_uv_a, g_q_b, g_kc_b, g_ks_b, g_kw_b, pe_cmp_b,
              phi_k1_b, phi_k2_b, phi_v1_b, phi_v2_b, w_up_a, w_up_b, w_out, g_mlp, w_ff1, w_ff2)
    for l in range(g_mix.shape[0]):
        x = _layer(x, *[p[l] for p in params])
    return x
```

```python
import functools
import math

import numpy as np
import jax
import jax.numpy as jnp
from jax import lax
from jax.experimental import pallas as pl
from jax.experimental.pallas import tpu as pltpu

HEAD_DIM = 64
DSA_HEADS = 8
DSA_LATENT = 128
IDX_HEADS = 8
IDX_DIM = 32
DSA_TOPK_MAX = 256
NSA_HEADS = 8
NSA_KV_HEADS = 2
NSA_GROUP = NSA_HEADS // NSA_KV_HEADS
CMP_LEN = 32
CMP_STRIDE = 16
CMP_HIDDEN = 128
SLC_LEN = 64
SLC_TOPN = 16
WINDOW = 512
FORCE_BONUS = 1e6
Q_BLOCK = 128
RMS_EPS = 1e-6
NEG_INF = -1e30

WIDTH_A = DSA_HEADS * HEAD_DIM
WIDTH_B = NSA_HEADS * HEAD_DIM
NSA_KV_W = NSA_KV_HEADS * HEAD_DIM

LANES = 128
GATE_ROWS = 3 * NSA_HEADS
ALIBI_COL = HEAD_DIM
BIAS_COL = 96
MAX_SLC_BLOCKS = LANES - BIAS_COL
BLOCK_BIAS = 2.0 ** 100
LOG2E = math.log2(math.e)
MXU_DTYPE = jnp.bfloat16
VMEM_LIMIT = 52 * 1024 * 1024
N_WIDTH_CLASSES = 8
TIE_CHUNK = 256
REDUCE_ROWS = 64
QK_SIDE_STEPS = 8

_NT = (((1,), (1,)), ((), ()))


def _alibi_slopes(n_heads):
    return [float(v) for v in np.asarray(
        2.0 ** (-8.0 * np.arange(1, n_heads + 1) / n_heads), dtype=np.float32)]


def _dot(a, b):
    return jnp.dot(a, b, preferred_element_type=jnp.float32)


def _dot_nt(a, b):
    return lax.dot_general(a, b, _NT, preferred_element_type=jnp.float32)


def _rms(x, g):
    ms = jnp.mean(x * x, axis=-1, keepdims=True)
    return x * lax.rsqrt(ms + RMS_EPS) * g


def _params(sem):
    return pltpu.CompilerParams(dimension_semantics=sem, vmem_limit_bytes=VMEM_LIMIT)


def _split3(c):
    c1 = c.astype(MXU_DTYPE).astype(jnp.float32)
    c2 = (c - c1).astype(MXU_DTYPE).astype(jnp.float32)
    c3 = (c - c1 - c2).astype(MXU_DTYPE).astype(jnp.float32)
    return c1, c2, c3


_T_KI = (0, 128)
_T_CA = (128, 256)
_T_KCV = (256, 768)
_T_KS = (768, 1024)
_T_KW = (1024, 1280)
_T_COLS = 1280
_F_QI = (0, 256)
_F_GW = (256, 384)
_F_QA = (384, 896)
_F_QB = (896, 1408)
_F_VS = (1408, 1536)
_F_VW = (1536, 1664)
_F_ROWS = 1664


def _proj_kernel(x_ref, gmix_ref, wtok_ref, wfeat_ref, glat_ref, wuk_ref, wuvt_ref, gka_ref, gks_ref, gkw_ref,
                 gqa_ref, gqb_ref,
                 ki_ref, ka_ref, vat_ref, kcv_ref, ks_ref, kw_ref, vst_ref, vwt_ref, qit_ref, gwt_ref,
                 qat_ref, qbt_ref, *, seq_len, w_idx_scale, slopes_a, slopes_b):
    f32 = jnp.float32
    x = x_ref[...]
    tm = x.shape[0]
    h = _rms(x, gmix_ref[...]).astype(MXU_DTYPE)

    def tok(c):
        return _dot(h, wtok_ref[:, c[0]:c[1]])

    def feat(r):
        return _dot_nt(wfeat_ref[r[0]:r[1], :], h)

    s0 = (pl.program_id(0) % (seq_len // tm)) * tm
    spos = s0 + lax.broadcasted_iota(jnp.int32, (tm, LANES), 0)
    lane = lax.broadcasted_iota(jnp.int32, (tm, LANES), 1)
    c1, c2, c3 = _split3(spos.astype(f32) * LOG2E)
    alibi_cols = jnp.where(lane == ALIBI_COL, c1,
                           jnp.where(lane == ALIBI_COL + 1, c2, jnp.where(lane == ALIBI_COL + 2, c3, 0.0)))
    block_onehot = jnp.where(lane - BIAS_COL == lax.shift_right_logical(spos, 6), 1.0, 0.0)

    def key_slab(raw, g):
        ms = jnp.sum(raw * raw, axis=-1, keepdims=True) * (1.0 / HEAD_DIM)
        return raw * lax.rsqrt(ms + RMS_EPS) * g

    ki_ref[...] = tok(_T_KI).astype(ki_ref.dtype)
    c = _rms(tok(_T_CA), glat_ref[...]).astype(MXU_DTYPE)
    ka_ref[...] = (key_slab(_dot(c, wuk_ref[...]), gka_ref[...]) + alibi_cols).astype(ka_ref.dtype)
    va_t = _dot_nt(wuvt_ref[...], c)
    frow = lax.broadcasted_iota(jnp.int32, va_t.shape, 0)
    vat_ref[...] = jnp.where(frow == HEAD_DIM, 1.0, va_t).astype(vat_ref.dtype)
    for j in range(2 * NSA_KV_HEADS):
        kcv_ref[j] = tok((_T_KCV[0] + j * LANES, _T_KCV[0] + (j + 1) * LANES))[:, :HEAD_DIM]
    for kh in range(NSA_KV_HEADS):
        sl = slice(kh * LANES, (kh + 1) * LANES)
        ks = key_slab(tok((_T_KS[0] + kh * LANES, _T_KS[0] + (kh + 1) * LANES)), gks_ref[...])
        ks_ref[:, sl] = (ks + alibi_cols + block_onehot).astype(ks_ref.dtype)
        kw = key_slab(tok((_T_KW[0] + kh * LANES, _T_KW[0] + (kh + 1) * LANES)), gkw_ref[...])
        kw_ref[:, sl] = (kw + alibi_cols).astype(kw_ref.dtype)

    qit_ref[...] = feat(_F_QI).astype(qit_ref.dtype)
    gw = feat(_F_GW)
    grow = lax.broadcasted_iota(jnp.int32, gw.shape, 0)
    gwt_ref[...] = jnp.where(grow < GATE_ROWS, jax.nn.sigmoid(gw), gw * w_idx_scale)
    erow = lax.broadcasted_iota(jnp.int32, (HEAD_DIM, tm), 0)
    q_scale = HEAD_DIM ** -0.5 * LOG2E
    for rows, g_ref, out_ref, slopes in ((_F_QA, gqa_ref, qat_ref, slopes_a), (_F_QB, gqb_ref, qbt_ref, slopes_b)):
        q_all = feat(rows)
        for hd in range(len(slopes)):
            q = q_all[hd * HEAD_DIM:(hd + 1) * HEAD_DIM]
            ms = jnp.mean(q * q, axis=0, keepdims=True)
            q = q * lax.rsqrt(ms + RMS_EPS) * g_ref[...] * q_scale
            out_ref[hd * LANES:hd * LANES + HEAD_DIM, :] = q.astype(out_ref.dtype)
            extra = jnp.where(erow < 3, slopes[hd], 0.0)
            out_ref[hd * LANES + HEAD_DIM:(hd + 1) * LANES, :] = extra.astype(out_ref.dtype)
    ones_row = jnp.where(erow == 0, 1.0, 0.0)
    for rows, out_ref in ((_F_VS, vst_ref), (_F_VW, vwt_ref)):
        v_t = feat(rows)
        for kh in range(NSA_KV_HEADS):
            out_ref[kh * LANES:kh * LANES + HEAD_DIM, :] = v_t[kh * HEAD_DIM:(kh + 1) * HEAD_DIM].astype(out_ref.dtype)
            out_ref[kh * LANES + HEAD_DIM:(kh + 1) * LANES, :] = ones_row.astype(out_ref.dtype)


def _proj(x2, seq_len, g_mix, w_tok, w_feat, g_lat, w_uk, w_uv_t, gk_a, gks, gkw, gq_a, gq_b, tm):
    T, D = x2.shape
    tokm = lambda w: pl.BlockSpec((tm, w), lambda i: (i, 0))
    featm = lambda r: pl.BlockSpec((r, tm), lambda i: (0, i))
    full = lambda a: pl.BlockSpec(a.shape, lambda i: (0,) * a.ndim)
    f32 = jnp.float32
    mx = MXU_DTYPE
    out_shape = (
        jax.ShapeDtypeStruct((T, LANES), mx),
        jax.ShapeDtypeStruct((T, LANES), mx),
        jax.ShapeDtypeStruct((LANES, T), mx),
        jax.ShapeDtypeStruct((2 * NSA_KV_HEADS, T, HEAD_DIM), f32),
        jax.ShapeDtypeStruct((T, NSA_KV_HEADS * LANES), mx),
        jax.ShapeDtypeStruct((T, NSA_KV_HEADS * LANES), mx),
        jax.ShapeDtypeStruct((NSA_KV_HEADS * LANES, T), mx),
        jax.ShapeDtypeStruct((NSA_KV_HEADS * LANES, T), mx),
        jax.ShapeDtypeStruct((IDX_HEADS * IDX_DIM, T), mx),
        jax.ShapeDtypeStruct((LANES, T), f32),
        jax.ShapeDtypeStruct((DSA_HEADS * LANES, T), mx),
        jax.ShapeDtypeStruct((NSA_HEADS * LANES, T), mx),
    )
    out_specs = (
        tokm(LANES), tokm(LANES), featm(LANES),
        pl.BlockSpec((2 * NSA_KV_HEADS, tm, HEAD_DIM), lambda i: (0, i, 0)),
        tokm(NSA_KV_HEADS * LANES), tokm(NSA_KV_HEADS * LANES),
        featm(NSA_KV_HEADS * LANES), featm(NSA_KV_HEADS * LANES),
        featm(IDX_HEADS * IDX_DIM), featm(LANES), featm(DSA_HEADS * LANES), featm(NSA_HEADS * LANES),
    )
    kern = functools.partial(
        _proj_kernel, seq_len=seq_len, w_idx_scale=IDX_HEADS ** -0.5 * IDX_DIM ** -0.5,
        slopes_a=_alibi_slopes(DSA_HEADS), slopes_b=_alibi_slopes(NSA_HEADS))
    ins = (x2, g_mix, w_tok, w_feat, g_lat, w_uk, w_uv_t, gk_a, gks, gkw, gq_a, gq_b)
    return pl.pallas_call(
        kern,
        grid=(T // tm,),
        in_specs=[tokm(D)] + [full(a) for a in ins[1:]],
        out_specs=out_specs,
        out_shape=out_shape,
        compiler_params=_params(("parallel",)),
        name="proj",
    )(*ins)


def _compress_kernel(xk_ref, xv_ref, pe1_ref, pe2_ref, w1_ref, w2k_ref, w2vt_ref, gkc_ref, kc_ref, vct_ref):
    def hidden(x, w1):
        n, half = x.shape
        a = _dot((x + pe1_ref[...]).astype(MXU_DTYPE), w1[:half, :])
        b = _dot((x + pe2_ref[...]).astype(MXU_DTYPE), w1[half:, :])
        return jax.nn.gelu(a + pltpu.roll(b, n - 1, 0)).astype(MXU_DTYPE)

    yk = _dot(hidden(xk_ref[...], w1_ref[0]), w2k_ref[...])
    ms = jnp.sum(yk * yk, axis=-1, keepdims=True) * (1.0 / HEAD_DIM)
    kc_ref[...] = (yk * lax.rsqrt(ms + RMS_EPS) * gkc_ref[...]).astype(kc_ref.dtype)
    vct_ref[...] = _dot_nt(w2vt_ref[...], hidden(xv_ref[...], w1_ref[1])).astype(vct_ref.dtype)


def _compress(xc, pe1, pe2, phi1, w2k, w2v_t, g_kc):
    _, B, n, w = xc.shape
    full = lambda a: pl.BlockSpec(a.shape, lambda b, kh: (0,) * a.ndim)
    return pl.pallas_call(
        _compress_kernel,
        grid=(B, NSA_KV_HEADS),
        in_specs=[
            pl.BlockSpec((None, None, n, w), lambda b, kh: (kh, b, 0, 0)),
            pl.BlockSpec((None, None, n, w), lambda b, kh: (NSA_KV_HEADS + kh, b, 0, 0)),
            full(pe1), full(pe2), full(phi1), full(w2k), full(w2v_t), full(g_kc),
        ],
        out_specs=(pl.BlockSpec((None, None, n, LANES), lambda b, kh: (b, kh, 0, 0)),
                   pl.BlockSpec((None, None, LANES, n), lambda b, kh: (b, kh, 0, 0))),
        out_shape=(jax.ShapeDtypeStruct((B, NSA_KV_HEADS, n, LANES), MXU_DTYPE),
                   jax.ShapeDtypeStruct((B, NSA_KV_HEADS, LANES, n), MXU_DTYPE)),
        compiler_params=_params(("parallel", "parallel")),
        name="compress",
    )(xc, xc, pe1, pe2, phi1, w2k, w2v_t, g_kc)


_INT_MIN = -2 ** 31


def _key_to_float(u):
    key = u ^ jnp.int32(_INT_MIN)
    bits = jnp.where(key >= 0, key, key ^ jnp.int32(0x7FFFFFFF))
    return lax.bitcast_convert_type(bits, jnp.float32)


def _col_reduce(x, op, final):
    rows = REDUCE_ROWS
    acc = x
    if x.shape[0] > rows and x.shape[0] % rows == 0:
        acc = x[:rows]
        for j in range(1, x.shape[0] // rows):
            acc = op(acc, x[j * rows:(j + 1) * rows])
    return final(acc.astype(jnp.float32), axis=0, keepdims=True)


def _col_sum(x):
    return _col_reduce(x, jnp.add, jnp.sum)


def _col_max(x):
    return _col_reduce(x, jnp.maximum, jnp.max)


def _kth_largest(score, k, side_steps=0, side_work=None):
    def body(it, u):
        u_try = u | lax.shift_left(jnp.int32(1), 31 - it)
        cnt = _col_sum(jnp.where(score >= _key_to_float(u_try), 1.0, 0.0))
        return jnp.where(cnt >= k, u_try, u)

    def body_with_side_work(j, u):
        side_work(j)
        for i in range(32 // side_steps):
            u = body(j * (32 // side_steps) + i, u)
        return u

    u = jnp.zeros((1, score.shape[1]), jnp.int32)
    if side_steps:
        u = lax.fori_loop(0, side_steps, body_with_side_work, u)
    else:
        u = lax.fori_loop(0, 32, body, u)
    thr = _key_to_float(u)
    return jnp.where(thr != thr, -jnp.inf, thr)


def _lane_tile(x, n):
    return jnp.concatenate([x] * n, axis=1)


def _softmax_pv(parts):
    m = None
    for s, _ in parts:
        m_p = _col_max(s)
        m = m_p if m is None else jnp.maximum(m, m_p)
    out = None
    for s, v_t in parts:
        o_p = _dot(v_t, jnp.exp2(s - m).astype(MXU_DTYPE))
        out = o_p if out is None else out + o_p
    return out


def _width_classes(n_qblk, seq_len):
    n_cls = N_WIDTH_CLASSES if n_qblk % N_WIDTH_CLASSES == 0 else 1
    per = n_qblk // n_cls
    return [(c * per, (c + 1) * per, (c + 1) * per * Q_BLOCK) for c in range(n_cls)]


def _dsa_body(W, qit_ref, gwt_ref, qat_ref, ki_ref, ka_ref, vat_ref, o_ref, sel_ref, qk_ref,*, topk):
    f32 = jnp.float32
    QB = Q_BLOCK
    t = pl.program_id(1) * QB + lax.broadcasted_iota(jnp.int32, (1, QB), 1)
    spos = lax.broadcasted_iota(jnp.int32, (W, QB), 0)
    causal = spos <= t

    ki = ki_ref[:W, :]
    G = 4
    kpad = jnp.zeros((LANES - IDX_DIM, G * QB), MXU_DTYPE)
    score = jnp.zeros((W, QB), f32)
    for hg in range(IDX_HEADS // G):
        q_idx = jnp.concatenate(
            [qit_ref[(hg * G + g) * IDX_DIM:(hg * G + g + 1) * IDX_DIM, :] for g in range(G)], axis=1)
        logits = _dot(ki, jnp.concatenate([q_idx, kpad], axis=0))
        for g in range(G):
            w_h = gwt_ref[GATE_ROWS + hg * G + g:GATE_ROWS + hg * G + g + 1, :]
            score = score + jnp.maximum(logits[:, g * QB:(g + 1) * QB], 0.0) * w_h
    score = jnp.where(causal, score, -jnp.inf)

    q_all = jnp.concatenate([qat_ref[h * LANES:(h + 1) * LANES, :] for h in range(DSA_HEADS)], axis=1)
    chunk = W // QK_SIDE_STEPS

    def qk_chunk(it):
        r0 = pl.multiple_of(it * chunk, chunk)
        qk_ref[pl.ds(r0, chunk), :] = _dot(ka_ref[pl.ds(r0, chunk), :], q_all)

    thr = _kth_largest(score, float(topk), QK_SIDE_STEPS, qk_chunk)
    gt = score > thr
    eq = (score == thr) & causal
    need = float(topk) - _col_sum(jnp.where(gt, 1.0, 0.0))
    n_eq = _col_sum(jnp.where(eq, 1.0, 0.0))
    sel_ref[:W, :] = jnp.where(gt | eq, 0.0, NEG_INF)

    @pl.when(jnp.max(n_eq - need) > 0.0)
    def _():
        cw = TIE_CHUNK
        r = lax.broadcasted_iota(jnp.int32, (cw, cw), 0)
        c = lax.broadcasted_iota(jnp.int32, (cw, cw), 1)
        lower = jnp.where(c <= r, 1.0, 0.0).astype(MXU_DTYPE)
        carry = jnp.zeros((1, QB), f32)
        for j in range(W // cw):
            sl = slice(j * cw, (j + 1) * cw)
            eq_j = jnp.where(eq[sl], 1.0, 0.0)
            prefix = _dot(lower, eq_j.astype(MXU_DTYPE)) + carry
            keep = gt[sl] | (eq[sl] & (prefix <= need))
            sel_ref[sl, :] = jnp.where(keep, 0.0, NEG_INF)
            carry = carry + jnp.sum(eq_j, axis=0, keepdims=True)

    G = 4
    va_t = vat_ref[:, :W]
    HG = range(DSA_HEADS // G)
    s = [qk_ref[:W, hg * G * QB:(hg + 1) * G * QB] + _lane_tile(sel_ref[:W, :], G) for hg in HG]
    m = [_col_max(s[hg]) for hg in HG]
    e = [jnp.exp2(s[hg] - m[hg]).astype(MXU_DTYPE) for hg in HG]
    o = [_dot(va_t, e[hg]) for hg in HG]
    for hg in HG:
        o_t = o[hg][:HEAD_DIM] * (1.0 / o[hg][HEAD_DIM:HEAD_DIM + 1])
        for g2 in range(G // 2):
            pair = jnp.concatenate([o_t[:, (2 * g2) * QB:(2 * g2 + 1) * QB],
                                    o_t[:, (2 * g2 + 1) * QB:(2 * g2 + 2) * QB]], axis=0)
            col = (hg * G + 2 * g2) * HEAD_DIM
            o_ref[:, col:col + 2 * HEAD_DIM] = pair.T.astype(o_ref.dtype)


def _dsa_kernel(qit_ref, gwt_ref, qat_ref, ki_ref, ka_ref, vat_ref, o_ref, sel_ref, qk_ref,*, topk, classes):
    qblk = pl.program_id(1)
    for lo, hi, W in classes:
        @pl.when((qblk >= lo) & (qblk < hi))
        def _(W=W):
            _dsa_body(W, qit_ref, gwt_ref, qat_ref, ki_ref, ka_ref, vat_ref, o_ref, sel_ref, qk_ref,topk=topk)


def _dsa(B, S, qi_t, gw_t, qa_t, ki, ka, va_t):
    nq = S // Q_BLOCK
    topk = min(DSA_TOPK_MAX, S // 4)
    fblk = lambda r: pl.BlockSpec((r, Q_BLOCK), lambda b, i: (0, b * nq + i))
    tseq = lambda w: pl.BlockSpec((S, w), lambda b, i: (b, 0))
    fseq = lambda r: pl.BlockSpec((r, S), lambda b, i: (0, b))
    return pl.pallas_call(
        functools.partial(_dsa_kernel, topk=topk, classes=_width_classes(nq, S)),
        grid=(B, nq),
        in_specs=[fblk(IDX_HEADS * IDX_DIM), fblk(LANES), fblk(DSA_HEADS * LANES),
                  tseq(LANES), tseq(LANES), fseq(LANES)],
        out_specs=pl.BlockSpec((Q_BLOCK, WIDTH_A), lambda b, i: (b * nq + i, 0)),
        out_shape=jax.ShapeDtypeStruct((B * S, WIDTH_A), MXU_DTYPE),
        scratch_shapes=[pltpu.VMEM((S, Q_BLOCK), jnp.float32),
                        pltpu.VMEM((S, DSA_HEADS * Q_BLOCK), jnp.float32)],
        compiler_params=_params(("parallel", "arbitrary")),
        name="dsa",
    )(qi_t, gw_t, qa_t, ki, ka, va_t)


def _nsa_body(W, first_blk, gwt_ref, qbt_ref, ks_ref, kw_ref, vst_ref, vwt_ref, kc_ref, vct_ref, ovl_ref, o_ref,
              *, seq_len):
    f32 = jnp.float32
    QB, G = Q_BLOCK, NSA_GROUP
    n_slc = seq_len // SLC_LEN
    n_top = min(SLC_TOPN, n_slc)
    win_len = WINDOW + QB
    nc = kc_ref.shape[1]

    t0 = pl.program_id(1) * QB
    t = t0 + lax.broadcasted_iota(jnp.int32, (1, QB), 1)
    t4 = _lane_tile(t, G)
    valid_c = lax.broadcasted_iota(jnp.int32, (nc, G * QB), 0) * CMP_STRIDE + (CMP_LEN - 1) <= t4
    past = first_blk * QB
    causal_tail = past + lax.broadcasted_iota(jnp.int32, (W - past, G * QB), 0) <= t4
    w0 = pl.multiple_of(jnp.maximum(t0 - WINDOW, 0), QB)
    full_window = past >= WINDOW
    if full_window:
        r_minus_lane = lax.broadcasted_iota(jnp.int32, (QB, G * QB), 0) - (t4 - t0)
        w_head_ok = r_minus_lane > 0
        w_tail_ok = r_minus_lane <= 0
    else:
        wd = t4 - (w0 + lax.broadcasted_iota(jnp.int32, (win_len, G * QB), 0))
        wvalid = (wd >= 0) & (wd < WINDOW)
    jblk = lax.broadcasted_iota(jnp.int32, (MAX_SLC_BLOCKS, QB), 0)
    cur = lax.shift_right_logical(t, 6)
    future = jblk > cur
    forced = (jblk == 0) | (jblk == cur) | (jblk == cur - 1)
    ovl = ovl_ref[...]

    KH = range(NSA_KV_HEADS)
    ksl = [slice(kh * LANES, (kh + 1) * LANES) for kh in KH]
    q = [jnp.concatenate([qbt_ref[(kh * G + g) * LANES:(kh * G + g + 1) * LANES, :] for g in range(G)], axis=1)
         for kh in KH]

    s_c = [jnp.where(valid_c, _dot(kc_ref[kh], q[kh]), NEG_INF) for kh in KH]
    s_w = [_dot(kw_ref[pl.ds(w0, win_len), ksl[kh]], q[kh]) for kh in KH]
    e_c = [jnp.where(valid_c, jnp.exp2(s_c[kh] - jnp.max(s_c[kh], axis=0, keepdims=True)), 0.0) for kh in KH]
    l_c = [jnp.sum(e_c[kh], axis=0, keepdims=True) for kh in KH]
    p_c = [e_c[kh] * (1.0 / jnp.where(l_c[kh] > 0.0, l_c[kh], 1.0)) for kh in KH]
    o_cmp = [_dot(vct_ref[kh], p_c[kh].astype(MXU_DTYPE)) for kh in KH]

    val = []
    for kh in KH:
        pc = p_c[kh]
        p_sum = pc[:, 0:QB] + pc[:, QB:2 * QB] + pc[:, 2 * QB:3 * QB] + pc[:, 3 * QB:4 * QB]
        p_hi = p_sum.astype(MXU_DTYPE)
        p_lo = (p_sum - p_hi.astype(f32)).astype(MXU_DTYPE)
        imp = _dot(ovl, p_hi) + _dot(ovl, p_lo)
        val.append(jnp.where(future, -jnp.inf, imp + jnp.where(forced, FORCE_BONUS, 0.0)))

    o_win = []
    for kh in KH:
        v_w = vwt_ref[ksl[kh], pl.ds(w0, win_len)]
        if full_window:
            parts = [(jnp.where(w_head_ok, s_w[kh][:QB], NEG_INF), v_w[:, :QB]),
                     (s_w[kh][QB:WINDOW], v_w[:, QB:WINDOW]),
                     (jnp.where(w_tail_ok, s_w[kh][WINDOW:], NEG_INF), v_w[:, WINDOW:])]
        else:
            parts = [(jnp.where(wvalid, s_w[kh], NEG_INF), v_w)]
        o_win.append(_softmax_pv(parts))

    rank = [jnp.zeros(val[kh].shape, f32) for kh in KH]
    for i in range(n_slc):
        for kh in KH:
            vi = val[kh][i:i + 1, :]
            rank[kh] = rank[kh] + jnp.where((vi > val[kh]) | ((vi == val[kh]) & (jblk > i)), 1.0, 0.0)

    s_s = []
    for kh in KH:
        bias = jnp.where((rank[kh] < n_top) & (jblk < n_slc), 0.0, -BLOCK_BIAS).astype(MXU_DTYPE)
        q_sel = jnp.concatenate([q[kh][:BIAS_COL], _lane_tile(bias, G)], axis=0)
        s_s.append(_dot(ks_ref[:W, ksl[kh]], q_sel))
    o_slc = []
    for kh in KH:
        parts = [(jnp.where(causal_tail, s_s[kh][past:], NEG_INF), vst_ref[ksl[kh], past:W])]
        if past:
            parts.append((s_s[kh][:past], vst_ref[ksl[kh], :past]))
        o_slc.append(_softmax_pv(parts))

    for kh in KH:
        inv_s = 1.0 / o_slc[kh][HEAD_DIM:HEAD_DIM + 1]
        inv_w = 1.0 / o_win[kh][HEAD_DIM:HEAD_DIM + 1]
        heads = []
        for g in range(G):
            h = kh * G + g
            ls = slice(g * QB, (g + 1) * QB)
            g0 = gwt_ref[3 * h + 0:3 * h + 1, :]
            g1 = gwt_ref[3 * h + 1:3 * h + 2, :] * inv_s[:, ls]
            g2 = gwt_ref[3 * h + 2:3 * h + 3, :] * inv_w[:, ls]
            heads.append(g0 * o_cmp[kh][:HEAD_DIM, ls] + g1 * o_slc[kh][:HEAD_DIM, ls]
                         + g2 * o_win[kh][:HEAD_DIM, ls])
        for g2_ in range(G // 2):
            pair = jnp.concatenate([heads[2 * g2_], heads[2 * g2_ + 1]], axis=0)
            col = (kh * G + 2 * g2_) * HEAD_DIM
            o_ref[:, col:col + 2 * HEAD_DIM] = pair.T.astype(o_ref.dtype)


def _nsa_kernel(gwt_ref, qbt_ref, ks_ref, kw_ref, vst_ref, vwt_ref, kc_ref, vct_ref, ovl_ref, o_ref,
                *, seq_len, classes):
    qblk = pl.program_id(1)
    for lo, hi, W in classes:
        @pl.when((qblk >= lo) & (qblk < hi))
        def _(W=W, lo=lo):
            _nsa_body(W, lo, gwt_ref, qbt_ref, ks_ref, kw_ref, vst_ref, vwt_ref, kc_ref, vct_ref, ovl_ref, o_ref,
                      seq_len=seq_len)


def _nsa(B, S, gw_t, qb_t, ks, kw, vs_t, vw_t, kc, vc_t, ovl):
    nq = S // Q_BLOCK
    nc = kc.shape[2]
    kvw = NSA_KV_HEADS * LANES
    fblk = lambda r: pl.BlockSpec((r, Q_BLOCK), lambda b, i: (0, b * nq + i))
    tseq = lambda w: pl.BlockSpec((S, w), lambda b, i: (b, 0))
    fseq = lambda r: pl.BlockSpec((r, S), lambda b, i: (0, b))
    return pl.pallas_call(
        functools.partial(_nsa_kernel, seq_len=S, classes=_width_classes(nq, S)),
        grid=(B, nq),
        in_specs=[fblk(LANES), fblk(NSA_HEADS * LANES), tseq(kvw), tseq(kvw), fseq(kvw), fseq(kvw),
                  pl.BlockSpec((None, NSA_KV_HEADS, nc, LANES), lambda b, i: (b, 0, 0, 0)),
                  pl.BlockSpec((None, NSA_KV_HEADS, LANES, nc), lambda b, i: (b, 0, 0, 0)),
                  pl.BlockSpec(ovl.shape, lambda b, i: (0, 0))],
        out_specs=pl.BlockSpec((Q_BLOCK, WIDTH_B), lambda b, i: (b * nq + i, 0)),
        out_shape=jax.ShapeDtypeStruct((B * S, WIDTH_B), MXU_DTYPE),
        compiler_params=_params(("parallel", "arbitrary")),
        name="nsa",
    )(gw_t, qb_t, ks, kw, vs_t, vw_t, kc, vc_t, ovl)


def _merge_ffn_kernel(x_ref, oa_ref, ob_ref, gmix_ref, wg_ref, wua_ref, wub_ref, wo_ref,
                      gmlp_ref, w1_ref, w2_ref, o_ref, *, chunk):
    x = x_ref[...]
    D = x.shape[1]
    h = _rms(x, gmix_ref[...]).astype(MXU_DTYPE)
    ga = jax.nn.sigmoid(_dot(h, wg_ref[:, :D]))
    gb = jax.nn.sigmoid(_dot(h, wg_ref[:, D:]))
    merged = ga * _dot(oa_ref[...], wua_ref[...]) + gb * _dot(ob_ref[...], wub_ref[...])
    x1 = x + _dot(merged.astype(MXU_DTYPE), wo_ref[...])
    h2 = _rms(x1, gmlp_ref[...]).astype(MXU_DTYPE)
    acc = x1
    for c in range(w1_ref.shape[1] // chunk):
        u = jnp.maximum(_dot(h2, w1_ref[:, c * chunk:(c + 1) * chunk]), 0.0)
        acc = acc + _dot((u * u).astype(MXU_DTYPE), w2_ref[c * chunk:(c + 1) * chunk, :])
    o_ref[...] = acc


def _merge_ffn(x2, oa, ob, g_mix, w_gate, w_up_a, w_up_b, w_out, g_mlp, w1, w2, tm):
    T, D = x2.shape
    row = lambda w: pl.BlockSpec((tm, w), lambda i: (i, 0))
    full = lambda a: pl.BlockSpec(a.shape, lambda i: (0,) * a.ndim, pipeline_mode=pl.Buffered(1))
    ins = (x2, oa, ob, g_mix, w_gate, w_up_a, w_up_b, w_out, g_mlp, w1, w2)
    return pl.pallas_call(
        functools.partial(_merge_ffn_kernel, chunk=1024),
        grid=(T // tm,),
        in_specs=[row(D), row(WIDTH_A), row(WIDTH_B)] + [full(a) for a in ins[3:]],
        out_specs=row(D),
        out_shape=jax.ShapeDtypeStruct((T, D), jnp.float32),
        compiler_params=_params(("parallel",)),
        name="merge_ffn",
    )(*ins)


def _pad_heads(w, n_heads, axis):
    shape = list(w.shape)
    shape[axis:axis + 1] = [n_heads, HEAD_DIM]
    w = w.reshape(shape)
    pad = [(0, 0)] * w.ndim
    pad[axis + 1] = (0, LANES - HEAD_DIM)
    w = jnp.pad(w, pad)
    shape[axis:axis + 2] = [n_heads * LANES]
    return w.reshape(shape)


def _layer(x, g_mix, w_in, g_q_a, g_k_a, g_lat_a, w_uk_a, w_uv_a, g_q_b, g_kc_b, g_ks_b, g_kw_b,
           pe_cmp_b, phi_k1_b, phi_k2_b, phi_v1_b, phi_v2_b, w_up_a, w_up_b, w_out, g_mlp, w_ff1, w_ff2):
    B, S, D = x.shape
    T = B * S
    f32 = jnp.float32
    mx = MXU_DTYPE
    n_slc = S // SLC_LEN
    assert n_slc <= MAX_SLC_BLOCKS and S % (N_WIDTH_CLASSES * TIE_CHUNK) == 0 and S >= WINDOW + Q_BLOCK
    col_sizes = (WIDTH_A, DSA_LATENT, IDX_HEADS * IDX_DIM, IDX_DIM, IDX_HEADS,
                 WIDTH_B, 6 * NSA_KV_W, 3 * NSA_HEADS, 2 * D)
    offs = np.cumsum((0,) + col_sizes)
    w_qa, w_ca, w_qi, w_ki, w_wi, w_qb, w_kvb, w_gb, w_gm = [
        w_in[:, offs[i]:offs[i + 1]] for i in range(len(col_sizes))]
    w_kc, w_vc, w_ks, w_vs, w_kw, w_vw = [w_kvb[:, j * NSA_KV_W:(j + 1) * NSA_KV_W] for j in range(6)]
    padh = lambda w: _pad_heads(w, NSA_KV_HEADS, 1)
    w_tok = jnp.concatenate(
        [w_ki, jnp.zeros((D, LANES - IDX_DIM), f32), w_ca, padh(w_kc), padh(w_vc), padh(w_ks), padh(w_kw)], axis=1).astype(mx)
    assert w_tok.shape[1] == _T_COLS
    gw_pad = jnp.zeros((D, LANES - GATE_ROWS - IDX_HEADS), f32)
    w_feat = jnp.concatenate(
        [w_qi, w_gb, w_wi, gw_pad, w_qa, w_qb, w_vs, w_vw], axis=1).T.astype(mx)
    assert w_feat.shape[0] == _F_ROWS
    w_gate = w_gm.astype(mx)
    zpad = jnp.zeros((DSA_LATENT, LANES - HEAD_DIM), f32)
    w_uk = jnp.concatenate([w_uk_a, zpad], axis=1).astype(mx)
    w_uv_t = jnp.concatenate([w_uv_a, zpad], axis=1).T.astype(mx)
    row = lambda g: g.reshape(1, -1).astype(f32)
    rpad = lambda g: jnp.concatenate([row(g), jnp.zeros((1, LANES - HEAD_DIM), f32)], axis=1)
    colv = lambda g: g.reshape(-1, 1).astype(f32)

    x2 = x.reshape(T, D)
    (ki, ka, va_t, kcv, ks, kw, vs_t, vw_t, qi_t, gw_t, qa_t, qb_t) = _proj(
        x2, S, row(g_mix), w_tok, w_feat, row(g_lat_a), w_uk, w_uv_t, rpad(g_k_a), rpad(g_ks_b), rpad(g_kw_b),
        colv(g_q_a), colv(g_q_b), tm=256)

    n_chunk = S // CMP_STRIDE
    n_cmp = (S - CMP_LEN) // CMP_STRIDE + 1
    xc = kcv.reshape(2 * NSA_KV_HEADS, B, n_chunk, CMP_STRIDE * HEAD_DIM)
    pe1 = pe_cmp_b[:CMP_STRIDE].reshape(1, -1)
    pe2 = pe_cmp_b[CMP_STRIDE:].reshape(1, -1)
    phi1 = jnp.stack([phi_k1_b, phi_v1_b]).astype(mx)
    hpad = jnp.zeros((CMP_HIDDEN, LANES - HEAD_DIM), f32)
    w2k = jnp.concatenate([phi_k2_b, hpad], axis=1).astype(mx)
    w2v_t = jnp.concatenate([phi_v2_b, hpad], axis=1).T.astype(mx)
    kc, vc_t = _compress(xc, pe1, pe2, phi1, w2k, w2v_t, rpad(g_kc_b))

    c_i = np.arange(n_chunk)[None, :] * CMP_STRIDE
    j_i = np.arange(MAX_SLC_BLOCKS)[:, None] * SLC_LEN
    ovl = (c_i < j_i + SLC_LEN) & (c_i + CMP_LEN > j_i) & (c_i < n_cmp * CMP_STRIDE) & (j_i < S)
    ovl = jnp.asarray(ovl, f32).astype(mx)

    o_a = _dsa(B, S, qi_t, gw_t, qa_t, ki, ka, va_t)
    o_b = _nsa(B, S, gw_t, qb_t, ks, kw, vs_t, vw_t, kc, vc_t, ovl)

    out = _merge_ffn(x2, o_a, o_b, row(g_mix), w_gate, w_up_a.astype(mx), w_up_b.astype(mx), w_out.astype(mx),
                     row(g_mlp), w_ff1.astype(mx), w_ff2.astype(mx), tm=256)
    return out.reshape(B, S, D)


def kernel(x, g_mix, w_in, g_q_a, g_k_a, g_lat_a, w_uk_a, w_uv_a, g_q_b, g_kc_b, g_ks_b, g_kw_b, pe_cmp_b,
           phi_k1_b, phi_k2_b, phi_v1_b, phi_v2_b, w_up_a, w_up_b, w_out, g_mlp, w_ff1, w_ff2):
    params = (g_mix, w_in, g_q_a, g_k_a, g_lat_a, w_uk_a, w_uv_a, g_q_b, g_kc_b, g_ks_b, g_kw_b, pe_cmp_b,
              phi_k1_b, phi_k2_b, phi_v1_b, phi_v2_b, w_up_a, w_up_b, w_out, g_mlp, w_ff1, w_ff2)
    for l in range(g_mix.shape[0]):
        x = _layer(x, *[p[l] for p in params])
    return x
```

```python
import functools
import math

import numpy as np
import jax
import jax.numpy as jnp
from jax import lax
from jax.experimental import pallas as pl
from jax.experimental.pallas import tpu as pltpu

HEAD_DIM = 64
DSA_HEADS = 8
DSA_LATENT = 128
IDX_HEADS = 8
IDX_DIM = 32
DSA_TOPK_MAX = 256
NSA_HEADS = 8
NSA_KV_HEADS = 2
NSA_GROUP = NSA_HEADS // NSA_KV_HEADS
CMP_LEN = 32
CMP_STRIDE = 16
CMP_HIDDEN = 128
SLC_LEN = 64
SLC_TOPN = 16
WINDOW = 512
FORCE_BONUS = 1e6
Q_BLOCK = 128
RMS_EPS = 1e-6
NEG_INF = -1e30

WIDTH_A = DSA_HEADS * HEAD_DIM
WIDTH_B = NSA_HEADS * HEAD_DIM
NSA_KV_W = NSA_KV_HEADS * HEAD_DIM

LANES = 128
GATE_ROWS = 3 * NSA_HEADS
ALIBI_COL = HEAD_DIM
BIAS_COL = 96
MAX_SLC_BLOCKS = LANES - BIAS_COL
BLOCK_BIAS = 2.0 ** 100
LOG2E = math.log2(math.e)
MXU_DTYPE = jnp.bfloat16
VMEM_LIMIT = 52 * 1024 * 1024
N_WIDTH_CLASSES = 8
TIE_CHUNK = 256
REDUCE_ROWS = 64
QK_SIDE_STEPS = 8

_NT = (((1,), (1,)), ((), ()))


def _alibi_slopes(n_heads):
    return [float(v) for v in np.asarray(
        2.0 ** (-8.0 * np.arange(1, n_heads + 1) / n_heads), dtype=np.float32)]


def _dot(a, b):
    return jnp.dot(a, b, preferred_element_type=jnp.float32)


def _dot_nt(a, b):
    return lax.dot_general(a, b, _NT, preferred_element_type=jnp.float32)


def _rms(x, g):
    ms = jnp.mean(x * x, axis=-1, keepdims=True)
    return x * lax.rsqrt(ms + RMS_EPS) * g


def _params(sem):
    return pltpu.CompilerParams(dimension_semantics=sem, vmem_limit_bytes=VMEM_LIMIT)


def _split3(c):
    c1 = c.astype(MXU_DTYPE).astype(jnp.float32)
    c2 = (c - c1).astype(MXU_DTYPE).astype(jnp.float32)
    c3 = (c - c1 - c2).astype(MXU_DTYPE).astype(jnp.float32)
    return c1, c2, c3


_T_KI = (0, 128)
_T_CA = (128, 256)
_T_KCV = (256, 512)
_T_KS = (512, 640)
_T_KW = (640, 768)
_T_COLS = 768
_F_QI = (0, 256)
_F_GW = (256, 384)
_F_QA = (384, 896)
_F_QB = (896, 1408)
_F_VS = (1408, 1536)
_F_VW = (1536, 1664)
_F_ROWS = 1664


def _proj_kernel(x_ref, gmix_ref, wtok_ref, wfeat_ref, glat_ref, wuk_ref, wuvt_ref, gka_ref, gks_ref, gkw_ref,
                 gqa_ref, gqb_ref,
                 ki_ref, ka_ref, vat_ref, kcv_ref, ks_ref, kw_ref, vst_ref, vwt_ref, qit_ref, gwt_ref,
                 qat_ref, qbt_ref, *, seq_len, w_idx_scale, slopes_a, slopes_b):
    f32 = jnp.float32
    x = x_ref[...]
    tm = x.shape[0]
    h = _rms(x, gmix_ref[...]).astype(MXU_DTYPE)

    def tok(c):
        return _dot(h, wtok_ref[:, c[0]:c[1]])

    def feat(r):
        return _dot_nt(wfeat_ref[r[0]:r[1], :], h)

    s0 = (pl.program_id(0) % (seq_len // tm)) * tm
    spos = s0 + lax.broadcasted_iota(jnp.int32, (tm, LANES), 0)
    lane = lax.broadcasted_iota(jnp.int32, (tm, LANES), 1)
    c1, c2, c3 = _split3(spos.astype(f32) * LOG2E)
    alibi_cols = jnp.where(lane == ALIBI_COL, c1,
                           jnp.where(lane == ALIBI_COL + 1, c2, jnp.where(lane == ALIBI_COL + 2, c3, 0.0)))
    block_onehot = jnp.where(lane - BIAS_COL == lax.shift_right_logical(spos, 6), 1.0, 0.0)

    def key_slab(raw, g):
        ms = jnp.sum(raw * raw, axis=-1, keepdims=True) * (1.0 / HEAD_DIM)
        return raw * lax.rsqrt(ms + RMS_EPS) * g

    ki_ref[...] = tok(_T_KI).astype(ki_ref.dtype)
    c = _rms(tok(_T_CA), glat_ref[...]).astype(MXU_DTYPE)
    ka_ref[...] = (key_slab(_dot(c, wuk_ref[...]), gka_ref[...]) + alibi_cols).astype(ka_ref.dtype)
    va_t = _dot_nt(wuvt_ref[...], c)
    frow = lax.broadcasted_iota(jnp.int32, va_t.shape, 0)
    vat_ref[...] = jnp.where(frow == HEAD_DIM, 1.0, va_t).astype(vat_ref.dtype)
    kcv = tok(_T_KCV)
    for j in range(2 * NSA_KV_HEADS):
        kcv_ref[j] = kcv[:, j * HEAD_DIM:(j + 1) * HEAD_DIM]
    low = lane < HEAD_DIM
    for cols, g_ref, out_ref, extra in ((_T_KS, gks_ref, ks_ref, alibi_cols + block_onehot),
                                        (_T_KW, gkw_ref, kw_ref, alibi_cols)):
        pair = tok(cols)
        for kh in range(NSA_KV_HEADS):
            raw = pair if kh == 0 else pltpu.roll(pair, HEAD_DIM, 1)
            slab = key_slab(jnp.where(low, raw, 0.0), g_ref[...]) + extra
            out_ref[:, kh * LANES:(kh + 1) * LANES] = slab.astype(out_ref.dtype)

    qit_ref[...] = feat(_F_QI).astype(qit_ref.dtype)
    gw = feat(_F_GW)
    grow = lax.broadcasted_iota(jnp.int32, gw.shape, 0)
    gwt_ref[...] = jnp.where(grow < GATE_ROWS, jax.nn.sigmoid(gw), gw * w_idx_scale)
    erow = lax.broadcasted_iota(jnp.int32, (HEAD_DIM, tm), 0)
    q_scale = HEAD_DIM ** -0.5 * LOG2E
    for rows, g_ref, out_ref, slopes in ((_F_QA, gqa_ref, qat_ref, slopes_a), (_F_QB, gqb_ref, qbt_ref, slopes_b)):
        q_all = feat(rows)
        for hd in range(len(slopes)):
            q = q_all[hd * HEAD_DIM:(hd + 1) * HEAD_DIM]
            ms = jnp.mean(q * q, axis=0, keepdims=True)
            q = q * lax.rsqrt(ms + RMS_EPS) * g_ref[...] * q_scale
            out_ref[hd * LANES:hd * LANES + HEAD_DIM, :] = q.astype(out_ref.dtype)
            extra = jnp.where(erow < 3, slopes[hd], 0.0)
            out_ref[hd * LANES + HEAD_DIM:(hd + 1) * LANES, :] = extra.astype(out_ref.dtype)
    ones_row = jnp.where(erow == 0, 1.0, 0.0)
    for rows, out_ref in ((_F_VS, vst_ref), (_F_VW, vwt_ref)):
        v_t = feat(rows)
        for kh in range(NSA_KV_HEADS):
            out_ref[kh * LANES:kh * LANES + HEAD_DIM, :] = v_t[kh * HEAD_DIM:(kh + 1) * HEAD_DIM].astype(out_ref.dtype)
            out_ref[kh * LANES + HEAD_DIM:(kh + 1) * LANES, :] = ones_row.astype(out_ref.dtype)


def _proj(x2, seq_len, g_mix, w_tok, w_feat, g_lat, w_uk, w_uv_t, gk_a, gks, gkw, gq_a, gq_b, tm):
    T, D = x2.shape
    tokm = lambda w: pl.BlockSpec((tm, w), lambda i: (i, 0))
    featm = lambda r: pl.BlockSpec((r, tm), lambda i: (0, i))
    full = lambda a: pl.BlockSpec(a.shape, lambda i: (0,) * a.ndim)
    f32 = jnp.float32
    mx = MXU_DTYPE
    out_shape = (
        jax.ShapeDtypeStruct((T, LANES), mx),
        jax.ShapeDtypeStruct((T, LANES), mx),
        jax.ShapeDtypeStruct((LANES, T), mx),
        jax.ShapeDtypeStruct((2 * NSA_KV_HEADS, T, HEAD_DIM), f32),
        jax.ShapeDtypeStruct((T, NSA_KV_HEADS * LANES), mx),
        jax.ShapeDtypeStruct((T, NSA_KV_HEADS * LANES), mx),
        jax.ShapeDtypeStruct((NSA_KV_HEADS * LANES, T), mx),
        jax.ShapeDtypeStruct((NSA_KV_HEADS * LANES, T), mx),
        jax.ShapeDtypeStruct((IDX_HEADS * IDX_DIM, T), mx),
        jax.ShapeDtypeStruct((LANES, T), f32),
        jax.ShapeDtypeStruct((DSA_HEADS * LANES, T), mx),
        jax.ShapeDtypeStruct((NSA_HEADS * LANES, T), mx),
    )
    out_specs = (
        tokm(LANES), tokm(LANES), featm(LANES),
        pl.BlockSpec((2 * NSA_KV_HEADS, tm, HEAD_DIM), lambda i: (0, i, 0)),
        tokm(NSA_KV_HEADS * LANES), tokm(NSA_KV_HEADS * LANES),
        featm(NSA_KV_HEADS * LANES), featm(NSA_KV_HEADS * LANES),
        featm(IDX_HEADS * IDX_DIM), featm(LANES), featm(DSA_HEADS * LANES), featm(NSA_HEADS * LANES),
    )
    kern = functools.partial(
        _proj_kernel, seq_len=seq_len, w_idx_scale=IDX_HEADS ** -0.5 * IDX_DIM ** -0.5,
        slopes_a=_alibi_slopes(DSA_HEADS), slopes_b=_alibi_slopes(NSA_HEADS))
    ins = (x2, g_mix, w_tok, w_feat, g_lat, w_uk, w_uv_t, gk_a, gks, gkw, gq_a, gq_b)
    return pl.pallas_call(
        kern,
        grid=(T // tm,),
        in_specs=[tokm(D)] + [full(a) for a in ins[1:]],
        out_specs=out_specs,
        out_shape=out_shape,
        compiler_params=_params(("parallel",)),
        name="proj",
    )(*ins)


def _compress_kernel(xk_ref, xv_ref, pe_ref, w1_ref, w2k_ref, w2vt_ref, gkc_ref, kc_ref, vct_ref):
    n = kc_ref.shape[0]

    def hidden(x_ref, kv):
        a = jnp.zeros((n, CMP_HIDDEN), jnp.float32)
        b = jnp.zeros((n, CMP_HIDDEN), jnp.float32)
        for l in range(CMP_STRIDE):
            x_l = x_ref[pl.ds(l, n, stride=CMP_STRIDE), :]
            for acc_is_b, row in ((False, l), (True, CMP_STRIDE + l)):
                lhs = (x_l + pe_ref[row:row + 1, :]).astype(MXU_DTYPE)
                part = _dot(lhs, w1_ref[kv, row * HEAD_DIM:(row + 1) * HEAD_DIM, :])
                if acc_is_b:
                    b = b + part
                else:
                    a = a + part
        return jax.nn.gelu(a + pltpu.roll(b, n - 1, 0)).astype(MXU_DTYPE)

    yk = _dot(hidden(xk_ref, 0), w2k_ref[...])
    ms = jnp.sum(yk * yk, axis=-1, keepdims=True) * (1.0 / HEAD_DIM)
    kc_ref[...] = (yk * lax.rsqrt(ms + RMS_EPS) * gkc_ref[...]).astype(kc_ref.dtype)
    vct_ref[...] = _dot_nt(w2vt_ref[...], hidden(xv_ref, 1)).astype(vct_ref.dtype)


def _compress(xc, pe, phi1, w2k, w2v_t, g_kc):
    _, B, S, w = xc.shape
    n = S // CMP_STRIDE
    full = lambda a: pl.BlockSpec(a.shape, lambda b, kh: (0,) * a.ndim)
    return pl.pallas_call(
        _compress_kernel,
        grid=(B, NSA_KV_HEADS),
        in_specs=[
            pl.BlockSpec((None, None, S, w), lambda b, kh: (kh, b, 0, 0)),
            pl.BlockSpec((None, None, S, w), lambda b, kh: (NSA_KV_HEADS + kh, b, 0, 0)),
            full(pe), full(phi1), full(w2k), full(w2v_t), full(g_kc),
        ],
        out_specs=(pl.BlockSpec((None, None, n, LANES), lambda b, kh: (b, kh, 0, 0)),
                   pl.BlockSpec((None, None, LANES, n), lambda b, kh: (b, kh, 0, 0))),
        out_shape=(jax.ShapeDtypeStruct((B, NSA_KV_HEADS, n, LANES), MXU_DTYPE),
                   jax.ShapeDtypeStruct((B, NSA_KV_HEADS, LANES, n), MXU_DTYPE)),
        compiler_params=_params(("parallel", "parallel")),
        name="compress",
    )(xc, xc, pe, phi1, w2k, w2v_t, g_kc)


_INT_MIN = -2 ** 31


def _key_to_float(u):
    key = u ^ jnp.int32(_INT_MIN)
    bits = jnp.where(key >= 0, key, key ^ jnp.int32(0x7FFFFFFF))
    return lax.bitcast_convert_type(bits, jnp.float32)


def _col_reduce(x, op, final):
    rows = REDUCE_ROWS
    acc = x
    if x.shape[0] > rows and x.shape[0] % rows == 0:
        acc = x[:rows]
        for j in range(1, x.shape[0] // rows):
            acc = op(acc, x[j * rows:(j + 1) * rows])
    return final(acc.astype(jnp.float32), axis=0, keepdims=True)


def _col_sum(x):
    return _col_reduce(x, jnp.add, jnp.sum)


def _col_max(x):
    return _col_reduce(x, jnp.maximum, jnp.max)


def _kth_largest(score, k, side_steps=0, side_work=None):
    def body(it, u):
        u_try = u | lax.shift_left(jnp.int32(1), 31 - it)
        cnt = _col_sum(jnp.where(score >= _key_to_float(u_try), 1.0, 0.0))
        return jnp.where(cnt >= k, u_try, u)

    def body_with_side_work(j, u):
        side_work(j)
        for i in range(32 // side_steps):
            u = body(j * (32 // side_steps) + i, u)
        return u

    u = jnp.zeros((1, score.shape[1]), jnp.int32)
    if side_steps:
        u = lax.fori_loop(0, side_steps, body_with_side_work, u)
    else:
        u = lax.fori_loop(0, 32, body, u)
    thr = _key_to_float(u)
    return jnp.where(thr != thr, -jnp.inf, thr)


def _lane_tile(x, n):
    return jnp.concatenate([x] * n, axis=1)


def _softmax_pv(parts):
    m = None
    for s, _ in parts:
        m_p = _col_max(s)
        m = m_p if m is None else jnp.maximum(m, m_p)
    out = None
    for s, v_t in parts:
        o_p = _dot(v_t, jnp.exp2(s - m).astype(MXU_DTYPE))
        out = o_p if out is None else out + o_p
    return out


def _width_classes(n_qblk, seq_len):
    n_cls = N_WIDTH_CLASSES if n_qblk % N_WIDTH_CLASSES == 0 else 1
    per = n_qblk // n_cls
    return [(c * per, (c + 1) * per, (c + 1) * per * Q_BLOCK) for c in range(n_cls)]


def _dsa_body(W, qit_ref, gwt_ref, qat_ref, ki_ref, ka_ref, vat_ref, o_ref, sel_ref, qk_ref,*, topk):
    f32 = jnp.float32
    QB = Q_BLOCK
    t = pl.program_id(1) * QB + lax.broadcasted_iota(jnp.int32, (1, QB), 1)
    spos = lax.broadcasted_iota(jnp.int32, (W, QB), 0)
    causal = spos <= t

    ki = ki_ref[:W, :]
    G = 4
    kpad = jnp.zeros((LANES - IDX_DIM, G * QB), MXU_DTYPE)
    score = jnp.zeros((W, QB), f32)
    for hg in range(IDX_HEADS // G):
        q_idx = jnp.concatenate(
            [qit_ref[(hg * G + g) * IDX_DIM:(hg * G + g + 1) * IDX_DIM, :] for g in range(G)], axis=1)
        logits = _dot(ki, jnp.concatenate([q_idx, kpad], axis=0))
        for g in range(G):
            w_h = gwt_ref[GATE_ROWS + hg * G + g:GATE_ROWS + hg * G + g + 1, :]
            score = score + jnp.maximum(logits[:, g * QB:(g + 1) * QB], 0.0) * w_h
    score = jnp.where(causal, score, -jnp.inf)

    q_all = jnp.concatenate([qat_ref[h * LANES:(h + 1) * LANES, :] for h in range(DSA_HEADS)], axis=1)
    chunk = W // QK_SIDE_STEPS

    def qk_chunk(it):
        r0 = pl.multiple_of(it * chunk, chunk)
        qk_ref[pl.ds(r0, chunk), :] = _dot(ka_ref[pl.ds(r0, chunk), :], q_all)

    thr = _kth_largest(score, float(topk), QK_SIDE_STEPS, qk_chunk)
    gt = score > thr
    eq = (score == thr) & causal
    need = float(topk) - _col_sum(jnp.where(gt, 1.0, 0.0))
    n_eq = _col_sum(jnp.where(eq, 1.0, 0.0))
    sel_ref[:W, :] = jnp.where(gt | eq, 0.0, NEG_INF)

    @pl.when(jnp.max(n_eq - need) > 0.0)
    def _():
        cw = TIE_CHUNK
        r = lax.broadcasted_iota(jnp.int32, (cw, cw), 0)
        c = lax.broadcasted_iota(jnp.int32, (cw, cw), 1)
        lower = jnp.where(c <= r, 1.0, 0.0).astype(MXU_DTYPE)
        carry = jnp.zeros((1, QB), f32)
        for j in range(W // cw):
            sl = slice(j * cw, (j + 1) * cw)
            eq_j = jnp.where(eq[sl], 1.0, 0.0)
            prefix = _dot(lower, eq_j.astype(MXU_DTYPE)) + carry
            keep = gt[sl] | (eq[sl] & (prefix <= need))
            sel_ref[sl, :] = jnp.where(keep, 0.0, NEG_INF)
            carry = carry + jnp.sum(eq_j, axis=0, keepdims=True)

    G = 4
    va_t = vat_ref[:, :W]
    for hg in range(DSA_HEADS // G):
        s = qk_ref[:W, hg * G * QB:(hg + 1) * G * QB] + _lane_tile(sel_ref[:W, :], G)
        o_t = _softmax_pv([(s, va_t)])
        o_t = o_t[:HEAD_DIM] * (1.0 / o_t[HEAD_DIM:HEAD_DIM + 1])
        for g2 in range(G // 2):
            pair = jnp.concatenate([o_t[:, (2 * g2) * QB:(2 * g2 + 1) * QB],
                                    o_t[:, (2 * g2 + 1) * QB:(2 * g2 + 2) * QB]], axis=0)
            col = (hg * G + 2 * g2) * HEAD_DIM
            o_ref[:, col:col + 2 * HEAD_DIM] = pair.T.astype(o_ref.dtype)


def _dsa_kernel(qit_ref, gwt_ref, qat_ref, ki_ref, ka_ref, vat_ref, o_ref, sel_ref, qk_ref,*, topk, classes):
    qblk = pl.program_id(1)
    for lo, hi, W in classes:
        @pl.when((qblk >= lo) & (qblk < hi))
        def _(W=W):
            _dsa_body(W, qit_ref, gwt_ref, qat_ref, ki_ref, ka_ref, vat_ref, o_ref, sel_ref, qk_ref,topk=topk)


def _dsa(B, S, qi_t, gw_t, qa_t, ki, ka, va_t):
    nq = S // Q_BLOCK
    topk = min(DSA_TOPK_MAX, S // 4)
    fblk = lambda r: pl.BlockSpec((r, Q_BLOCK), lambda b, i: (0, b * nq + i))
    tseq = lambda w: pl.BlockSpec((S, w), lambda b, i: (b, 0))
    fseq = lambda r: pl.BlockSpec((r, S), lambda b, i: (0, b))
    return pl.pallas_call(
        functools.partial(_dsa_kernel, topk=topk, classes=_width_classes(nq, S)),
        grid=(B, nq),
        in_specs=[fblk(IDX_HEADS * IDX_DIM), fblk(LANES), fblk(DSA_HEADS * LANES),
                  tseq(LANES), tseq(LANES), fseq(LANES)],
        out_specs=pl.BlockSpec((Q_BLOCK, WIDTH_A), lambda b, i: (b * nq + i, 0)),
        out_shape=jax.ShapeDtypeStruct((B * S, WIDTH_A), MXU_DTYPE),
        scratch_shapes=[pltpu.VMEM((S, Q_BLOCK), jnp.float32),
                        pltpu.VMEM((S, DSA_HEADS * Q_BLOCK), jnp.float32)],
        compiler_params=_params(("parallel", "arbitrary")),
        name="dsa",
    )(qi_t, gw_t, qa_t, ki, ka, va_t)


def _nsa_body(W, first_blk, gwt_ref, qbt_ref, ks_ref, kw_ref, vst_ref, vwt_ref, kc_ref, vct_ref, ovl_ref, o_ref,
              *, seq_len):
    f32 = jnp.float32
    QB, G = Q_BLOCK, NSA_GROUP
    n_slc = seq_len // SLC_LEN
    n_top = min(SLC_TOPN, n_slc)
    win_len = WINDOW + QB
    nc = kc_ref.shape[1]

    t0 = pl.program_id(1) * QB
    t = t0 + lax.broadcasted_iota(jnp.int32, (1, QB), 1)
    t4 = _lane_tile(t, G)
    valid_c = lax.broadcasted_iota(jnp.int32, (nc, G * QB), 0) * CMP_STRIDE + (CMP_LEN - 1) <= t4
    past = first_blk * QB
    causal_tail = past + lax.broadcasted_iota(jnp.int32, (W - past, G * QB), 0) <= t4
    w0 = pl.multiple_of(jnp.maximum(t0 - WINDOW, 0), QB)
    full_window = past >= WINDOW
    if full_window:
        r_minus_lane = lax.broadcasted_iota(jnp.int32, (QB, G * QB), 0) - (t4 - t0)
        w_head_ok = r_minus_lane > 0
        w_tail_ok = r_minus_lane <= 0
    else:
        wd = t4 - (w0 + lax.broadcasted_iota(jnp.int32, (win_len, G * QB), 0))
        wvalid = (wd >= 0) & (wd < WINDOW)
    jblk = lax.broadcasted_iota(jnp.int32, (MAX_SLC_BLOCKS, QB), 0)
    cur = lax.shift_right_logical(t, 6)
    future = jblk > cur
    forced = (jblk == 0) | (jblk == cur) | (jblk == cur - 1)
    ovl = ovl_ref[...]

    KH = range(NSA_KV_HEADS)
    ksl = [slice(kh * LANES, (kh + 1) * LANES) for kh in KH]
    q = [jnp.concatenate([qbt_ref[(kh * G + g) * LANES:(kh * G + g + 1) * LANES, :] for g in range(G)], axis=1)
         for kh in KH]

    s_c = [jnp.where(valid_c, _dot(kc_ref[kh], q[kh]), NEG_INF) for kh in KH]
    s_w = [_dot(kw_ref[pl.ds(w0, win_len), ksl[kh]], q[kh]) for kh in KH]
    e_c = [jnp.where(valid_c, jnp.exp2(s_c[kh] - jnp.max(s_c[kh], axis=0, keepdims=True)), 0.0) for kh in KH]
    l_c = [jnp.sum(e_c[kh], axis=0, keepdims=True) for kh in KH]
    p_c = [e_c[kh] * (1.0 / jnp.where(l_c[kh] > 0.0, l_c[kh], 1.0)) for kh in KH]
    o_cmp = [_dot(vct_ref[kh], p_c[kh].astype(MXU_DTYPE)) for kh in KH]

    val = []
    for kh in KH:
        pc = p_c[kh]
        p_sum = pc[:, 0:QB] + pc[:, QB:2 * QB] + pc[:, 2 * QB:3 * QB] + pc[:, 3 * QB:4 * QB]
        p_hi = p_sum.astype(MXU_DTYPE)
        p_lo = (p_sum - p_hi.astype(f32)).astype(MXU_DTYPE)
        imp = _dot(ovl, p_hi) + _dot(ovl, p_lo)
        val.append(jnp.where(future, -jnp.inf, imp + jnp.where(forced, FORCE_BONUS, 0.0)))

    o_win = []
    for kh in KH:
        v_w = vwt_ref[ksl[kh], pl.ds(w0, win_len)]
        if full_window:
            parts = [(jnp.where(w_head_ok, s_w[kh][:QB], NEG_INF), v_w[:, :QB]),
                     (s_w[kh][QB:WINDOW], v_w[:, QB:WINDOW]),
                     (jnp.where(w_tail_ok, s_w[kh][WINDOW:], NEG_INF), v_w[:, WINDOW:])]
        else:
            parts = [(jnp.where(wvalid, s_w[kh], NEG_INF), v_w)]
        o_win.append(_softmax_pv(parts))

    rank = [jnp.zeros(val[kh].shape, f32) for kh in KH]
    for i in range(n_slc):
        for kh in KH:
            vi = val[kh][i:i + 1, :]
            rank[kh] = rank[kh] + jnp.where((vi > val[kh]) | ((vi == val[kh]) & (jblk > i)), 1.0, 0.0)

    s_s = []
    for kh in KH:
        bias = jnp.where((rank[kh] < n_top) & (jblk < n_slc), 0.0, -BLOCK_BIAS).astype(MXU_DTYPE)
        q_sel = jnp.concatenate([q[kh][:BIAS_COL], _lane_tile(bias, G)], axis=0)
        s_s.append(_dot(ks_ref[:W, ksl[kh]], q_sel))
    o_slc = []
    for kh in KH:
        parts = [(jnp.where(causal_tail, s_s[kh][past:], NEG_INF), vst_ref[ksl[kh], past:W])]
        if past:
            parts.append((s_s[kh][:past], vst_ref[ksl[kh], :past]))
        o_slc.append(_softmax_pv(parts))

    for kh in KH:
        inv_s = 1.0 / o_slc[kh][HEAD_DIM:HEAD_DIM + 1]
        inv_w = 1.0 / o_win[kh][HEAD_DIM:HEAD_DIM + 1]
        heads = []
        for g in range(G):
            h = kh * G + g
            ls = slice(g * QB, (g + 1) * QB)
            g0 = gwt_ref[3 * h + 0:3 * h + 1, :]
            g1 = gwt_ref[3 * h + 1:3 * h + 2, :] * inv_s[:, ls]
            g2 = gwt_ref[3 * h + 2:3 * h + 3, :] * inv_w[:, ls]
            heads.append(g0 * o_cmp[kh][:HEAD_DIM, ls] + g1 * o_slc[kh][:HEAD_DIM, ls]
                         + g2 * o_win[kh][:HEAD_DIM, ls])
        for g2_ in range(G // 2):
            pair = jnp.concatenate([heads[2 * g2_], heads[2 * g2_ + 1]], axis=0)
            col = (kh * G + 2 * g2_) * HEAD_DIM
            o_ref[:, col:col + 2 * HEAD_DIM] = pair.T.astype(o_ref.dtype)


def _nsa_kernel(gwt_ref, qbt_ref, ks_ref, kw_ref, vst_ref, vwt_ref, kc_ref, vct_ref, ovl_ref, o_ref,
                *, seq_len, classes):
    qblk = pl.program_id(1)
    for lo, hi, W in classes:
        @pl.when((qblk >= lo) & (qblk < hi))
        def _(W=W, lo=lo):
            _nsa_body(W, lo, gwt_ref, qbt_ref, ks_ref, kw_ref, vst_ref, vwt_ref, kc_ref, vct_ref, ovl_ref, o_ref,
                      seq_len=seq_len)


def _nsa(B, S, gw_t, qb_t, ks, kw, vs_t, vw_t, kc, vc_t, ovl):
    nq = S // Q_BLOCK
    nc = kc.shape[2]
    kvw = NSA_KV_HEADS * LANES
    fblk = lambda r: pl.BlockSpec((r, Q_BLOCK), lambda b, i: (0, b * nq + i))
    tseq = lambda w: pl.BlockSpec((S, w), lambda b, i: (b, 0))
    fseq = lambda r: pl.BlockSpec((r, S), lambda b, i: (0, b))
    return pl.pallas_call(
        functools.partial(_nsa_kernel, seq_len=S, classes=_width_classes(nq, S)),
        grid=(B, nq),
        in_specs=[fblk(LANES), fblk(NSA_HEADS * LANES), tseq(kvw), tseq(kvw), fseq(kvw), fseq(kvw),
                  pl.BlockSpec((None, NSA_KV_HEADS, nc, LANES), lambda b, i: (b, 0, 0, 0)),
                  pl.BlockSpec((None, NSA_KV_HEADS, LANES, nc), lambda b, i: (b, 0, 0, 0)),
                  pl.BlockSpec(ovl.shape, lambda b, i: (0, 0))],
        out_specs=pl.BlockSpec((Q_BLOCK, WIDTH_B), lambda b, i: (b * nq + i, 0)),
        out_shape=jax.ShapeDtypeStruct((B * S, WIDTH_B), MXU_DTYPE),
        compiler_params=_params(("parallel", "arbitrary")),
        name="nsa",
    )(gw_t, qb_t, ks, kw, vs_t, vw_t, kc, vc_t, ovl)


def _merge_ffn_kernel(x_ref, oa_ref, ob_ref, gmix_ref, wg_ref, wua_ref, wub_ref, wo_ref,
                      gmlp_ref, w1_ref, w2_ref, o_ref, *, chunk):
    x = x_ref[...]
    D = x.shape[1]
    h = _rms(x, gmix_ref[...]).astype(MXU_DTYPE)
    ga = jax.nn.sigmoid(_dot(h, wg_ref[:, :D]))
    gb = jax.nn.sigmoid(_dot(h, wg_ref[:, D:]))
    merged = ga * _dot(oa_ref[...], wua_ref[...]) + gb * _dot(ob_ref[...], wub_ref[...])
    x1 = x + _dot(merged.astype(MXU_DTYPE), wo_ref[...])
    h2 = _rms(x1, gmlp_ref[...]).astype(MXU_DTYPE)
    acc = x1
    for c in range(w1_ref.shape[1] // chunk):
        u = jnp.maximum(_dot(h2, w1_ref[:, c * chunk:(c + 1) * chunk]), 0.0)
        acc = acc + _dot((u * u).astype(MXU_DTYPE), w2_ref[c * chunk:(c + 1) * chunk, :])
    o_ref[...] = acc


def _merge_ffn(x2, oa, ob, g_mix, w_gate, w_up_a, w_up_b, w_out, g_mlp, w1, w2, tm):
    T, D = x2.shape
    row = lambda w: pl.BlockSpec((tm, w), lambda i: (i, 0))
    full = lambda a: pl.BlockSpec(a.shape, lambda i: (0,) * a.ndim, pipeline_mode=pl.Buffered(1))
    ins = (x2, oa, ob, g_mix, w_gate, w_up_a, w_up_b, w_out, g_mlp, w1, w2)
    return pl.pallas_call(
        functools.partial(_merge_ffn_kernel, chunk=1024),
        grid=(T // tm,),
        in_specs=[row(D), row(WIDTH_A), row(WIDTH_B)] + [full(a) for a in ins[3:]],
        out_specs=row(D),
        out_shape=jax.ShapeDtypeStruct((T, D), jnp.float32),
        compiler_params=_params(("parallel",)),
        name="merge_ffn",
    )(*ins)


def _layer(x, g_mix, w_in, g_q_a, g_k_a, g_lat_a, w_uk_a, w_uv_a, g_q_b, g_kc_b, g_ks_b, g_kw_b,
           pe_cmp_b, phi_k1_b, phi_k2_b, phi_v1_b, phi_v2_b, w_up_a, w_up_b, w_out, g_mlp, w_ff1, w_ff2):
    B, S, D = x.shape
    T = B * S
    f32 = jnp.float32
    mx = MXU_DTYPE
    n_slc = S // SLC_LEN
    assert n_slc <= MAX_SLC_BLOCKS and S % (N_WIDTH_CLASSES * TIE_CHUNK) == 0 and S >= WINDOW + Q_BLOCK
    col_sizes = (WIDTH_A, DSA_LATENT, IDX_HEADS * IDX_DIM, IDX_DIM, IDX_HEADS,
                 WIDTH_B, 6 * NSA_KV_W, 3 * NSA_HEADS, 2 * D)
    offs = np.cumsum((0,) + col_sizes)
    w_qa, w_ca, w_qi, w_ki, w_wi, w_qb, w_kvb, w_gb, w_gm = [
        w_in[:, offs[i]:offs[i + 1]] for i in range(len(col_sizes))]
    w_kc, w_vc, w_ks, w_vs, w_kw, w_vw = [w_kvb[:, j * NSA_KV_W:(j + 1) * NSA_KV_W] for j in range(6)]
    w_tok = jnp.concatenate(
        [w_ki, jnp.zeros((D, LANES - IDX_DIM), f32), w_ca, w_kc, w_vc, w_ks, w_kw], axis=1).astype(mx)
    assert w_tok.shape[1] == _T_COLS
    gw_pad = jnp.zeros((D, LANES - GATE_ROWS - IDX_HEADS), f32)
    w_feat = jnp.concatenate(
        [w_qi, w_gb, w_wi, gw_pad, w_qa, w_qb, w_vs, w_vw], axis=1).T.astype(mx)
    assert w_feat.shape[0] == _F_ROWS
    w_gate = w_gm.astype(mx)
    zpad = jnp.zeros((DSA_LATENT, LANES - HEAD_DIM), f32)
    w_uk = jnp.concatenate([w_uk_a, zpad], axis=1).astype(mx)
    w_uv_t = jnp.concatenate([w_uv_a, zpad], axis=1).T.astype(mx)
    row = lambda g: g.reshape(1, -1).astype(f32)
    rpad = lambda g: jnp.concatenate([row(g), jnp.zeros((1, LANES - HEAD_DIM), f32)], axis=1)
    colv = lambda g: g.reshape(-1, 1).astype(f32)

    x2 = x.reshape(T, D)
    (ki, ka, va_t, kcv, ks, kw, vs_t, vw_t, qi_t, gw_t, qa_t, qb_t) = _proj(
        x2, S, row(g_mix), w_tok, w_feat, row(g_lat_a), w_uk, w_uv_t, rpad(g_k_a), rpad(g_ks_b), rpad(g_kw_b),
        colv(g_q_a), colv(g_q_b), tm=256)

    n_chunk = S // CMP_STRIDE
    n_cmp = (S - CMP_LEN) // CMP_STRIDE + 1
    xc = kcv.reshape(2 * NSA_KV_HEADS, B, S, HEAD_DIM)
    phi1 = jnp.stack([phi_k1_b, phi_v1_b]).astype(mx)
    hpad = jnp.zeros((CMP_HIDDEN, LANES - HEAD_DIM), f32)
    w2k = jnp.concatenate([phi_k2_b, hpad], axis=1).astype(mx)
    w2v_t = jnp.concatenate([phi_v2_b, hpad], axis=1).T.astype(mx)
    kc, vc_t = _compress(xc, pe_cmp_b.astype(f32), phi1, w2k, w2v_t, rpad(g_kc_b))

    c_i = np.arange(n_chunk)[None, :] * CMP_STRIDE
    j_i = np.arange(MAX_SLC_BLOCKS)[:, None] * SLC_LEN
    ovl = (c_i < j_i + SLC_LEN) & (c_i + CMP_LEN > j_i) & (c_i < n_cmp * CMP_STRIDE) & (j_i < S)
    ovl = jnp.asarray(ovl, f32).astype(mx)

    o_a = _dsa(B, S, qi_t, gw_t, qa_t, ki, ka, va_t)
    o_b = _nsa(B, S, gw_t, qb_t, ks, kw, vs_t, vw_t, kc, vc_t, ovl)

    out = _merge_ffn(x2, o_a, o_b, row(g_mix), w_gate, w_up_a.astype(mx), w_up_b.astype(mx), w_out.astype(mx),
                     row(g_mlp), w_ff1.astype(mx), w_ff2.astype(mx), tm=256)
    return out.reshape(B, S, D)


def kernel(x, g_mix, w_in, g_q_a, g_k_a, g_lat_a, w_uk_a, w_uv_a, g_q_b, g_kc_b, g_ks_b, g_kw_b, pe_cmp_b,
           phi_k1_b, phi_k2_b, phi_v1_b, phi_v2_b, w_up_a, w_up_b, w_out, g_mlp, w_ff1, w_ff2):
    params = (g_mix, w_in, g_q_a, g_k_a, g_lat_a, w_uk_a, w_uv_a, g_q_b, g_kc_b, g_ks_b, g_kw_b, pe_cmp_b,
              phi_k1_b, phi_k2_b, phi_v1_b, phi_v2_b, w_up_a, w_up_b, w_out, g_mlp, w_ff1, w_ff2)
    for l in range(g_mix.shape[0]):
        x = _layer(x, *[p[l] for p in params])
    return x
```

```python
import functools
import math

import numpy as np
import jax
import jax.numpy as jnp
from jax import lax
from jax.experimental import pallas as pl
from jax.experimental.pallas import tpu as pltpu

HEAD_DIM = 64
DSA_HEADS = 8
DSA_LATENT = 128
IDX_HEADS = 8
IDX_DIM = 32
DSA_TOPK_MAX = 256
NSA_HEADS = 8
NSA_KV_HEADS = 2
NSA_GROUP = NSA_HEADS // NSA_KV_HEADS
CMP_LEN = 32
CMP_STRIDE = 16
CMP_HIDDEN = 128
SLC_LEN = 64
SLC_TOPN = 16
WINDOW = 512
FORCE_BONUS = 1e6
Q_BLOCK = 128
RMS_EPS = 1e-6
NEG_INF = -1e30

WIDTH_A = DSA_HEADS * HEAD_DIM
WIDTH_B = NSA_HEADS * HEAD_DIM
NSA_KV_W = NSA_KV_HEADS * HEAD_DIM

LANES = 128
GATE_ROWS = 3 * NSA_HEADS
ALIBI_COL = HEAD_DIM
BIAS_COL = 96
MAX_SLC_BLOCKS = LANES - BIAS_COL
BLOCK_BIAS = 2.0 ** 100
LOG2E = math.log2(math.e)
MXU_DTYPE = jnp.bfloat16
VMEM_LIMIT = 52 * 1024 * 1024
N_WIDTH_CLASSES = 8
TIE_CHUNK = 256
REDUCE_ROWS = 64
QK_SIDE_STEPS = 8

_NT = (((1,), (1,)), ((), ()))


def _alibi_slopes(n_heads):
    return [float(v) for v in np.asarray(
        2.0 ** (-8.0 * np.arange(1, n_heads + 1) / n_heads), dtype=np.float32)]


def _dot(a, b):
    return jnp.dot(a, b, preferred_element_type=jnp.float32)


def _dot_nt(a, b):
    return lax.dot_general(a, b, _NT, preferred_element_type=jnp.float32)


def _rms(x, g):
    ms = jnp.mean(x * x, axis=-1, keepdims=True)
    return x * lax.rsqrt(ms + RMS_EPS) * g


def _params(sem):
    return pltpu.CompilerParams(dimension_semantics=sem, vmem_limit_bytes=VMEM_LIMIT)


def _split3(c):
    c1 = c.astype(MXU_DTYPE).astype(jnp.float32)
    c2 = (c - c1).astype(MXU_DTYPE).astype(jnp.float32)
    c3 = (c - c1 - c2).astype(MXU_DTYPE).astype(jnp.float32)
    return c1, c2, c3


_T_KI = (0, 128)
_T_CA = (128, 256)
_T_KCV = (256, 512)
_T_KS = (512, 640)
_T_KW = (640, 768)
_T_COLS = 768
_F_QI = (0, 256)
_F_GW = (256, 384)
_F_QA = (384, 896)
_F_QB = (896, 1408)
_F_VS = (1408, 1536)
_F_VW = (1536, 1664)
_F_ROWS = 1664


def _proj_kernel(x_ref, gmix_ref, wtok_ref, wfeat_ref, glat_ref, wuk_ref, wuvt_ref, gka_ref, gks_ref, gkw_ref,
                 gqa_ref, gqb_ref,
                 ki_ref, ka_ref, vat_ref, kcv_ref, ks_ref, kw_ref, vst_ref, vwt_ref, qit_ref, gwt_ref,
                 qat_ref, qbt_ref, *, seq_len, w_idx_scale, slopes_a, slopes_b):
    f32 = jnp.float32
    x = x_ref[...]
    tm = x.shape[0]
    h = _rms(x, gmix_ref[...]).astype(MXU_DTYPE)

    def tok(c):
        return _dot(h, wtok_ref[:, c[0]:c[1]])

    def feat(r):
        return _dot_nt(wfeat_ref[r[0]:r[1], :], h)

    s0 = (pl.program_id(0) % (seq_len // tm)) * tm
    spos = s0 + lax.broadcasted_iota(jnp.int32, (tm, LANES), 0)
    lane = lax.broadcasted_iota(jnp.int32, (tm, LANES), 1)
    c1, c2, c3 = _split3(spos.astype(f32) * LOG2E)
    alibi_cols = jnp.where(lane == ALIBI_COL, c1,
                           jnp.where(lane == ALIBI_COL + 1, c2, jnp.where(lane == ALIBI_COL + 2, c3, 0.0)))
    block_onehot = jnp.where(lane - BIAS_COL == lax.shift_right_logical(spos, 6), 1.0, 0.0)

    def key_slab(raw, g):
        ms = jnp.sum(raw * raw, axis=-1, keepdims=True) * (1.0 / HEAD_DIM)
        return raw * lax.rsqrt(ms + RMS_EPS) * g

    ki_ref[...] = tok(_T_KI).astype(ki_ref.dtype)
    c = _rms(tok(_T_CA), glat_ref[...]).astype(MXU_DTYPE)
    ka_ref[...] = (key_slab(_dot(c, wuk_ref[...]), gka_ref[...]) + alibi_cols).astype(ka_ref.dtype)
    va_t = _dot_nt(wuvt_ref[...], c)
    frow = lax.broadcasted_iota(jnp.int32, va_t.shape, 0)
    vat_ref[...] = jnp.where(frow == HEAD_DIM, 1.0, va_t).astype(vat_ref.dtype)
    kcv = tok(_T_KCV)
    for j in range(2 * NSA_KV_HEADS):
        kcv_ref[j] = kcv[:, j * HEAD_DIM:(j + 1) * HEAD_DIM]
    low = lane < HEAD_DIM
    for cols, g_ref, out_ref, extra in ((_T_KS, gks_ref, ks_ref, alibi_cols + block_onehot),
                                        (_T_KW, gkw_ref, kw_ref, alibi_cols)):
        pair = tok(cols)
        for kh in range(NSA_KV_HEADS):
            raw = pair if kh == 0 else pltpu.roll(pair, HEAD_DIM, 1)
            slab = key_slab(jnp.where(low, raw, 0.0), g_ref[...]) + extra
            out_ref[:, kh * LANES:(kh + 1) * LANES] = slab.astype(out_ref.dtype)

    qit_ref[...] = feat(_F_QI).astype(qit_ref.dtype)
    gw = feat(_F_GW)
    grow = lax.broadcasted_iota(jnp.int32, gw.shape, 0)
    gwt_ref[...] = jnp.where(grow < GATE_ROWS, jax.nn.sigmoid(gw), gw * w_idx_scale)
    erow = lax.broadcasted_iota(jnp.int32, (HEAD_DIM, tm), 0)
    q_scale = HEAD_DIM ** -0.5 * LOG2E
    for rows, g_ref, out_ref, slopes in ((_F_QA, gqa_ref, qat_ref, slopes_a), (_F_QB, gqb_ref, qbt_ref, slopes_b)):
        q_all = feat(rows)
        for hd in range(len(slopes)):
            q = q_all[hd * HEAD_DIM:(hd + 1) * HEAD_DIM]
            ms = jnp.mean(q * q, axis=0, keepdims=True)
            q = q * lax.rsqrt(ms + RMS_EPS) * g_ref[...] * q_scale
            out_ref[hd * LANES:hd * LANES + HEAD_DIM, :] = q.astype(out_ref.dtype)
            extra = jnp.where(erow < 3, slopes[hd], 0.0)
            out_ref[hd * LANES + HEAD_DIM:(hd + 1) * LANES, :] = extra.astype(out_ref.dtype)
    ones_row = jnp.where(erow == 0, 1.0, 0.0)
    for rows, out_ref in ((_F_VS, vst_ref), (_F_VW, vwt_ref)):
        v_t = feat(rows)
        for kh in range(NSA_KV_HEADS):
            out_ref[kh * LANES:kh * LANES + HEAD_DIM, :] = v_t[kh * HEAD_DIM:(kh + 1) * HEAD_DIM].astype(out_ref.dtype)
            out_ref[kh * LANES + HEAD_DIM:(kh + 1) * LANES, :] = ones_row.astype(out_ref.dtype)


def _proj(x2, seq_len, g_mix, w_tok, w_feat, g_lat, w_uk, w_uv_t, gk_a, gks, gkw, gq_a, gq_b, tm):
    T, D = x2.shape
    tokm = lambda w: pl.BlockSpec((tm, w), lambda i: (i, 0))
    featm = lambda r: pl.BlockSpec((r, tm), lambda i: (0, i))
    full = lambda a: pl.BlockSpec(a.shape, lambda i: (0,) * a.ndim)
    f32 = jnp.float32
    mx = MXU_DTYPE
    out_shape = (
        jax.ShapeDtypeStruct((T, LANES), mx),
        jax.ShapeDtypeStruct((T, LANES), mx),
        jax.ShapeDtypeStruct((LANES, T), mx),
        jax.ShapeDtypeStruct((2 * NSA_KV_HEADS, T, HEAD_DIM), f32),
        jax.ShapeDtypeStruct((T, NSA_KV_HEADS * LANES), mx),
        jax.ShapeDtypeStruct((T, NSA_KV_HEADS * LANES), mx),
        jax.ShapeDtypeStruct((NSA_KV_HEADS * LANES, T), mx),
        jax.ShapeDtypeStruct((NSA_KV_HEADS * LANES, T), mx),
        jax.ShapeDtypeStruct((IDX_HEADS * IDX_DIM, T), mx),
        jax.ShapeDtypeStruct((LANES, T), f32),
        jax.ShapeDtypeStruct((DSA_HEADS * LANES, T), mx),
        jax.ShapeDtypeStruct((NSA_HEADS * LANES, T), mx),
    )
    out_specs = (
        tokm(LANES), tokm(LANES), featm(LANES),
        pl.BlockSpec((2 * NSA_KV_HEADS, tm, HEAD_DIM), lambda i: (0, i, 0)),
        tokm(NSA_KV_HEADS * LANES), tokm(NSA_KV_HEADS * LANES),
        featm(NSA_KV_HEADS * LANES), featm(NSA_KV_HEADS * LANES),
        featm(IDX_HEADS * IDX_DIM), featm(LANES), featm(DSA_HEADS * LANES), featm(NSA_HEADS * LANES),
    )
    kern = functools.partial(
        _proj_kernel, seq_len=seq_len, w_idx_scale=IDX_HEADS ** -0.5 * IDX_DIM ** -0.5,
        slopes_a=_alibi_slopes(DSA_HEADS), slopes_b=_alibi_slopes(NSA_HEADS))
    ins = (x2, g_mix, w_tok, w_feat, g_lat, w_uk, w_uv_t, gk_a, gks, gkw, gq_a, gq_b)
    return pl.pallas_call(
        kern,
        grid=(T // tm,),
        in_specs=[tokm(D)] + [full(a) for a in ins[1:]],
        out_specs=out_specs,
        out_shape=out_shape,
        compiler_params=_params(("parallel",)),
        name="proj",
    )(*ins)


def _compress_kernel(xk_ref, xv_ref, pe_ref, w1_ref, w2k_ref, w2vt_ref, gkc_ref, kc_ref, vct_ref):
    n = kc_ref.shape[0]

    def hidden(x_ref, kv):
        a = jnp.zeros((n, CMP_HIDDEN), jnp.float32)
        b = jnp.zeros((n, CMP_HIDDEN), jnp.float32)
        for l in range(CMP_STRIDE):
            x_l = x_ref[pl.ds(l, n, stride=CMP_STRIDE), :]
            for acc_is_b, row in ((False, l), (True, CMP_STRIDE + l)):
                lhs = (x_l + pe_ref[row:row + 1, :]).astype(MXU_DTYPE)
                part = _dot(lhs, w1_ref[kv, row * HEAD_DIM:(row + 1) * HEAD_DIM, :])
                if acc_is_b:
                    b = b + part
                else:
                    a = a + part
        return jax.nn.gelu(a + pltpu.roll(b, n - 1, 0)).astype(MXU_DTYPE)

    yk = _dot(hidden(xk_ref, 0), w2k_ref[...])
    ms = jnp.sum(yk * yk, axis=-1, keepdims=True) * (1.0 / HEAD_DIM)
    kc_ref[...] = (yk * lax.rsqrt(ms + RMS_EPS) * gkc_ref[...]).astype(kc_ref.dtype)
    vct_ref[...] = _dot_nt(w2vt_ref[...], hidden(xv_ref, 1)).astype(vct_ref.dtype)


def _compress(xc, pe, phi1, w2k, w2v_t, g_kc):
    _, B, S, w = xc.shape
    n = S // CMP_STRIDE
    full = lambda a: pl.BlockSpec(a.shape, lambda b, kh: (0,) * a.ndim)
    return pl.pallas_call(
        _compress_kernel,
        grid=(B, NSA_KV_HEADS),
        in_specs=[
            pl.BlockSpec((None, None, S, w), lambda b, kh: (kh, b, 0, 0)),
            pl.BlockSpec((None, None, S, w), lambda b, kh: (NSA_KV_HEADS + kh, b, 0, 0)),
            full(pe), full(phi1), full(w2k), full(w2v_t), full(g_kc),
        ],
        out_specs=(pl.BlockSpec((None, None, n, LANES), lambda b, kh: (b, kh, 0, 0)),
                   pl.BlockSpec((None, None, LANES, n), lambda b, kh: (b, kh, 0, 0))),
        out_shape=(jax.ShapeDtypeStruct((B, NSA_KV_HEADS, n, LANES), MXU_DTYPE),
                   jax.ShapeDtypeStruct((B, NSA_KV_HEADS, LANES, n), MXU_DTYPE)),
        compiler_params=_params(("parallel", "parallel")),
        name="compress",
    )(xc, xc, pe, phi1, w2k, w2v_t, g_kc)


_INT_MIN = -2 ** 31


def _key_to_float(u):
    key = u ^ jnp.int32(_INT_MIN)
    bits = jnp.where(key >= 0, key, key ^ jnp.int32(0x7FFFFFFF))
    return lax.bitcast_convert_type(bits, jnp.float32)


def _col_reduce(x, op, final):
    rows = REDUCE_ROWS
    acc = x
    if x.shape[0] > rows and x.shape[0] % rows == 0:
        acc = x[:rows]
        for j in range(1, x.shape[0] // rows):
            acc = op(acc, x[j * rows:(j + 1) * rows])
    return final(acc.astype(jnp.float32), axis=0, keepdims=True)


def _col_sum(x):
    return _col_reduce(x, jnp.add, jnp.sum)


def _col_max(x):
    return _col_reduce(x, jnp.maximum, jnp.max)


def _kth_largest(score, k, side_steps=0, side_work=None):
    def body(it, u):
        u_try = u | lax.shift_left(jnp.int32(1), 31 - it)
        cnt = _col_sum(jnp.where(score >= _key_to_float(u_try), 1.0, 0.0))
        return jnp.where(cnt >= k, u_try, u)

    def body_with_side_work(j, u):
        side_work(j)
        for i in range(32 // side_steps):
            u = body(j * (32 // side_steps) + i, u)
        return u

    u = jnp.zeros((1, score.shape[1]), jnp.int32)
    if side_steps:
        u = lax.fori_loop(0, side_steps, body_with_side_work, u)
    else:
        u = lax.fori_loop(0, 32, body, u)
    thr = _key_to_float(u)
    return jnp.where(thr != thr, -jnp.inf, thr)


def _lane_tile(x, n):
    return jnp.concatenate([x] * n, axis=1)


def _softmax_pv(parts):
    m = None
    for s, _ in parts:
        m_p = _col_max(s)
        m = m_p if m is None else jnp.maximum(m, m_p)
    out = None
    for s, v_t in parts:
        o_p = _dot(v_t, jnp.exp2(s - m).astype(MXU_DTYPE))
        out = o_p if out is None else out + o_p
    return out


def _width_classes(n_qblk, seq_len):
    n_cls = N_WIDTH_CLASSES if n_qblk % N_WIDTH_CLASSES == 0 else 1
    per = n_qblk // n_cls
    return [(c * per, (c + 1) * per, (c + 1) * per * Q_BLOCK) for c in range(n_cls)]


def _dsa_body(W, qit_ref, gwt_ref, qat_ref, ki_ref, ka_ref, vat_ref, o_ref, sel_ref, qk_ref,*, topk):
    f32 = jnp.float32
    QB = Q_BLOCK
    t = pl.program_id(1) * QB + lax.broadcasted_iota(jnp.int32, (1, QB), 1)
    spos = lax.broadcasted_iota(jnp.int32, (W, QB), 0)
    causal = spos <= t

    ki = ki_ref[:W, :]
    G = 4
    kpad = jnp.zeros((LANES - IDX_DIM, G * QB), MXU_DTYPE)
    score = jnp.zeros((W, QB), f32)
    for hg in range(IDX_HEADS // G):
        q_idx = jnp.concatenate(
            [qit_ref[(hg * G + g) * IDX_DIM:(hg * G + g + 1) * IDX_DIM, :] for g in range(G)], axis=1)
        logits = _dot(ki, jnp.concatenate([q_idx, kpad], axis=0))
        for g in range(G):
            w_h = gwt_ref[GATE_ROWS + hg * G + g:GATE_ROWS + hg * G + g + 1, :]
            score = score + jnp.maximum(logits[:, g * QB:(g + 1) * QB], 0.0) * w_h
    score = jnp.where(causal, score, -jnp.inf)

    q_all = jnp.concatenate([qat_ref[h * LANES:(h + 1) * LANES, :] for h in range(DSA_HEADS)], axis=1)
    chunk = W // QK_SIDE_STEPS

    def qk_chunk(it):
        r0 = pl.multiple_of(it * chunk, chunk)
        qk_ref[pl.ds(r0, chunk), :] = _dot(ka_ref[pl.ds(r0, chunk), :], q_all)

    thr = _kth_largest(score, float(topk), QK_SIDE_STEPS, qk_chunk)
    thr = jnp.maximum(thr, float(jnp.finfo(jnp.float32).min))
    ge = score >= thr
    sel_ref[:W, :] = jnp.where(ge, 0.0, NEG_INF)

    @pl.when(jnp.max(_col_sum(jnp.where(ge, 1.0, 0.0))) > float(topk))
    def _():
        gt = score > thr
        eq = score == thr
        need = float(topk) - _col_sum(jnp.where(gt, 1.0, 0.0))
        cw = TIE_CHUNK
        r = lax.broadcasted_iota(jnp.int32, (cw, cw), 0)
        c = lax.broadcasted_iota(jnp.int32, (cw, cw), 1)
        lower = jnp.where(c <= r, 1.0, 0.0).astype(MXU_DTYPE)
        carry = jnp.zeros((1, QB), f32)
        for j in range(W // cw):
            sl = slice(j * cw, (j + 1) * cw)
            eq_j = jnp.where(eq[sl], 1.0, 0.0)
            prefix = _dot(lower, eq_j.astype(MXU_DTYPE)) + carry
            keep = gt[sl] | (eq[sl] & (prefix <= need))
            sel_ref[sl, :] = jnp.where(keep, 0.0, NEG_INF)
            carry = carry + jnp.sum(eq_j, axis=0, keepdims=True)

    G = 4
    va_t = vat_ref[:, :W]
    for hg in range(DSA_HEADS // G):
        s = qk_ref[:W, hg * G * QB:(hg + 1) * G * QB] + _lane_tile(sel_ref[:W, :], G)
        o_t = _softmax_pv([(s, va_t)])
        o_t = o_t[:HEAD_DIM] * (1.0 / o_t[HEAD_DIM:HEAD_DIM + 1])
        for g2 in range(G // 2):
            pair = jnp.concatenate([o_t[:, (2 * g2) * QB:(2 * g2 + 1) * QB],
                                    o_t[:, (2 * g2 + 1) * QB:(2 * g2 + 2) * QB]], axis=0)
            col = (hg * G + 2 * g2) * HEAD_DIM
            o_ref[:, col:col + 2 * HEAD_DIM] = pair.T.astype(o_ref.dtype)


def _dsa_kernel(qit_ref, gwt_ref, qat_ref, ki_ref, ka_ref, vat_ref, o_ref, sel_ref, qk_ref,*, topk, classes):
    qblk = pl.program_id(1)
    for lo, hi, W in classes:
        @pl.when((qblk >= lo) & (qblk < hi))
        def _(W=W):
            _dsa_body(W, qit_ref, gwt_ref, qat_ref, ki_ref, ka_ref, vat_ref, o_ref, sel_ref, qk_ref,topk=topk)


def _dsa(B, S, qi_t, gw_t, qa_t, ki, ka, va_t):
    nq = S // Q_BLOCK
    topk = min(DSA_TOPK_MAX, S // 4)
    fblk = lambda r: pl.BlockSpec((r, Q_BLOCK), lambda b, i: (0, b * nq + i))
    tseq = lambda w: pl.BlockSpec((S, w), lambda b, i: (b, 0))
    fseq = lambda r: pl.BlockSpec((r, S), lambda b, i: (0, b))
    return pl.pallas_call(
        functools.partial(_dsa_kernel, topk=topk, classes=_width_classes(nq, S)),
        grid=(B, nq),
        in_specs=[fblk(IDX_HEADS * IDX_DIM), fblk(LANES), fblk(DSA_HEADS * LANES),
                  tseq(LANES), tseq(LANES), fseq(LANES)],
        out_specs=pl.BlockSpec((Q_BLOCK, WIDTH_A), lambda b, i: (b * nq + i, 0)),
        out_shape=jax.ShapeDtypeStruct((B * S, WIDTH_A), MXU_DTYPE),
        scratch_shapes=[pltpu.VMEM((S, Q_BLOCK), jnp.float32),
                        pltpu.VMEM((S, DSA_HEADS * Q_BLOCK), jnp.float32)],
        compiler_params=_params(("parallel", "arbitrary")),
        name="dsa",
    )(qi_t, gw_t, qa_t, ki, ka, va_t)


def _nsa_body(W, first_blk, gwt_ref, qbt_ref, ks_ref, kw_ref, vst_ref, vwt_ref, kc_ref, vct_ref, ovl_ref, o_ref,
              *, seq_len):
    f32 = jnp.float32
    QB, G = Q_BLOCK, NSA_GROUP
    n_slc = seq_len // SLC_LEN
    n_top = min(SLC_TOPN, n_slc)
    win_len = WINDOW + QB
    nc = kc_ref.shape[1]

    t0 = pl.program_id(1) * QB
    t = t0 + lax.broadcasted_iota(jnp.int32, (1, QB), 1)
    t4 = _lane_tile(t, G)
    valid_c = lax.broadcasted_iota(jnp.int32, (nc, G * QB), 0) * CMP_STRIDE + (CMP_LEN - 1) <= t4
    past = first_blk * QB
    causal_tail = past + lax.broadcasted_iota(jnp.int32, (W - past, G * QB), 0) <= t4
    w0 = pl.multiple_of(jnp.maximum(t0 - WINDOW, 0), QB)
    full_window = past >= WINDOW
    if full_window:
        r_minus_lane = lax.broadcasted_iota(jnp.int32, (QB, G * QB), 0) - (t4 - t0)
        w_head_ok = r_minus_lane > 0
        w_tail_ok = r_minus_lane <= 0
    else:
        wd = t4 - (w0 + lax.broadcasted_iota(jnp.int32, (win_len, G * QB), 0))
        wvalid = (wd >= 0) & (wd < WINDOW)
    jblk = lax.broadcasted_iota(jnp.int32, (MAX_SLC_BLOCKS, QB), 0)
    cur = lax.shift_right_logical(t, 6)
    future = jblk > cur
    forced = (jblk == 0) | (jblk == cur) | (jblk == cur - 1)
    ovl = ovl_ref[...]

    KH = range(NSA_KV_HEADS)
    ksl = [slice(kh * LANES, (kh + 1) * LANES) for kh in KH]
    q = [jnp.concatenate([qbt_ref[(kh * G + g) * LANES:(kh * G + g + 1) * LANES, :] for g in range(G)], axis=1)
         for kh in KH]

    s_c = [jnp.where(valid_c, _dot(kc_ref[kh], q[kh]), NEG_INF) for kh in KH]
    s_w = [_dot(kw_ref[pl.ds(w0, win_len), ksl[kh]], q[kh]) for kh in KH]
    e_c = [jnp.where(valid_c, jnp.exp2(s_c[kh] - jnp.max(s_c[kh], axis=0, keepdims=True)), 0.0) for kh in KH]
    l_c = [jnp.sum(e_c[kh], axis=0, keepdims=True) for kh in KH]
    p_c = [e_c[kh] * (1.0 / jnp.where(l_c[kh] > 0.0, l_c[kh], 1.0)) for kh in KH]
    o_cmp = [_dot(vct_ref[kh], p_c[kh].astype(MXU_DTYPE)) for kh in KH]

    val = []
    for kh in KH:
        pc = p_c[kh]
        p_sum = pc[:, 0:QB] + pc[:, QB:2 * QB] + pc[:, 2 * QB:3 * QB] + pc[:, 3 * QB:4 * QB]
        p_hi = p_sum.astype(MXU_DTYPE)
        p_lo = (p_sum - p_hi.astype(f32)).astype(MXU_DTYPE)
        imp = _dot(ovl, p_hi) + _dot(ovl, p_lo)
        val.append(jnp.where(future, -jnp.inf, imp + jnp.where(forced, FORCE_BONUS, 0.0)))

    o_win = []
    for kh in KH:
        v_w = vwt_ref[ksl[kh], pl.ds(w0, win_len)]
        if full_window:
            parts = [(jnp.where(w_head_ok, s_w[kh][:QB], NEG_INF), v_w[:, :QB]),
                     (s_w[kh][QB:WINDOW], v_w[:, QB:WINDOW]),
                     (jnp.where(w_tail_ok, s_w[kh][WINDOW:], NEG_INF), v_w[:, WINDOW:])]
        else:
            parts = [(jnp.where(wvalid, s_w[kh], NEG_INF), v_w)]
        o_win.append(_softmax_pv(parts))

    rank = [jnp.zeros(val[kh].shape, f32) for kh in KH]
    for i in range(n_slc):
        for kh in KH:
            vi = val[kh][i:i + 1, :]
            rank[kh] = rank[kh] + jnp.where((vi > val[kh]) | ((vi == val[kh]) & (jblk > i)), 1.0, 0.0)

    s_s = []
    for kh in KH:
        bias = jnp.where((rank[kh] < n_top) & (jblk < n_slc), 0.0, -BLOCK_BIAS).astype(MXU_DTYPE)
        q_sel = jnp.concatenate([q[kh][:BIAS_COL], _lane_tile(bias, G)], axis=0)
        s_s.append(_dot(ks_ref[:W, ksl[kh]], q_sel))
    o_slc = []
    for kh in KH:
        parts = [(jnp.where(causal_tail, s_s[kh][past:], NEG_INF), vst_ref[ksl[kh], past:W])]
        if past:
            parts.append((s_s[kh][:past], vst_ref[ksl[kh], :past]))
        o_slc.append(_softmax_pv(parts))

    for kh in KH:
        inv_s = 1.0 / o_slc[kh][HEAD_DIM:HEAD_DIM + 1]
        inv_w = 1.0 / o_win[kh][HEAD_DIM:HEAD_DIM + 1]
        heads = []
        for g in range(G):
            h = kh * G + g
            ls = slice(g * QB, (g + 1) * QB)
            g0 = gwt_ref[3 * h + 0:3 * h + 1, :]
            g1 = gwt_ref[3 * h + 1:3 * h + 2, :] * inv_s[:, ls]
            g2 = gwt_ref[3 * h + 2:3 * h + 3, :] * inv_w[:, ls]
            heads.append(g0 * o_cmp[kh][:HEAD_DIM, ls] + g1 * o_slc[kh][:HEAD_DIM, ls]
                         + g2 * o_win[kh][:HEAD_DIM, ls])
        for g2_ in range(G // 2):
            pair = jnp.concatenate([heads[2 * g2_], heads[2 * g2_ + 1]], axis=0)
            col = (kh * G + 2 * g2_) * HEAD_DIM
            o_ref[:, col:col + 2 * HEAD_DIM] = pair.T.astype(o_ref.dtype)


def _nsa_kernel(gwt_ref, qbt_ref, ks_ref, kw_ref, vst_ref, vwt_ref, kc_ref, vct_ref, ovl_ref, o_ref,
                *, seq_len, classes):
    qblk = pl.program_id(1)
    for lo, hi, W in classes:
        @pl.when((qblk >= lo) & (qblk < hi))
        def _(W=W, lo=lo):
            _nsa_body(W, lo, gwt_ref, qbt_ref, ks_ref, kw_ref, vst_ref, vwt_ref, kc_ref, vct_ref, ovl_ref, o_ref,
                      seq_len=seq_len)


def _nsa(B, S, gw_t, qb_t, ks, kw, vs_t, vw_t, kc, vc_t, ovl):
    nq = S // Q_BLOCK
    nc = kc.shape[2]
    kvw = NSA_KV_HEADS * LANES
    fblk = lambda r: pl.BlockSpec((r, Q_BLOCK), lambda b, i: (0, b * nq + i))
    tseq = lambda w: pl.BlockSpec((S, w), lambda b, i: (b, 0))
    fseq = lambda r: pl.BlockSpec((r, S), lambda b, i: (0, b))
    return pl.pallas_call(
        functools.partial(_nsa_kernel, seq_len=S, classes=_width_classes(nq, S)),
        grid=(B, nq),
        in_specs=[fblk(LANES), fblk(NSA_HEADS * LANES), tseq(kvw), tseq(kvw), fseq(kvw), fseq(kvw),
                  pl.BlockSpec((None, NSA_KV_HEADS, nc, LANES), lambda b, i: (b, 0, 0, 0)),
                  pl.BlockSpec((None, NSA_KV_HEADS, LANES, nc), lambda b, i: (b, 0, 0, 0)),
                  pl.BlockSpec(ovl.shape, lambda b, i: (0, 0))],
        out_specs=pl.BlockSpec((Q_BLOCK, WIDTH_B), lambda b, i: (b * nq + i, 0)),
        out_shape=jax.ShapeDtypeStruct((B * S, WIDTH_B), MXU_DTYPE),
        compiler_params=_params(("parallel", "arbitrary")),
        name="nsa",
    )(gw_t, qb_t, ks, kw, vs_t, vw_t, kc, vc_t, ovl)


def _merge_ffn_kernel(x_ref, oa_ref, ob_ref, gmix_ref, wg_ref, wua_ref, wub_ref, wo_ref,
                      gmlp_ref, w1_ref, w2_ref, o_ref, *, chunk):
    x = x_ref[...]
    D = x.shape[1]
    h = _rms(x, gmix_ref[...]).astype(MXU_DTYPE)
    ga = jax.nn.sigmoid(_dot(h, wg_ref[:, :D]))
    gb = jax.nn.sigmoid(_dot(h, wg_ref[:, D:]))
    merged = ga * _dot(oa_ref[...], wua_ref[...]) + gb * _dot(ob_ref[...], wub_ref[...])
    x1 = x + _dot(merged.astype(MXU_DTYPE), wo_ref[...])
    h2 = _rms(x1, gmlp_ref[...]).astype(MXU_DTYPE)
    acc = x1
    for c in range(w1_ref.shape[1] // chunk):
        u = jnp.maximum(_dot(h2, w1_ref[:, c * chunk:(c + 1) * chunk]), 0.0)
        acc = acc + _dot((u * u).astype(MXU_DTYPE), w2_ref[c * chunk:(c + 1) * chunk, :])
    o_ref[...] = acc


def _merge_ffn(x2, oa, ob, g_mix, w_gate, w_up_a, w_up_b, w_out, g_mlp, w1, w2, tm):
    T, D = x2.shape
    row = lambda w: pl.BlockSpec((tm, w), lambda i: (i, 0))
    full = lambda a: pl.BlockSpec(a.shape, lambda i: (0,) * a.ndim, pipeline_mode=pl.Buffered(1))
    ins = (x2, oa, ob, g_mix, w_gate, w_up_a, w_up_b, w_out, g_mlp, w1, w2)
    return pl.pallas_call(
        functools.partial(_merge_ffn_kernel, chunk=1024),
        grid=(T // tm,),
        in_specs=[row(D), row(WIDTH_A), row(WIDTH_B)] + [full(a) for a in ins[3:]],
        out_specs=row(D),
        out_shape=jax.ShapeDtypeStruct((T, D), jnp.float32),
        compiler_params=_params(("parallel",)),
        name="merge_ffn",
    )(*ins)


def _layer(x, g_mix, w_in, g_q_a, g_k_a, g_lat_a, w_uk_a, w_uv_a, g_q_b, g_kc_b, g_ks_b, g_kw_b,
           pe_cmp_b, phi_k1_b, phi_k2_b, phi_v1_b, phi_v2_b, w_up_a, w_up_b, w_out, g_mlp, w_ff1, w_ff2):
    B, S, D = x.shape
    T = B * S
    f32 = jnp.float32
    mx = MXU_DTYPE
    n_slc = S // SLC_LEN
    assert n_slc <= MAX_SLC_BLOCKS and S % (N_WIDTH_CLASSES * TIE_CHUNK) == 0 and S >= WINDOW + Q_BLOCK
    col_sizes = (WIDTH_A, DSA_LATENT, IDX_HEADS * IDX_DIM, IDX_DIM, IDX_HEADS,
                 WIDTH_B, 6 * NSA_KV_W, 3 * NSA_HEADS, 2 * D)
    offs = np.cumsum((0,) + col_sizes)
    w_qa, w_ca, w_qi, w_ki, w_wi, w_qb, w_kvb, w_gb, w_gm = [
        w_in[:, offs[i]:offs[i + 1]] for i in range(len(col_sizes))]
    w_kc, w_vc, w_ks, w_vs, w_kw, w_vw = [w_kvb[:, j * NSA_KV_W:(j + 1) * NSA_KV_W] for j in range(6)]
    w_tok = jnp.concatenate(
        [w_ki, jnp.zeros((D, LANES - IDX_DIM), f32), w_ca, w_kc, w_vc, w_ks, w_kw], axis=1).astype(mx)
    assert w_tok.shape[1] == _T_COLS
    gw_pad = jnp.zeros((D, LANES - GATE_ROWS - IDX_HEADS), f32)
    w_feat = jnp.concatenate(
        [w_qi, w_gb, w_wi, gw_pad, w_qa, w_qb, w_vs, w_vw], axis=1).T.astype(mx)
    assert w_feat.shape[0] == _F_ROWS
    w_gate = w_gm.astype(mx)
    zpad = jnp.zeros((DSA_LATENT, LANES - HEAD_DIM), f32)
    w_uk = jnp.concatenate([w_uk_a, zpad], axis=1).astype(mx)
    w_uv_t = jnp.concatenate([w_uv_a, zpad], axis=1).T.astype(mx)
    row = lambda g: g.reshape(1, -1).astype(f32)
    rpad = lambda g: jnp.concatenate([row(g), jnp.zeros((1, LANES - HEAD_DIM), f32)], axis=1)
    colv = lambda g: g.reshape(-1, 1).astype(f32)

    x2 = x.reshape(T, D)
    (ki, ka, va_t, kcv, ks, kw, vs_t, vw_t, qi_t, gw_t, qa_t, qb_t) = _proj(
        x2, S, row(g_mix), w_tok, w_feat, row(g_lat_a), w_uk, w_uv_t, rpad(g_k_a), rpad(g_ks_b), rpad(g_kw_b),
        colv(g_q_a), colv(g_q_b), tm=256)

    n_chunk = S // CMP_STRIDE
    n_cmp = (S - CMP_LEN) // CMP_STRIDE + 1
    xc = kcv.reshape(2 * NSA_KV_HEADS, B, S, HEAD_DIM)
    phi1 = jnp.stack([phi_k1_b, phi_v1_b]).astype(mx)
    hpad = jnp.zeros((CMP_HIDDEN, LANES - HEAD_DIM), f32)
    w2k = jnp.concatenate([phi_k2_b, hpad], axis=1).astype(mx)
    w2v_t = jnp.concatenate([phi_v2_b, hpad], axis=1).T.astype(mx)
    kc, vc_t = _compress(xc, pe_cmp_b.astype(f32), phi1, w2k, w2v_t, rpad(g_kc_b))

    c_i = np.arange(n_chunk)[None, :] * CMP_STRIDE
    j_i = np.arange(MAX_SLC_BLOCKS)[:, None] * SLC_LEN
    ovl = (c_i < j_i + SLC_LEN) & (c_i + CMP_LEN > j_i) & (c_i < n_cmp * CMP_STRIDE) & (j_i < S)
    ovl = jnp.asarray(ovl, f32).astype(mx)

    o_a = _dsa(B, S, qi_t, gw_t, qa_t, ki, ka, va_t)
    o_b = _nsa(B, S, gw_t, qb_t, ks, kw, vs_t, vw_t, kc, vc_t, ovl)

    out = _merge_ffn(x2, o_a, o_b, row(g_mix), w_gate, w_up_a.astype(mx), w_up_b.astype(mx), w_out.astype(mx),
                     row(g_mlp), w_ff1.astype(mx), w_ff2.astype(mx), tm=512)
    return out.reshape(B, S, D)


def kernel(x, g_mix, w_in, g_q_a, g_k_a, g_lat_a, w_uk_a, w_uv_a, g_q_b, g_kc_b, g_ks_b, g_kw_b, pe_cmp_b,
           phi_k1_b, phi_k2_b, phi_v1_b, phi_v2_b, w_up_a, w_up_b, w_out, g_mlp, w_ff1, w_ff2):
    params = (g_mix, w_in, g_q_a, g_k_a, g_lat_a, w_uk_a, w_uv_a, g_q_b, g_kc_b, g_ks_b, g_kw_b, pe_cmp_b,
              phi_k1_b, phi_k2_b, phi_v1_b, phi_v2_b, w_up_a, w_up_b, w_out, g_mlp, w_ff1, w_ff2)
    for l in range(g_mix.shape[0]):
        x = _layer(x, *[p[l] for p in params])
    return x
```

```python
import functools
import math

import numpy as np
import jax
import jax.numpy as jnp
from jax import lax
from jax.experimental import pallas as pl
from jax.experimental.pallas import tpu as pltpu

HEAD_DIM = 64
DSA_HEADS = 8
DSA_LATENT = 128
IDX_HEADS = 8
IDX_DIM = 32
DSA_TOPK_MAX = 256
NSA_HEADS = 8
NSA_KV_HEADS = 2
NSA_GROUP = NSA_HEADS // NSA_KV_HEADS
CMP_LEN = 32
CMP_STRIDE = 16
CMP_HIDDEN = 128
SLC_LEN = 64
SLC_TOPN = 16
WINDOW = 512
FORCE_BONUS = 1e6
Q_BLOCK = 128
RMS_EPS = 1e-6
NEG_INF = -1e30

WIDTH_A = DSA_HEADS * HEAD_DIM
WIDTH_B = NSA_HEADS * HEAD_DIM
NSA_KV_W = NSA_KV_HEADS * HEAD_DIM

LANES = 128
GATE_ROWS = 3 * NSA_HEADS
ALIBI_COL = HEAD_DIM
BIAS_COL = 96
MAX_SLC_BLOCKS = LANES - BIAS_COL
BLOCK_BIAS = 2.0 ** 100
LOG2E = math.log2(math.e)
MXU_DTYPE = jnp.bfloat16
VMEM_LIMIT = 52 * 1024 * 1024
DSA_WIDTH_CLASSES = 16
NSA_WIDTH_CLASSES = 8
TIE_CHUNK = 128
REDUCE_ROWS = 64
QK_SIDE_STEPS = 8

_NT = (((1,), (1,)), ((), ()))


def _alibi_slopes(n_heads):
    return [float(v) for v in np.asarray(
        2.0 ** (-8.0 * np.arange(1, n_heads + 1) / n_heads), dtype=np.float32)]


def _dot(a, b):
    return jnp.dot(a, b, preferred_element_type=jnp.float32)


def _dot_nt(a, b):
    return lax.dot_general(a, b, _NT, preferred_element_type=jnp.float32)


def _rms(x, g):
    ms = jnp.mean(x * x, axis=-1, keepdims=True)
    return x * lax.rsqrt(ms + RMS_EPS) * g


def _params(sem):
    return pltpu.CompilerParams(dimension_semantics=sem, vmem_limit_bytes=VMEM_LIMIT)


def _split3(c):
    c1 = c.astype(MXU_DTYPE).astype(jnp.float32)
    c2 = (c - c1).astype(MXU_DTYPE).astype(jnp.float32)
    c3 = (c - c1 - c2).astype(MXU_DTYPE).astype(jnp.float32)
    return c1, c2, c3


_T_KI = (0, 128)
_T_CA = (128, 256)
_T_KCV = (256, 512)
_T_KS = (512, 640)
_T_KW = (640, 768)
_T_COLS = 768
_F_QI = (0, 256)
_F_GW = (256, 384)
_F_QA = (384, 896)
_F_QB = (896, 1408)
_F_VS = (1408, 1536)
_F_VW = (1536, 1664)
_F_ROWS = 1664


def _proj_kernel(x_ref, gmix_ref, wtok_ref, wfeat_ref, glat_ref, wuk_ref, wuvt_ref, gka_ref, gks_ref, gkw_ref,
                 gqa_ref, gqb_ref,
                 ki_ref, ka_ref, vat_ref, kcv_ref, ks_ref, kw_ref, vst_ref, vwt_ref, qit_ref, gwt_ref,
                 qat_ref, qbt_ref, *, seq_len, w_idx_scale, slopes_a, slopes_b):
    f32 = jnp.float32
    x = x_ref[...]
    tm = x.shape[0]
    h = _rms(x, gmix_ref[...]).astype(MXU_DTYPE)

    def tok(c):
        return _dot(h, wtok_ref[:, c[0]:c[1]])

    def feat(r):
        return _dot_nt(wfeat_ref[r[0]:r[1], :], h)

    s0 = (pl.program_id(0) % (seq_len // tm)) * tm
    spos = s0 + lax.broadcasted_iota(jnp.int32, (tm, LANES), 0)
    lane = lax.broadcasted_iota(jnp.int32, (tm, LANES), 1)
    c1, c2, c3 = _split3(spos.astype(f32) * LOG2E)
    alibi_cols = jnp.where(lane == ALIBI_COL, c1,
                           jnp.where(lane == ALIBI_COL + 1, c2, jnp.where(lane == ALIBI_COL + 2, c3, 0.0)))
    block_onehot = jnp.where(lane - BIAS_COL == lax.shift_right_logical(spos, 6), 1.0, 0.0)

    def key_slab(raw, g):
        ms = jnp.sum(raw * raw, axis=-1, keepdims=True) * (1.0 / HEAD_DIM)
        return raw * lax.rsqrt(ms + RMS_EPS) * g

    ki_ref[...] = tok(_T_KI).astype(ki_ref.dtype)
    c = _rms(tok(_T_CA), glat_ref[...]).astype(MXU_DTYPE)
    ka_ref[...] = (key_slab(_dot(c, wuk_ref[...]), gka_ref[...]) + alibi_cols).astype(ka_ref.dtype)
    va_t = _dot_nt(wuvt_ref[...], c)
    frow = lax.broadcasted_iota(jnp.int32, va_t.shape, 0)
    vat_ref[...] = jnp.where(frow == HEAD_DIM, 1.0, va_t).astype(vat_ref.dtype)
    kcv = tok(_T_KCV)
    for j in range(2 * NSA_KV_HEADS):
        kcv_ref[j] = kcv[:, j * HEAD_DIM:(j + 1) * HEAD_DIM]
    low = lane < HEAD_DIM
    for cols, g_ref, out_ref, extra in ((_T_KS, gks_ref, ks_ref, alibi_cols + block_onehot),
                                        (_T_KW, gkw_ref, kw_ref, alibi_cols)):
        pair = tok(cols)
        for kh in range(NSA_KV_HEADS):
            raw = pair if kh == 0 else pltpu.roll(pair, HEAD_DIM, 1)
            slab = key_slab(jnp.where(low, raw, 0.0), g_ref[...]) + extra
            out_ref[:, kh * LANES:(kh + 1) * LANES] = slab.astype(out_ref.dtype)

    qit_ref[...] = feat(_F_QI).astype(qit_ref.dtype)
    gw = feat(_F_GW)
    grow = lax.broadcasted_iota(jnp.int32, gw.shape, 0)
    gwt_ref[...] = jnp.where(grow < GATE_ROWS, jax.nn.sigmoid(gw), gw * w_idx_scale)
    erow = lax.broadcasted_iota(jnp.int32, (HEAD_DIM, tm), 0)
    q_scale = HEAD_DIM ** -0.5 * LOG2E
    for rows, g_ref, out_ref, slopes in ((_F_QA, gqa_ref, qat_ref, slopes_a), (_F_QB, gqb_ref, qbt_ref, slopes_b)):
        q_all = feat(rows)
        for hd in range(len(slopes)):
            q = q_all[hd * HEAD_DIM:(hd + 1) * HEAD_DIM]
            ms = jnp.mean(q * q, axis=0, keepdims=True)
            q = q * lax.rsqrt(ms + RMS_EPS) * g_ref[...] * q_scale
            out_ref[hd * LANES:hd * LANES + HEAD_DIM, :] = q.astype(out_ref.dtype)
            extra = jnp.where(erow < 3, slopes[hd], 0.0)
            out_ref[hd * LANES + HEAD_DIM:(hd + 1) * LANES, :] = extra.astype(out_ref.dtype)
    ones_row = jnp.where(erow == 0, 1.0, 0.0)
    for rows, out_ref in ((_F_VS, vst_ref), (_F_VW, vwt_ref)):
        v_t = feat(rows)
        for kh in range(NSA_KV_HEADS):
            out_ref[kh * LANES:kh * LANES + HEAD_DIM, :] = v_t[kh * HEAD_DIM:(kh + 1) * HEAD_DIM].astype(out_ref.dtype)
            out_ref[kh * LANES + HEAD_DIM:(kh + 1) * LANES, :] = ones_row.astype(out_ref.dtype)


def _proj(x2, seq_len, g_mix, w_tok, w_feat, g_lat, w_uk, w_uv_t, gk_a, gks, gkw, gq_a, gq_b, tm):
    T, D = x2.shape
    tokm = lambda w: pl.BlockSpec((tm, w), lambda i: (i, 0))
    featm = lambda r: pl.BlockSpec((r, tm), lambda i: (0, i))
    full = lambda a: pl.BlockSpec(a.shape, lambda i: (0,) * a.ndim)
    f32 = jnp.float32
    mx = MXU_DTYPE
    out_shape = (
        jax.ShapeDtypeStruct((T, LANES), mx),
        jax.ShapeDtypeStruct((T, LANES), mx),
        jax.ShapeDtypeStruct((LANES, T), mx),
        jax.ShapeDtypeStruct((2 * NSA_KV_HEADS, T, HEAD_DIM), f32),
        jax.ShapeDtypeStruct((T, NSA_KV_HEADS * LANES), mx),
        jax.ShapeDtypeStruct((T, NSA_KV_HEADS * LANES), mx),
        jax.ShapeDtypeStruct((NSA_KV_HEADS * LANES, T), mx),
        jax.ShapeDtypeStruct((NSA_KV_HEADS * LANES, T), mx),
        jax.ShapeDtypeStruct((IDX_HEADS * IDX_DIM, T), mx),
        jax.ShapeDtypeStruct((LANES, T), f32),
        jax.ShapeDtypeStruct((DSA_HEADS * LANES, T), mx),
        jax.ShapeDtypeStruct((NSA_HEADS * LANES, T), mx),
    )
    out_specs = (
        tokm(LANES), tokm(LANES), featm(LANES),
        pl.BlockSpec((2 * NSA_KV_HEADS, tm, HEAD_DIM), lambda i: (0, i, 0)),
        tokm(NSA_KV_HEADS * LANES), tokm(NSA_KV_HEADS * LANES),
        featm(NSA_KV_HEADS * LANES), featm(NSA_KV_HEADS * LANES),
        featm(IDX_HEADS * IDX_DIM), featm(LANES), featm(DSA_HEADS * LANES), featm(NSA_HEADS * LANES),
    )
    kern = functools.partial(
        _proj_kernel, seq_len=seq_len, w_idx_scale=IDX_HEADS ** -0.5 * IDX_DIM ** -0.5,
        slopes_a=_alibi_slopes(DSA_HEADS), slopes_b=_alibi_slopes(NSA_HEADS))
    ins = (x2, g_mix, w_tok, w_feat, g_lat, w_uk, w_uv_t, gk_a, gks, gkw, gq_a, gq_b)
    return pl.pallas_call(
        kern,
        grid=(T // tm,),
        in_specs=[tokm(D)] + [full(a) for a in ins[1:]],
        out_specs=out_specs,
        out_shape=out_shape,
        compiler_params=_params(("parallel",)),
        name="proj",
    )(*ins)


def _compress_kernel(xk_ref, xv_ref, pe_ref, w1_ref, w2k_ref, w2vt_ref, gkc_ref, kc_ref, vct_ref):
    n = kc_ref.shape[0]

    def hidden(x_ref, kv):
        a = jnp.zeros((n, CMP_HIDDEN), jnp.float32)
        b = jnp.zeros((n, CMP_HIDDEN), jnp.float32)
        for l in range(CMP_STRIDE):
            x_l = x_ref[pl.ds(l, n, stride=CMP_STRIDE), :]
            for acc_is_b, row in ((False, l), (True, CMP_STRIDE + l)):
                lhs = (x_l + pe_ref[row:row + 1, :]).astype(MXU_DTYPE)
                part = _dot(lhs, w1_ref[kv, row * HEAD_DIM:(row + 1) * HEAD_DIM, :])
                if acc_is_b:
                    b = b + part
                else:
                    a = a + part
        return jax.nn.gelu(a + pltpu.roll(b, n - 1, 0)).astype(MXU_DTYPE)

    yk = _dot(hidden(xk_ref, 0), w2k_ref[...])
    ms = jnp.sum(yk * yk, axis=-1, keepdims=True) * (1.0 / HEAD_DIM)
    kc_ref[...] = (yk * lax.rsqrt(ms + RMS_EPS) * gkc_ref[...]).astype(kc_ref.dtype)
    vct_ref[...] = _dot_nt(w2vt_ref[...], hidden(xv_ref, 1)).astype(vct_ref.dtype)


def _compress(xc, pe, phi1, w2k, w2v_t, g_kc):
    _, B, S, w = xc.shape
    n = S // CMP_STRIDE
    full = lambda a: pl.BlockSpec(a.shape, lambda b, kh: (0,) * a.ndim)
    return pl.pallas_call(
        _compress_kernel,
        grid=(B, NSA_KV_HEADS),
        in_specs=[
            pl.BlockSpec((None, None, S, w), lambda b, kh: (kh, b, 0, 0)),
            pl.BlockSpec((None, None, S, w), lambda b, kh: (NSA_KV_HEADS + kh, b, 0, 0)),
            full(pe), full(phi1), full(w2k), full(w2v_t), full(g_kc),
        ],
        out_specs=(pl.BlockSpec((None, None, n, LANES), lambda b, kh: (b, kh, 0, 0)),
                   pl.BlockSpec((None, None, LANES, n), lambda b, kh: (b, kh, 0, 0))),
        out_shape=(jax.ShapeDtypeStruct((B, NSA_KV_HEADS, n, LANES), MXU_DTYPE),
                   jax.ShapeDtypeStruct((B, NSA_KV_HEADS, LANES, n), MXU_DTYPE)),
        compiler_params=_params(("parallel", "parallel")),
        name="compress",
    )(xc, xc, pe, phi1, w2k, w2v_t, g_kc)


_INT_MIN = -2 ** 31


def _key_to_float(u):
    key = u ^ jnp.int32(_INT_MIN)
    bits = jnp.where(key >= 0, key, key ^ jnp.int32(0x7FFFFFFF))
    return lax.bitcast_convert_type(bits, jnp.float32)


def _col_reduce(x, op, final):
    rows = REDUCE_ROWS
    acc = x
    if x.shape[0] > rows and x.shape[0] % rows == 0:
        acc = x[:rows]
        for j in range(1, x.shape[0] // rows):
            acc = op(acc, x[j * rows:(j + 1) * rows])
    return final(acc.astype(jnp.float32), axis=0, keepdims=True)


def _col_sum(x):
    return _col_reduce(x, jnp.add, jnp.sum)


def _col_max(x):
    return _col_reduce(x, jnp.maximum, jnp.max)


def _kth_largest(score, k, side_steps=0, side_work=None):
    def body(it, u):
        u_try = u | lax.shift_left(jnp.int32(1), 31 - it)
        cnt = _col_sum(jnp.where(score >= _key_to_float(u_try), 1.0, 0.0))
        return jnp.where(cnt >= k, u_try, u)

    def body_with_side_work(j, u):
        side_work(j)
        for i in range(32 // side_steps):
            u = body(j * (32 // side_steps) + i, u)
        return u

    u = jnp.zeros((1, score.shape[1]), jnp.int32)
    if side_steps:
        u = lax.fori_loop(0, side_steps, body_with_side_work, u)
    else:
        u = lax.fori_loop(0, 32, body, u)
    thr = _key_to_float(u)
    return jnp.where(thr != thr, -jnp.inf, thr)


def _lane_tile(x, n):
    return jnp.concatenate([x] * n, axis=1)


def _softmax_pv(parts):
    m = None
    for s, _ in parts:
        m_p = _col_max(s)
        m = m_p if m is None else jnp.maximum(m, m_p)
    out = None
    for s, v_t in parts:
        o_p = _dot(v_t, jnp.exp2(s - m).astype(MXU_DTYPE))
        out = o_p if out is None else out + o_p
    return out


def _width_classes(n_qblk, n_classes):
    n_cls = n_classes if n_qblk % n_classes == 0 else 1
    per = n_qblk // n_cls
    return [(c * per, (c + 1) * per, (c + 1) * per * Q_BLOCK) for c in range(n_cls)]


def _dsa_body(W, qit_ref, gwt_ref, qat_ref, ki_ref, ka_ref, vat_ref, o_ref, sel_ref, qk_ref,*, topk):
    f32 = jnp.float32
    QB = Q_BLOCK
    t = pl.program_id(1) * QB + lax.broadcasted_iota(jnp.int32, (1, QB), 1)
    spos = lax.broadcasted_iota(jnp.int32, (W, QB), 0)
    causal = spos <= t

    ki = ki_ref[:W, :]
    G = 4
    kpad = jnp.zeros((LANES - IDX_DIM, G * QB), MXU_DTYPE)
    score = jnp.zeros((W, QB), f32)
    for hg in range(IDX_HEADS // G):
        q_idx = jnp.concatenate(
            [qit_ref[(hg * G + g) * IDX_DIM:(hg * G + g + 1) * IDX_DIM, :] for g in range(G)], axis=1)
        logits = _dot(ki, jnp.concatenate([q_idx, kpad], axis=0))
        for g in range(G):
            w_h = gwt_ref[GATE_ROWS + hg * G + g:GATE_ROWS + hg * G + g + 1, :]
            score = score + jnp.maximum(logits[:, g * QB:(g + 1) * QB], 0.0) * w_h
    score = jnp.where(causal, score, -jnp.inf)

    q_all = jnp.concatenate([qat_ref[h * LANES:(h + 1) * LANES, :] for h in range(DSA_HEADS)], axis=1)
    chunk = W // QK_SIDE_STEPS

    def qk_chunk(it):
        r0 = pl.multiple_of(it * chunk, chunk)
        qk_ref[pl.ds(r0, chunk), :] = _dot(ka_ref[pl.ds(r0, chunk), :], q_all)

    thr = _kth_largest(score, float(topk), QK_SIDE_STEPS, qk_chunk)
    thr = jnp.maximum(thr, float(jnp.finfo(jnp.float32).min))
    ge = score >= thr
    sel_ref[:W, :] = jnp.where(ge, 0.0, NEG_INF)

    @pl.when(jnp.max(_col_sum(jnp.where(ge, 1.0, 0.0))) > float(topk))
    def _():
        gt = score > thr
        eq = score == thr
        need = float(topk) - _col_sum(jnp.where(gt, 1.0, 0.0))
        cw = TIE_CHUNK
        r = lax.broadcasted_iota(jnp.int32, (cw, cw), 0)
        c = lax.broadcasted_iota(jnp.int32, (cw, cw), 1)
        lower = jnp.where(c <= r, 1.0, 0.0).astype(MXU_DTYPE)
        carry = jnp.zeros((1, QB), f32)
        for j in range(W // cw):
            sl = slice(j * cw, (j + 1) * cw)
            eq_j = jnp.where(eq[sl], 1.0, 0.0)
            prefix = _dot(lower, eq_j.astype(MXU_DTYPE)) + carry
            keep = gt[sl] | (eq[sl] & (prefix <= need))
            sel_ref[sl, :] = jnp.where(keep, 0.0, NEG_INF)
            carry = carry + jnp.sum(eq_j, axis=0, keepdims=True)

    G = 4
    va_t = vat_ref[:, :W]
    for hg in range(DSA_HEADS // G):
        s = qk_ref[:W, hg * G * QB:(hg + 1) * G * QB] + _lane_tile(sel_ref[:W, :], G)
        o_t = _softmax_pv([(s, va_t)])
        o_t = o_t[:HEAD_DIM] * (1.0 / o_t[HEAD_DIM:HEAD_DIM + 1])
        for g2 in range(G // 2):
            pair = jnp.concatenate([o_t[:, (2 * g2) * QB:(2 * g2 + 1) * QB],
                                    o_t[:, (2 * g2 + 1) * QB:(2 * g2 + 2) * QB]], axis=0)
            col = (hg * G + 2 * g2) * HEAD_DIM
            o_ref[:, col:col + 2 * HEAD_DIM] = pair.T.astype(o_ref.dtype)


def _dsa_kernel(qit_ref, gwt_ref, qat_ref, ki_ref, ka_ref, vat_ref, o_ref, sel_ref, qk_ref,*, topk, classes):
    qblk = pl.program_id(1)
    for lo, hi, W in classes:
        @pl.when((qblk >= lo) & (qblk < hi))
        def _(W=W):
            _dsa_body(W, qit_ref, gwt_ref, qat_ref, ki_ref, ka_ref, vat_ref, o_ref, sel_ref, qk_ref,topk=topk)


def _dsa(B, S, qi_t, gw_t, qa_t, ki, ka, va_t):
    nq = S // Q_BLOCK
    topk = min(DSA_TOPK_MAX, S // 4)
    fblk = lambda r: pl.BlockSpec((r, Q_BLOCK), lambda b, i: (0, b * nq + i))
    tseq = lambda w: pl.BlockSpec((S, w), lambda b, i: (b, 0))
    fseq = lambda r: pl.BlockSpec((r, S), lambda b, i: (0, b))
    return pl.pallas_call(
        functools.partial(_dsa_kernel, topk=topk, classes=_width_classes(nq, DSA_WIDTH_CLASSES)),
        grid=(B, nq),
        in_specs=[fblk(IDX_HEADS * IDX_DIM), fblk(LANES), fblk(DSA_HEADS * LANES),
                  tseq(LANES), tseq(LANES), fseq(LANES)],
        out_specs=pl.BlockSpec((Q_BLOCK, WIDTH_A), lambda b, i: (b * nq + i, 0)),
        out_shape=jax.ShapeDtypeStruct((B * S, WIDTH_A), MXU_DTYPE),
        scratch_shapes=[pltpu.VMEM((S, Q_BLOCK), jnp.float32),
                        pltpu.VMEM((S, DSA_HEADS * Q_BLOCK), jnp.float32)],
        compiler_params=_params(("parallel", "arbitrary")),
        name="dsa",
    )(qi_t, gw_t, qa_t, ki, ka, va_t)


def _nsa_body(W, first_blk, gwt_ref, qbt_ref, ks_ref, kw_ref, vst_ref, vwt_ref, kc_ref, vct_ref, ovl_ref, o_ref,
              *, seq_len):
    f32 = jnp.float32
    QB, G = Q_BLOCK, NSA_GROUP
    n_slc = seq_len // SLC_LEN
    n_top = min(SLC_TOPN, n_slc)
    win_len = WINDOW + QB
    nc = kc_ref.shape[1]

    t0 = pl.program_id(1) * QB
    t = t0 + lax.broadcasted_iota(jnp.int32, (1, QB), 1)
    t4 = _lane_tile(t, G)
    valid_c = lax.broadcasted_iota(jnp.int32, (nc, G * QB), 0) * CMP_STRIDE + (CMP_LEN - 1) <= t4
    past = first_blk * QB
    causal_tail = past + lax.broadcasted_iota(jnp.int32, (W - past, G * QB), 0) <= t4
    w0 = pl.multiple_of(jnp.maximum(t0 - WINDOW, 0), QB)
    full_window = past >= WINDOW
    if full_window:
        r_minus_lane = lax.broadcasted_iota(jnp.int32, (QB, G * QB), 0) - (t4 - t0)
        w_head_ok = r_minus_lane > 0
        w_tail_ok = r_minus_lane <= 0
    else:
        wd = t4 - (w0 + lax.broadcasted_iota(jnp.int32, (win_len, G * QB), 0))
        wvalid = (wd >= 0) & (wd < WINDOW)
    jblk = lax.broadcasted_iota(jnp.int32, (MAX_SLC_BLOCKS, QB), 0)
    cur = lax.shift_right_logical(t, 6)
    future = jblk > cur
    forced = (jblk == 0) | (jblk == cur) | (jblk == cur - 1)
    ovl = ovl_ref[...]

    KH = range(NSA_KV_HEADS)
    ksl = [slice(kh * LANES, (kh + 1) * LANES) for kh in KH]
    q = [jnp.concatenate([qbt_ref[(kh * G + g) * LANES:(kh * G + g + 1) * LANES, :] for g in range(G)], axis=1)
         for kh in KH]

    s_c = [jnp.where(valid_c, _dot(kc_ref[kh], q[kh]), NEG_INF) for kh in KH]
    s_w = [_dot(kw_ref[pl.ds(w0, win_len), ksl[kh]], q[kh]) for kh in KH]
    e_c = [jnp.where(valid_c, jnp.exp2(s_c[kh] - jnp.max(s_c[kh], axis=0, keepdims=True)), 0.0) for kh in KH]
    l_c = [jnp.sum(e_c[kh], axis=0, keepdims=True) for kh in KH]
    p_c = [e_c[kh] * (1.0 / jnp.where(l_c[kh] > 0.0, l_c[kh], 1.0)) for kh in KH]
    o_cmp = [_dot(vct_ref[kh], p_c[kh].astype(MXU_DTYPE)) for kh in KH]

    val = []
    for kh in KH:
        pc = p_c[kh]
        p_sum = pc[:, 0:QB] + pc[:, QB:2 * QB] + pc[:, 2 * QB:3 * QB] + pc[:, 3 * QB:4 * QB]
        p_hi = p_sum.astype(MXU_DTYPE)
        p_lo = (p_sum - p_hi.astype(f32)).astype(MXU_DTYPE)
        imp = _dot(ovl, p_hi) + _dot(ovl, p_lo)
        val.append(jnp.where(future, -jnp.inf, imp + jnp.where(forced, FORCE_BONUS, 0.0)))

    o_win = []
    for kh in KH:
        v_w = vwt_ref[ksl[kh], pl.ds(w0, win_len)]
        if full_window:
            parts = [(jnp.where(w_head_ok, s_w[kh][:QB], NEG_INF), v_w[:, :QB]),
                     (s_w[kh][QB:WINDOW], v_w[:, QB:WINDOW]),
                     (jnp.where(w_tail_ok, s_w[kh][WINDOW:], NEG_INF), v_w[:, WINDOW:])]
        else:
            parts = [(jnp.where(wvalid, s_w[kh], NEG_INF), v_w)]
        o_win.append(_softmax_pv(parts))

    rank = [jnp.zeros(val[kh].shape, f32) for kh in KH]
    for i in range(n_slc):
        for kh in KH:
            vi = val[kh][i:i + 1, :]
            rank[kh] = rank[kh] + jnp.where((vi > val[kh]) | ((vi == val[kh]) & (jblk > i)), 1.0, 0.0)

    s_s = []
    for kh in KH:
        bias = jnp.where((rank[kh] < n_top) & (jblk < n_slc), 0.0, -BLOCK_BIAS).astype(MXU_DTYPE)
        q_sel = jnp.concatenate([q[kh][:BIAS_COL], _lane_tile(bias, G)], axis=0)
        s_s.append(_dot(ks_ref[:W, ksl[kh]], q_sel))
    o_slc = []
    for kh in KH:
        parts = [(jnp.where(causal_tail, s_s[kh][past:], NEG_INF), vst_ref[ksl[kh], past:W])]
        if past:
            parts.append((s_s[kh][:past], vst_ref[ksl[kh], :past]))
        o_slc.append(_softmax_pv(parts))

    for kh in KH:
        inv_s = 1.0 / o_slc[kh][HEAD_DIM:HEAD_DIM + 1]
        inv_w = 1.0 / o_win[kh][HEAD_DIM:HEAD_DIM + 1]
        heads = []
        for g in range(G):
            h = kh * G + g
            ls = slice(g * QB, (g + 1) * QB)
            g0 = gwt_ref[3 * h + 0:3 * h + 1, :]
            g1 = gwt_ref[3 * h + 1:3 * h + 2, :] * inv_s[:, ls]
            g2 = gwt_ref[3 * h + 2:3 * h + 3, :] * inv_w[:, ls]
            heads.append(g0 * o_cmp[kh][:HEAD_DIM, ls] + g1 * o_slc[kh][:HEAD_DIM, ls]
                         + g2 * o_win[kh][:HEAD_DIM, ls])
        for g2_ in range(G // 2):
            pair = jnp.concatenate([heads[2 * g2_], heads[2 * g2_ + 1]], axis=0)
            col = (kh * G + 2 * g2_) * HEAD_DIM
            o_ref[:, col:col + 2 * HEAD_DIM] = pair.T.astype(o_ref.dtype)


def _nsa_kernel(gwt_ref, qbt_ref, ks_ref, kw_ref, vst_ref, vwt_ref, kc_ref, vct_ref, ovl_ref, o_ref,
                *, seq_len, classes):
    qblk = pl.program_id(1)
    for lo, hi, W in classes:
        @pl.when((qblk >= lo) & (qblk < hi))
        def _(W=W, lo=lo):
            _nsa_body(W, lo, gwt_ref, qbt_ref, ks_ref, kw_ref, vst_ref, vwt_ref, kc_ref, vct_ref, ovl_ref, o_ref,
                      seq_len=seq_len)


def _nsa(B, S, gw_t, qb_t, ks, kw, vs_t, vw_t, kc, vc_t, ovl):
    nq = S // Q_BLOCK
    nc = kc.shape[2]
    kvw = NSA_KV_HEADS * LANES
    fblk = lambda r: pl.BlockSpec((r, Q_BLOCK), lambda b, i: (0, b * nq + i))
    tseq = lambda w: pl.BlockSpec((S, w), lambda b, i: (b, 0))
    fseq = lambda r: pl.BlockSpec((r, S), lambda b, i: (0, b))
    return pl.pallas_call(
        functools.partial(_nsa_kernel, seq_len=S, classes=_width_classes(nq, NSA_WIDTH_CLASSES)),
        grid=(B, nq),
        in_specs=[fblk(LANES), fblk(NSA_HEADS * LANES), tseq(kvw), tseq(kvw), fseq(kvw), fseq(kvw),
                  pl.BlockSpec((None, NSA_KV_HEADS, nc, LANES), lambda b, i: (b, 0, 0, 0)),
                  pl.BlockSpec((None, NSA_KV_HEADS, LANES, nc), lambda b, i: (b, 0, 0, 0)),
                  pl.BlockSpec(ovl.shape, lambda b, i: (0, 0))],
        out_specs=pl.BlockSpec((Q_BLOCK, WIDTH_B), lambda b, i: (b * nq + i, 0)),
        out_shape=jax.ShapeDtypeStruct((B * S, WIDTH_B), MXU_DTYPE),
        compiler_params=_params(("parallel", "arbitrary")),
        name="nsa",
    )(gw_t, qb_t, ks, kw, vs_t, vw_t, kc, vc_t, ovl)


def _merge_ffn_kernel(x_ref, oa_ref, ob_ref, gmix_ref, wg_ref, wua_ref, wub_ref, wo_ref,
                      gmlp_ref, w1_ref, w2_ref, o_ref, *, chunk):
    x = x_ref[...]
    D = x.shape[1]
    h = _rms(x, gmix_ref[...]).astype(MXU_DTYPE)
    ga = jax.nn.sigmoid(_dot(h, wg_ref[:, :D]))
    gb = jax.nn.sigmoid(_dot(h, wg_ref[:, D:]))
    merged = ga * _dot(oa_ref[...], wua_ref[...]) + gb * _dot(ob_ref[...], wub_ref[...])
    x1 = x + _dot(merged.astype(MXU_DTYPE), wo_ref[...])
    h2 = _rms(x1, gmlp_ref[...]).astype(MXU_DTYPE)
    acc = x1
    for c in range(w1_ref.shape[1] // chunk):
        u = jnp.maximum(_dot(h2, w1_ref[:, c * chunk:(c + 1) * chunk]), 0.0)
        acc = acc + _dot((u * u).astype(MXU_DTYPE), w2_ref[c * chunk:(c + 1) * chunk, :])
    o_ref[...] = acc


def _merge_ffn(x2, oa, ob, g_mix, w_gate, w_up_a, w_up_b, w_out, g_mlp, w1, w2, tm):
    T, D = x2.shape
    row = lambda w: pl.BlockSpec((tm, w), lambda i: (i, 0))
    full = lambda a: pl.BlockSpec(a.shape, lambda i: (0,) * a.ndim, pipeline_mode=pl.Buffered(1))
    ins = (x2, oa, ob, g_mix, w_gate, w_up_a, w_up_b, w_out, g_mlp, w1, w2)
    return pl.pallas_call(
        functools.partial(_merge_ffn_kernel, chunk=1024),
        grid=(T // tm,),
        in_specs=[row(D), row(WIDTH_A), row(WIDTH_B)] + [full(a) for a in ins[3:]],
        out_specs=row(D),
        out_shape=jax.ShapeDtypeStruct((T, D), jnp.float32),
        compiler_params=_params(("parallel",)),
        name="merge_ffn",
    )(*ins)


def _layer(x, g_mix, w_in, g_q_a, g_k_a, g_lat_a, w_uk_a, w_uv_a, g_q_b, g_kc_b, g_ks_b, g_kw_b,
           pe_cmp_b, phi_k1_b, phi_k2_b, phi_v1_b, phi_v2_b, w_up_a, w_up_b, w_out, g_mlp, w_ff1, w_ff2):
    B, S, D = x.shape
    T = B * S
    f32 = jnp.float32
    mx = MXU_DTYPE
    n_slc = S // SLC_LEN
    assert n_slc <= MAX_SLC_BLOCKS and S % (DSA_WIDTH_CLASSES * TIE_CHUNK) == 0 and S >= WINDOW + Q_BLOCK
    col_sizes = (WIDTH_A, DSA_LATENT, IDX_HEADS * IDX_DIM, IDX_DIM, IDX_HEADS,
                 WIDTH_B, 6 * NSA_KV_W, 3 * NSA_HEADS, 2 * D)
    offs = np.cumsum((0,) + col_sizes)
    w_qa, w_ca, w_qi, w_ki, w_wi, w_qb, w_kvb, w_gb, w_gm = [
        w_in[:, offs[i]:offs[i + 1]] for i in range(len(col_sizes))]
    w_kc, w_vc, w_ks, w_vs, w_kw, w_vw = [w_kvb[:, j * NSA_KV_W:(j + 1) * NSA_KV_W] for j in range(6)]
    w_tok = jnp.concatenate(
        [w_ki, jnp.zeros((D, LANES - IDX_DIM), f32), w_ca, w_kc, w_vc, w_ks, w_kw], axis=1).astype(mx)
    assert w_tok.shape[1] == _T_COLS
    gw_pad = jnp.zeros((D, LANES - GATE_ROWS - IDX_HEADS), f32)
    w_feat = jnp.concatenate(
        [w_qi, w_gb, w_wi, gw_pad, w_qa, w_qb, w_vs, w_vw], axis=1).T.astype(mx)
    assert w_feat.shape[0] == _F_ROWS
    w_gate = w_gm.astype(mx)
    zpad = jnp.zeros((DSA_LATENT, LANES - HEAD_DIM), f32)
    w_uk = jnp.concatenate([w_uk_a, zpad], axis=1).astype(mx)
    w_uv_t = jnp.concatenate([w_uv_a, zpad], axis=1).T.astype(mx)
    row = lambda g: g.reshape(1, -1).astype(f32)
    rpad = lambda g: jnp.concatenate([row(g), jnp.zeros((1, LANES - HEAD_DIM), f32)], axis=1)
    colv = lambda g: g.reshape(-1, 1).astype(f32)

    x2 = x.reshape(T, D)
    (ki, ka, va_t, kcv, ks, kw, vs_t, vw_t, qi_t, gw_t, qa_t, qb_t) = _proj(
        x2, S, row(g_mix), w_tok, w_feat, row(g_lat_a), w_uk, w_uv_t, rpad(g_k_a), rpad(g_ks_b), rpad(g_kw_b),
        colv(g_q_a), colv(g_q_b), tm=256)

    n_chunk = S // CMP_STRIDE
    n_cmp = (S - CMP_LEN) // CMP_STRIDE + 1
    xc = kcv.reshape(2 * NSA_KV_HEADS, B, S, HEAD_DIM)
    phi1 = jnp.stack([phi_k1_b, phi_v1_b]).astype(mx)
    hpad = jnp.zeros((CMP_HIDDEN, LANES - HEAD_DIM), f32)
    w2k = jnp.concatenate([phi_k2_b, hpad], axis=1).astype(mx)
    w2v_t = jnp.concatenate([phi_v2_b, hpad], axis=1).T.astype(mx)
    kc, vc_t = _compress(xc, pe_cmp_b.astype(f32), phi1, w2k, w2v_t, rpad(g_kc_b))

    c_i = np.arange(n_chunk)[None, :] * CMP_STRIDE
    j_i = np.arange(MAX_SLC_BLOCKS)[:, None] * SLC_LEN
    ovl = (c_i < j_i + SLC_LEN) & (c_i + CMP_LEN > j_i) & (c_i < n_cmp * CMP_STRIDE) & (j_i < S)
    ovl = jnp.asarray(ovl, f32).astype(mx)

    o_a = _dsa(B, S, qi_t, gw_t, qa_t, ki, ka, va_t)
    o_b = _nsa(B, S, gw_t, qb_t, ks, kw, vs_t, vw_t, kc, vc_t, ovl)

    out = _merge_ffn(x2, o_a, o_b, row(g_mix), w_gate, w_up_a.astype(mx), w_up_b.astype(mx), w_out.astype(mx),
                     row(g_mlp), w_ff1.astype(mx), w_ff2.astype(mx), tm=512)
    return out.reshape(B, S, D)


def kernel(x, g_mix, w_in, g_q_a, g_k_a, g_lat_a, w_uk_a, w_uv_a, g_q_b, g_kc_b, g_ks_b, g_kw_b, pe_cmp_b,
           phi_k1_b, phi_k2_b, phi_v1_b, phi_v2_b, w_up_a, w_up_b, w_out, g_mlp, w_ff1, w_ff2):
    params = (g_mix, w_in, g_q_a, g_k_a, g_lat_a, w_uk_a, w_uv_a, g_q_b, g_kc_b, g_ks_b, g_kw_b, pe_cmp_b,
              phi_k1_b, phi_k2_b, phi_v1_b, phi_v2_b, w_up_a, w_up_b, w_out, g_mlp, w_ff1, w_ff2)
    for l in range(g_mix.shape[0]):
        x = _layer(x, *[p[l] for p in params])
    return x
```

```python
import functools
import math

import numpy as np
import jax
import jax.numpy as jnp
from jax import lax
from jax.experimental import pallas as pl
from jax.experimental.pallas import tpu as pltpu

HEAD_DIM = 64
DSA_HEADS = 8
DSA_LATENT = 128
IDX_HEADS = 8
IDX_DIM = 32
DSA_TOPK_MAX = 256
NSA_HEADS = 8
NSA_KV_HEADS = 2
NSA_GROUP = NSA_HEADS // NSA_KV_HEADS
CMP_LEN = 32
CMP_STRIDE = 16
CMP_HIDDEN = 128
SLC_LEN = 64
SLC_TOPN = 16
WINDOW = 512
FORCE_BONUS = 1e6
Q_BLOCK = 128
RMS_EPS = 1e-6
NEG_INF = -1e30

WIDTH_A = DSA_HEADS * HEAD_DIM
WIDTH_B = NSA_HEADS * HEAD_DIM
NSA_KV_W = NSA_KV_HEADS * HEAD_DIM

LANES = 128
GATE_ROWS = 3 * NSA_HEADS
ALIBI_COL = HEAD_DIM
BIAS_COL = 96
MAX_SLC_BLOCKS = LANES - BIAS_COL
BLOCK_BIAS = 2.0 ** 100
LOG2E = math.log2(math.e)
MXU_DTYPE = jnp.bfloat16
VMEM_LIMIT = 52 * 1024 * 1024
DSA_WIDTH_CLASSES = 8
NSA_WIDTH_CLASSES = 8
TIE_CHUNK = 128
REDUCE_ROWS = 64
QK_SIDE_STEPS = 8

_NT = (((1,), (1,)), ((), ()))


def _alibi_slopes(n_heads):
    return [float(v) for v in np.asarray(
        2.0 ** (-8.0 * np.arange(1, n_heads + 1) / n_heads), dtype=np.float32)]


def _dot(a, b):
    return jnp.dot(a, b, preferred_element_type=jnp.float32)


def _dot_nt(a, b):
    return lax.dot_general(a, b, _NT, preferred_element_type=jnp.float32)


def _rms(x, g):
    ms = jnp.mean(x * x, axis=-1, keepdims=True)
    return x * lax.rsqrt(ms + RMS_EPS) * g


def _params(sem):
    return pltpu.CompilerParams(dimension_semantics=sem, vmem_limit_bytes=VMEM_LIMIT)


def _split3(c):
    c1 = c.astype(MXU_DTYPE).astype(jnp.float32)
    c2 = (c - c1).astype(MXU_DTYPE).astype(jnp.float32)
    c3 = (c - c1 - c2).astype(MXU_DTYPE).astype(jnp.float32)
    return c1, c2, c3


_T_KI = (0, 128)
_T_CA = (128, 256)
_T_KCV = (256, 512)
_T_KS = (512, 640)
_T_KW = (640, 768)
_T_COLS = 768
_F_QI = (0, 256)
_F_GW = (256, 384)
_F_QA = (384, 896)
_F_QB = (896, 1408)
_F_VS = (1408, 1536)
_F_VW = (1536, 1664)
_F_ROWS = 1664


def _proj_kernel(x_ref, gmix_ref, wtok_ref, wfeat_ref, glat_ref, wuk_ref, wuvt_ref, gka_ref, gks_ref, gkw_ref,
                 gqa_ref, gqb_ref,
                 ki_ref, ka_ref, vat_ref, kcv_ref, ks_ref, kw_ref, vst_ref, vwt_ref, qit_ref, gwt_ref,
                 qat_ref, qbt_ref, *, seq_len, w_idx_scale, slopes_a, slopes_b):
    f32 = jnp.float32
    x = x_ref[...]
    tm = x.shape[0]
    h = _rms(x, gmix_ref[...]).astype(MXU_DTYPE)

    def tok(c):
        return _dot(h, wtok_ref[:, c[0]:c[1]])

    def feat(r):
        return _dot_nt(wfeat_ref[r[0]:r[1], :], h)

    s0 = (pl.program_id(0) % (seq_len // tm)) * tm
    spos = s0 + lax.broadcasted_iota(jnp.int32, (tm, LANES), 0)
    lane = lax.broadcasted_iota(jnp.int32, (tm, LANES), 1)
    c1, c2, c3 = _split3(spos.astype(f32) * LOG2E)
    alibi_cols = jnp.where(lane == ALIBI_COL, c1,
                           jnp.where(lane == ALIBI_COL + 1, c2, jnp.where(lane == ALIBI_COL + 2, c3, 0.0)))
    block_onehot = jnp.where(lane - BIAS_COL == lax.shift_right_logical(spos, 6), 1.0, 0.0)

    def key_slab(raw, g):
        ms = jnp.sum(raw * raw, axis=-1, keepdims=True) * (1.0 / HEAD_DIM)
        return raw * lax.rsqrt(ms + RMS_EPS) * g

    ki_ref[...] = tok(_T_KI).astype(ki_ref.dtype)
    c = _rms(tok(_T_CA), glat_ref[...]).astype(MXU_DTYPE)
    ka_ref[...] = (key_slab(_dot(c, wuk_ref[...]), gka_ref[...]) + alibi_cols).astype(ka_ref.dtype)
    va_t = _dot_nt(wuvt_ref[...], c)
    frow = lax.broadcasted_iota(jnp.int32, va_t.shape, 0)
    vat_ref[...] = jnp.where(frow == HEAD_DIM, 1.0, va_t).astype(vat_ref.dtype)
    kcv = tok(_T_KCV)
    for j in range(2 * NSA_KV_HEADS):
        kcv_ref[j] = kcv[:, j * HEAD_DIM:(j + 1) * HEAD_DIM]
    low = lane < HEAD_DIM
    for cols, g_ref, out_ref, extra in ((_T_KS, gks_ref, ks_ref, alibi_cols + block_onehot),
                                        (_T_KW, gkw_ref, kw_ref, alibi_cols)):
        pair = tok(cols)
        for kh in range(NSA_KV_HEADS):
            raw = pair if kh == 0 else pltpu.roll(pair, HEAD_DIM, 1)
            slab = key_slab(jnp.where(low, raw, 0.0), g_ref[...]) + extra
            out_ref[:, kh * LANES:(kh + 1) * LANES] = slab.astype(out_ref.dtype)

    qit_ref[...] = feat(_F_QI).astype(qit_ref.dtype)
    gw = feat(_F_GW)
    grow = lax.broadcasted_iota(jnp.int32, gw.shape, 0)
    gwt_ref[...] = jnp.where(grow < GATE_ROWS, jax.nn.sigmoid(gw), gw * w_idx_scale)
    erow = lax.broadcasted_iota(jnp.int32, (HEAD_DIM, tm), 0)
    q_scale = HEAD_DIM ** -0.5 * LOG2E
    for rows, g_ref, out_ref, slopes in ((_F_QA, gqa_ref, qat_ref, slopes_a), (_F_QB, gqb_ref, qbt_ref, slopes_b)):
        q_all = feat(rows)
        for hd in range(len(slopes)):
            q = q_all[hd * HEAD_DIM:(hd + 1) * HEAD_DIM]
            ms = jnp.mean(q * q, axis=0, keepdims=True)
            q = q * lax.rsqrt(ms + RMS_EPS) * g_ref[...] * q_scale
            out_ref[hd * LANES:hd * LANES + HEAD_DIM, :] = q.astype(out_ref.dtype)
            extra = jnp.where(erow < 3, slopes[hd], 0.0)
            out_ref[hd * LANES + HEAD_DIM:(hd + 1) * LANES, :] = extra.astype(out_ref.dtype)
    ones_row = jnp.where(erow == 0, 1.0, 0.0)
    for rows, out_ref in ((_F_VS, vst_ref), (_F_VW, vwt_ref)):
        v_t = feat(rows)
        for kh in range(NSA_KV_HEADS):
            out_ref[kh * LANES:kh * LANES + HEAD_DIM, :] = v_t[kh * HEAD_DIM:(kh + 1) * HEAD_DIM].astype(out_ref.dtype)
            out_ref[kh * LANES + HEAD_DIM:(kh + 1) * LANES, :] = ones_row.astype(out_ref.dtype)


def _proj(x2, seq_len, g_mix, w_tok, w_feat, g_lat, w_uk, w_uv_t, gk_a, gks, gkw, gq_a, gq_b, tm):
    T, D = x2.shape
    tokm = lambda w: pl.BlockSpec((tm, w), lambda i: (i, 0))
    featm = lambda r: pl.BlockSpec((r, tm), lambda i: (0, i))
    full = lambda a: pl.BlockSpec(a.shape, lambda i: (0,) * a.ndim)
    f32 = jnp.float32
    mx = MXU_DTYPE
    out_shape = (
        jax.ShapeDtypeStruct((T, LANES), mx),
        jax.ShapeDtypeStruct((T, LANES), mx),
        jax.ShapeDtypeStruct((LANES, T), mx),
        jax.ShapeDtypeStruct((2 * NSA_KV_HEADS, T, HEAD_DIM), f32),
        jax.ShapeDtypeStruct((T, NSA_KV_HEADS * LANES), mx),
        jax.ShapeDtypeStruct((T, NSA_KV_HEADS * LANES), mx),
        jax.ShapeDtypeStruct((NSA_KV_HEADS * LANES, T), mx),
        jax.ShapeDtypeStruct((NSA_KV_HEADS * LANES, T), mx),
        jax.ShapeDtypeStruct((IDX_HEADS * IDX_DIM, T), mx),
        jax.ShapeDtypeStruct((LANES, T), f32),
        jax.ShapeDtypeStruct((DSA_HEADS * LANES, T), mx),
        jax.ShapeDtypeStruct((NSA_HEADS * LANES, T), mx),
    )
    out_specs = (
        tokm(LANES), tokm(LANES), featm(LANES),
        pl.BlockSpec((2 * NSA_KV_HEADS, tm, HEAD_DIM), lambda i: (0, i, 0)),
        tokm(NSA_KV_HEADS * LANES), tokm(NSA_KV_HEADS * LANES),
        featm(NSA_KV_HEADS * LANES), featm(NSA_KV_HEADS * LANES),
        featm(IDX_HEADS * IDX_DIM), featm(LANES), featm(DSA_HEADS * LANES), featm(NSA_HEADS * LANES),
    )
    kern = functools.partial(
        _proj_kernel, seq_len=seq_len, w_idx_scale=IDX_HEADS ** -0.5 * IDX_DIM ** -0.5,
        slopes_a=_alibi_slopes(DSA_HEADS), slopes_b=_alibi_slopes(NSA_HEADS))
    ins = (x2, g_mix, w_tok, w_feat, g_lat, w_uk, w_uv_t, gk_a, gks, gkw, gq_a, gq_b)
    return pl.pallas_call(
        kern,
        grid=(T // tm,),
        in_specs=[tokm(D)] + [full(a) for a in ins[1:]],
        out_specs=out_specs,
        out_shape=out_shape,
        compiler_params=_params(("parallel",)),
        name="proj",
    )(*ins)


def _compress_kernel(xk_ref, xv_ref, pe_ref, w1_ref, w2k_ref, w2vt_ref, gkc_ref, kc_ref, vct_ref):
    n = kc_ref.shape[0]

    def hidden(x_ref, kv):
        a = jnp.zeros((n, CMP_HIDDEN), jnp.float32)
        b = jnp.zeros((n, CMP_HIDDEN), jnp.float32)
        for l in range(CMP_STRIDE):
            x_l = x_ref[pl.ds(l, n, stride=CMP_STRIDE), :]
            for acc_is_b, row in ((False, l), (True, CMP_STRIDE + l)):
                lhs = (x_l + pe_ref[row:row + 1, :]).astype(MXU_DTYPE)
                part = _dot(lhs, w1_ref[kv, row * HEAD_DIM:(row + 1) * HEAD_DIM, :])
                if acc_is_b:
                    b = b + part
                else:
                    a = a + part
        return jax.nn.gelu(a + pltpu.roll(b, n - 1, 0)).astype(MXU_DTYPE)

    yk = _dot(hidden(xk_ref, 0), w2k_ref[...])
    ms = jnp.sum(yk * yk, axis=-1, keepdims=True) * (1.0 / HEAD_DIM)
    kc_ref[...] = (yk * lax.rsqrt(ms + RMS_EPS) * gkc_ref[...]).astype(kc_ref.dtype)
    vct_ref[...] = _dot_nt(w2vt_ref[...], hidden(xv_ref, 1)).astype(vct_ref.dtype)


def _compress(xc, pe, phi1, w2k, w2v_t, g_kc):
    _, B, S, w = xc.shape
    n = S // CMP_STRIDE
    full = lambda a: pl.BlockSpec(a.shape, lambda b, kh: (0,) * a.ndim)
    return pl.pallas_call(
        _compress_kernel,
        grid=(B, NSA_KV_HEADS),
        in_specs=[
            pl.BlockSpec((None, None, S, w), lambda b, kh: (kh, b, 0, 0)),
            pl.BlockSpec((None, None, S, w), lambda b, kh: (NSA_KV_HEADS + kh, b, 0, 0)),
            full(pe), full(phi1), full(w2k), full(w2v_t), full(g_kc),
        ],
        out_specs=(pl.BlockSpec((None, None, n, LANES), lambda b, kh: (b, kh, 0, 0)),
                   pl.BlockSpec((None, None, LANES, n), lambda b, kh: (b, kh, 0, 0))),
        out_shape=(jax.ShapeDtypeStruct((B, NSA_KV_HEADS, n, LANES), MXU_DTYPE),
                   jax.ShapeDtypeStruct((B, NSA_KV_HEADS, LANES, n), MXU_DTYPE)),
        compiler_params=_params(("parallel", "parallel")),
        name="compress",
    )(xc, xc, pe, phi1, w2k, w2v_t, g_kc)


_INT_MIN = -2 ** 31


def _key_to_float(u):
    key = u ^ jnp.int32(_INT_MIN)
    bits = jnp.where(key >= 0, key, key ^ jnp.int32(0x7FFFFFFF))
    return lax.bitcast_convert_type(bits, jnp.float32)


def _col_reduce(x, op, final):
    rows = REDUCE_ROWS
    acc = x
    if x.shape[0] > rows and x.shape[0] % rows == 0:
        acc = x[:rows]
        for j in range(1, x.shape[0] // rows):
            acc = op(acc, x[j * rows:(j + 1) * rows])
    return final(acc.astype(jnp.float32), axis=0, keepdims=True)


def _col_sum(x):
    return _col_reduce(x, jnp.add, jnp.sum)


def _col_max(x):
    return _col_reduce(x, jnp.maximum, jnp.max)


def _kth_largest(score, k, side_steps=0, side_work=None):
    def body(it, u):
        u_try = u | lax.shift_left(jnp.int32(1), 31 - it)
        cnt = _col_sum(jnp.where(score >= _key_to_float(u_try), 1.0, 0.0))
        return jnp.where(cnt >= k, u_try, u)

    def body_with_side_work(j, u):
        side_work(j)
        for i in range(32 // side_steps):
            u = body(j * (32 // side_steps) + i, u)
        return u

    u = jnp.zeros((1, score.shape[1]), jnp.int32)
    if side_steps:
        u = lax.fori_loop(0, side_steps, body_with_side_work, u)
    else:
        u = lax.fori_loop(0, 32, body, u)
    thr = _key_to_float(u)
    return jnp.where(thr != thr, -jnp.inf, thr)


def _lane_tile(x, n):
    return jnp.concatenate([x] * n, axis=1)


def _softmax_pv(parts):
    return _softmax_pv_staged([parts])[0]


def _softmax_pv_staged(problems):
    maxes = []
    for parts in problems:
        m = None
        for s, _ in parts:
            m_p = _col_max(s)
            m = m_p if m is None else jnp.maximum(m, m_p)
        maxes.append(m)
    outs = []
    for parts, m in zip(problems, maxes):
        out = None
        for s, v_t in parts:
            o_p = _dot(v_t, jnp.exp2(s - m).astype(MXU_DTYPE))
            out = o_p if out is None else out + o_p
        outs.append(out)
    return outs


def _width_classes(n_qblk, n_classes):
    n_cls = n_classes if n_qblk % n_classes == 0 else 1
    per = n_qblk // n_cls
    return [(c * per, (c + 1) * per, (c + 1) * per * Q_BLOCK) for c in range(n_cls)]


def _dsa_body(W, qit_ref, gwt_ref, qat_ref, ki_ref, ka_ref, vat_ref, o_ref, sel_ref, qk_ref,*, topk):
    f32 = jnp.float32
    QB = Q_BLOCK
    t = pl.program_id(1) * QB + lax.broadcasted_iota(jnp.int32, (1, QB), 1)
    spos = lax.broadcasted_iota(jnp.int32, (W, QB), 0)
    causal = spos <= t

    ki = ki_ref[:W, :]
    G = 4
    kpad = jnp.zeros((LANES - IDX_DIM, G * QB), MXU_DTYPE)
    score = jnp.zeros((W, QB), f32)
    for hg in range(IDX_HEADS // G):
        q_idx = jnp.concatenate(
            [qit_ref[(hg * G + g) * IDX_DIM:(hg * G + g + 1) * IDX_DIM, :] for g in range(G)], axis=1)
        logits = _dot(ki, jnp.concatenate([q_idx, kpad], axis=0))
        for g in range(G):
            w_h = gwt_ref[GATE_ROWS + hg * G + g:GATE_ROWS + hg * G + g + 1, :]
            score = score + jnp.maximum(logits[:, g * QB:(g + 1) * QB], 0.0) * w_h
    score = jnp.where(causal, score, -jnp.inf)

    q_all = jnp.concatenate([qat_ref[h * LANES:(h + 1) * LANES, :] for h in range(DSA_HEADS)], axis=1)
    chunk = W // QK_SIDE_STEPS

    def qk_chunk(it):
        r0 = pl.multiple_of(it * chunk, chunk)
        qk_ref[pl.ds(r0, chunk), :] = _dot(ka_ref[pl.ds(r0, chunk), :], q_all)

    thr = _kth_largest(score, float(topk), QK_SIDE_STEPS, qk_chunk)
    thr = jnp.maximum(thr, float(jnp.finfo(jnp.float32).min))
    ge = score >= thr
    sel_ref[:W, :] = jnp.where(ge, 0.0, NEG_INF)

    @pl.when(jnp.max(_col_sum(jnp.where(ge, 1.0, 0.0))) > float(topk))
    def _():
        gt = score > thr
        eq = score == thr
        need = float(topk) - _col_sum(jnp.where(gt, 1.0, 0.0))
        cw = TIE_CHUNK
        r = lax.broadcasted_iota(jnp.int32, (cw, cw), 0)
        c = lax.broadcasted_iota(jnp.int32, (cw, cw), 1)
        lower = jnp.where(c <= r, 1.0, 0.0).astype(MXU_DTYPE)
        carry = jnp.zeros((1, QB), f32)
        for j in range(W // cw):
            sl = slice(j * cw, (j + 1) * cw)
            eq_j = jnp.where(eq[sl], 1.0, 0.0)
            prefix = _dot(lower, eq_j.astype(MXU_DTYPE)) + carry
            keep = gt[sl] | (eq[sl] & (prefix <= need))
            sel_ref[sl, :] = jnp.where(keep, 0.0, NEG_INF)
            carry = carry + jnp.sum(eq_j, axis=0, keepdims=True)

    G = 4
    va_t = vat_ref[:, :W]
    for hg in range(DSA_HEADS // G):
        s = qk_ref[:W, hg * G * QB:(hg + 1) * G * QB] + _lane_tile(sel_ref[:W, :], G)
        o_t = _softmax_pv([(s, va_t)])
        o_t = o_t[:HEAD_DIM] * (1.0 / o_t[HEAD_DIM:HEAD_DIM + 1])
        for g2 in range(G // 2):
            pair = jnp.concatenate([o_t[:, (2 * g2) * QB:(2 * g2 + 1) * QB],
                                    o_t[:, (2 * g2 + 1) * QB:(2 * g2 + 2) * QB]], axis=0)
            col = (hg * G + 2 * g2) * HEAD_DIM
            o_ref[:, col:col + 2 * HEAD_DIM] = pair.T.astype(o_ref.dtype)


def _dsa_kernel(qit_ref, gwt_ref, qat_ref, ki_ref, ka_ref, vat_ref, o_ref, sel_ref, qk_ref,*, topk, classes):
    qblk = pl.program_id(1)
    for lo, hi, W in classes:
        @pl.when((qblk >= lo) & (qblk < hi))
        def _(W=W):
            _dsa_body(W, qit_ref, gwt_ref, qat_ref, ki_ref, ka_ref, vat_ref, o_ref, sel_ref, qk_ref,topk=topk)


def _dsa(B, S, qi_t, gw_t, qa_t, ki, ka, va_t):
    nq = S // Q_BLOCK
    topk = min(DSA_TOPK_MAX, S // 4)
    fblk = lambda r: pl.BlockSpec((r, Q_BLOCK), lambda b, i: (0, b * nq + i))
    tseq = lambda w: pl.BlockSpec((S, w), lambda b, i: (b, 0))
    fseq = lambda r: pl.BlockSpec((r, S), lambda b, i: (0, b))
    return pl.pallas_call(
        functools.partial(_dsa_kernel, topk=topk, classes=_width_classes(nq, DSA_WIDTH_CLASSES)),
        grid=(B, nq),
        in_specs=[fblk(IDX_HEADS * IDX_DIM), fblk(LANES), fblk(DSA_HEADS * LANES),
                  tseq(LANES), tseq(LANES), fseq(LANES)],
        out_specs=pl.BlockSpec((Q_BLOCK, WIDTH_A), lambda b, i: (b * nq + i, 0)),
        out_shape=jax.ShapeDtypeStruct((B * S, WIDTH_A), MXU_DTYPE),
        scratch_shapes=[pltpu.VMEM((S, Q_BLOCK), jnp.float32),
                        pltpu.VMEM((S, DSA_HEADS * Q_BLOCK), jnp.float32)],
        compiler_params=_params(("parallel", "arbitrary")),
        name="dsa",
    )(qi_t, gw_t, qa_t, ki, ka, va_t)


def _nsa_body(W, first_blk, gwt_ref, qbt_ref, ks_ref, kw_ref, vst_ref, vwt_ref, kc_ref, vct_ref, ovl_ref, o_ref,
              *, seq_len):
    f32 = jnp.float32
    QB, G = Q_BLOCK, NSA_GROUP
    n_slc = seq_len // SLC_LEN
    n_top = min(SLC_TOPN, n_slc)
    win_len = WINDOW + QB
    nc = kc_ref.shape[1]

    t0 = pl.program_id(1) * QB
    t = t0 + lax.broadcasted_iota(jnp.int32, (1, QB), 1)
    t4 = _lane_tile(t, G)
    valid_c = lax.broadcasted_iota(jnp.int32, (nc, G * QB), 0) * CMP_STRIDE + (CMP_LEN - 1) <= t4
    past = first_blk * QB
    causal_tail = past + lax.broadcasted_iota(jnp.int32, (W - past, G * QB), 0) <= t4
    w0 = pl.multiple_of(jnp.maximum(t0 - WINDOW, 0), QB)
    full_window = past >= WINDOW
    if full_window:
        r_minus_lane = lax.broadcasted_iota(jnp.int32, (QB, G * QB), 0) - (t4 - t0)
        w_head_ok = r_minus_lane > 0
        w_tail_ok = r_minus_lane <= 0
    else:
        wd = t4 - (w0 + lax.broadcasted_iota(jnp.int32, (win_len, G * QB), 0))
        wvalid = (wd >= 0) & (wd < WINDOW)
    jblk = lax.broadcasted_iota(jnp.int32, (MAX_SLC_BLOCKS, QB), 0)
    cur = lax.shift_right_logical(t, 6)
    future = jblk > cur
    forced = (jblk == 0) | (jblk == cur) | (jblk == cur - 1)
    ovl = ovl_ref[...]

    KH = range(NSA_KV_HEADS)
    ksl = [slice(kh * LANES, (kh + 1) * LANES) for kh in KH]
    q = [jnp.concatenate([qbt_ref[(kh * G + g) * LANES:(kh * G + g + 1) * LANES, :] for g in range(G)], axis=1)
         for kh in KH]

    s_c = [jnp.where(valid_c, _dot(kc_ref[kh], q[kh]), NEG_INF) for kh in KH]
    s_w = [_dot(kw_ref[pl.ds(w0, win_len), ksl[kh]], q[kh]) for kh in KH]
    e_c = [jnp.where(valid_c, jnp.exp2(s_c[kh] - jnp.max(s_c[kh], axis=0, keepdims=True)), 0.0) for kh in KH]
    l_c = [jnp.sum(e_c[kh], axis=0, keepdims=True) for kh in KH]
    p_c = [e_c[kh] * (1.0 / jnp.where(l_c[kh] > 0.0, l_c[kh], 1.0)) for kh in KH]
    o_cmp = [_dot(vct_ref[kh], p_c[kh].astype(MXU_DTYPE)) for kh in KH]

    val = []
    for kh in KH:
        pc = p_c[kh]
        p_sum = pc[:, 0:QB] + pc[:, QB:2 * QB] + pc[:, 2 * QB:3 * QB] + pc[:, 3 * QB:4 * QB]
        p_hi = p_sum.astype(MXU_DTYPE)
        p_lo = (p_sum - p_hi.astype(f32)).astype(MXU_DTYPE)
        imp = _dot(ovl, p_hi) + _dot(ovl, p_lo)
        val.append(jnp.where(future, -jnp.inf, imp + jnp.where(forced, FORCE_BONUS, 0.0)))

    win_problems = []
    for kh in KH:
        v_w = vwt_ref[ksl[kh], pl.ds(w0, win_len)]
        if full_window:
            parts = [(jnp.where(w_head_ok, s_w[kh][:QB], NEG_INF), v_w[:, :QB]),
                     (s_w[kh][QB:WINDOW], v_w[:, QB:WINDOW]),
                     (jnp.where(w_tail_ok, s_w[kh][WINDOW:], NEG_INF), v_w[:, WINDOW:])]
        else:
            parts = [(jnp.where(wvalid, s_w[kh], NEG_INF), v_w)]
        win_problems.append(parts)
    o_win = _softmax_pv_staged(win_problems)

    rank = [jnp.zeros(val[kh].shape, f32) for kh in KH]
    for i in range(n_slc):
        for kh in KH:
            vi = val[kh][i:i + 1, :]
            rank[kh] = rank[kh] + jnp.where((vi > val[kh]) | ((vi == val[kh]) & (jblk > i)), 1.0, 0.0)

    s_s = []
    for kh in KH:
        bias = jnp.where((rank[kh] < n_top) & (jblk < n_slc), 0.0, -BLOCK_BIAS).astype(MXU_DTYPE)
        q_sel = jnp.concatenate([q[kh][:BIAS_COL], _lane_tile(bias, G)], axis=0)
        s_s.append(_dot(ks_ref[:W, ksl[kh]], q_sel))
    slc_problems = []
    for kh in KH:
        parts = [(jnp.where(causal_tail, s_s[kh][past:], NEG_INF), vst_ref[ksl[kh], past:W])]
        if past:
            parts.append((s_s[kh][:past], vst_ref[ksl[kh], :past]))
        slc_problems.append(parts)
    o_slc = _softmax_pv_staged(slc_problems)

    for kh in KH:
        inv_s = 1.0 / o_slc[kh][HEAD_DIM:HEAD_DIM + 1]
        inv_w = 1.0 / o_win[kh][HEAD_DIM:HEAD_DIM + 1]
        heads = []
        for g in range(G):
            h = kh * G + g
            ls = slice(g * QB, (g + 1) * QB)
            g0 = gwt_ref[3 * h + 0:3 * h + 1, :]
            g1 = gwt_ref[3 * h + 1:3 * h + 2, :] * inv_s[:, ls]
            g2 = gwt_ref[3 * h + 2:3 * h + 3, :] * inv_w[:, ls]
            heads.append(g0 * o_cmp[kh][:HEAD_DIM, ls] + g1 * o_slc[kh][:HEAD_DIM, ls]
                         + g2 * o_win[kh][:HEAD_DIM, ls])
        for g2_ in range(G // 2):
            pair = jnp.concatenate([heads[2 * g2_], heads[2 * g2_ + 1]], axis=0)
            col = (kh * G + 2 * g2_) * HEAD_DIM
            o_ref[:, col:col + 2 * HEAD_DIM] = pair.T.astype(o_ref.dtype)


def _nsa_kernel(gwt_ref, qbt_ref, ks_ref, kw_ref, vst_ref, vwt_ref, kc_ref, vct_ref, ovl_ref, o_ref,
                *, seq_len, classes):
    qblk = pl.program_id(1)
    for lo, hi, W in classes:
        @pl.when((qblk >= lo) & (qblk < hi))
        def _(W=W, lo=lo):
            _nsa_body(W, lo, gwt_ref, qbt_ref, ks_ref, kw_ref, vst_ref, vwt_ref, kc_ref, vct_ref, ovl_ref, o_ref,
                      seq_len=seq_len)


def _nsa(B, S, gw_t, qb_t, ks, kw, vs_t, vw_t, kc, vc_t, ovl):
    nq = S // Q_BLOCK
    nc = kc.shape[2]
    kvw = NSA_KV_HEADS * LANES
    fblk = lambda r: pl.BlockSpec((r, Q_BLOCK), lambda b, i: (0, b * nq + i))
    tseq = lambda w: pl.BlockSpec((S, w), lambda b, i: (b, 0))
    fseq = lambda r: pl.BlockSpec((r, S), lambda b, i: (0, b))
    return pl.pallas_call(
        functools.partial(_nsa_kernel, seq_len=S, classes=_width_classes(nq, NSA_WIDTH_CLASSES)),
        grid=(B, nq),
        in_specs=[fblk(LANES), fblk(NSA_HEADS * LANES), tseq(kvw), tseq(kvw), fseq(kvw), fseq(kvw),
                  pl.BlockSpec((None, NSA_KV_HEADS, nc, LANES), lambda b, i: (b, 0, 0, 0)),
                  pl.BlockSpec((None, NSA_KV_HEADS, LANES, nc), lambda b, i: (b, 0, 0, 0)),
                  pl.BlockSpec(ovl.shape, lambda b, i: (0, 0))],
        out_specs=pl.BlockSpec((Q_BLOCK, WIDTH_B), lambda b, i: (b * nq + i, 0)),
        out_shape=jax.ShapeDtypeStruct((B * S, WIDTH_B), MXU_DTYPE),
        compiler_params=_params(("parallel", "arbitrary")),
        name="nsa",
    )(gw_t, qb_t, ks, kw, vs_t, vw_t, kc, vc_t, ovl)


def _merge_ffn_kernel(x_ref, oa_ref, ob_ref, gmix_ref, wg_ref, wua_ref, wub_ref, wo_ref,
                      gmlp_ref, w1_ref, w2_ref, o_ref, *, chunk):
    x = x_ref[...]
    D = x.shape[1]
    h = _rms(x, gmix_ref[...]).astype(MXU_DTYPE)
    ga = jax.nn.sigmoid(_dot(h, wg_ref[:, :D]))
    gb = jax.nn.sigmoid(_dot(h, wg_ref[:, D:]))
    merged = ga * _dot(oa_ref[...], wua_ref[...]) + gb * _dot(ob_ref[...], wub_ref[...])
    x1 = x + _dot(merged.astype(MXU_DTYPE), wo_ref[...])
    h2 = _rms(x1, gmlp_ref[...]).astype(MXU_DTYPE)
    acc = x1
    for c in range(w1_ref.shape[1] // chunk):
        u = jnp.maximum(_dot(h2, w1_ref[:, c * chunk:(c + 1) * chunk]), 0.0)
        acc = acc + _dot((u * u).astype(MXU_DTYPE), w2_ref[c * chunk:(c + 1) * chunk, :])
    o_ref[...] = acc


def _merge_ffn(x2, oa, ob, g_mix, w_gate, w_up_a, w_up_b, w_out, g_mlp, w1, w2, tm):
    T, D = x2.shape
    row = lambda w: pl.BlockSpec((tm, w), lambda i: (i, 0))
    full = lambda a: pl.BlockSpec(a.shape, lambda i: (0,) * a.ndim, pipeline_mode=pl.Buffered(1))
    ins = (x2, oa, ob, g_mix, w_gate, w_up_a, w_up_b, w_out, g_mlp, w1, w2)
    return pl.pallas_call(
        functools.partial(_merge_ffn_kernel, chunk=1024),
        grid=(T // tm,),
        in_specs=[row(D), row(WIDTH_A), row(WIDTH_B)] + [full(a) for a in ins[3:]],
        out_specs=row(D),
        out_shape=jax.ShapeDtypeStruct((T, D), jnp.float32),
        compiler_params=_params(("parallel",)),
        name="merge_ffn",
    )(*ins)


def _layer(x, g_mix, w_in, g_q_a, g_k_a, g_lat_a, w_uk_a, w_uv_a, g_q_b, g_kc_b, g_ks_b, g_kw_b,
           pe_cmp_b, phi_k1_b, phi_k2_b, phi_v1_b, phi_v2_b, w_up_a, w_up_b, w_out, g_mlp, w_ff1, w_ff2):
    B, S, D = x.shape
    T = B * S
    f32 = jnp.float32
    mx = MXU_DTYPE
    n_slc = S // SLC_LEN
    assert n_slc <= MAX_SLC_BLOCKS and S % (DSA_WIDTH_CLASSES * TIE_CHUNK) == 0 and S >= WINDOW + Q_BLOCK
    col_sizes = (WIDTH_A, DSA_LATENT, IDX_HEADS * IDX_DIM, IDX_DIM, IDX_HEADS,
                 WIDTH_B, 6 * NSA_KV_W, 3 * NSA_HEADS, 2 * D)
    offs = np.cumsum((0,) + col_sizes)
    w_qa, w_ca, w_qi, w_ki, w_wi, w_qb, w_kvb, w_gb, w_gm = [
        w_in[:, offs[i]:offs[i + 1]] for i in range(len(col_sizes))]
    w_kc, w_vc, w_ks, w_vs, w_kw, w_vw = [w_kvb[:, j * NSA_KV_W:(j + 1) * NSA_KV_W] for j in range(6)]
    w_tok = jnp.concatenate(
        [w_ki, jnp.zeros((D, LANES - IDX_DIM), f32), w_ca, w_kc, w_vc, w_ks, w_kw], axis=1).astype(mx)
    assert w_tok.shape[1] == _T_COLS
    gw_pad = jnp.zeros((D, LANES - GATE_ROWS - IDX_HEADS), f32)
    w_feat = jnp.concatenate(
        [w_qi, w_gb, w_wi, gw_pad, w_qa, w_qb, w_vs, w_vw], axis=1).T.astype(mx)
    assert w_feat.shape[0] == _F_ROWS
    w_gate = w_gm.astype(mx)
    zpad = jnp.zeros((DSA_LATENT, LANES - HEAD_DIM), f32)
    w_uk = jnp.concatenate([w_uk_a, zpad], axis=1).astype(mx)
    w_uv_t = jnp.concatenate([w_uv_a, zpad], axis=1).T.astype(mx)
    row = lambda g: g.reshape(1, -1).astype(f32)
    rpad = lambda g: jnp.concatenate([row(g), jnp.zeros((1, LANES - HEAD_DIM), f32)], axis=1)
    colv = lambda g: g.reshape(-1, 1).astype(f32)

    x2 = x.reshape(T, D)
    (ki, ka, va_t, kcv, ks, kw, vs_t, vw_t, qi_t, gw_t, qa_t, qb_t) = _proj(
        x2, S, row(g_mix), w_tok, w_feat, row(g_lat_a), w_uk, w_uv_t, rpad(g_k_a), rpad(g_ks_b), rpad(g_kw_b),
        colv(g_q_a), colv(g_q_b), tm=512)

    n_chunk = S // CMP_STRIDE
    n_cmp = (S - CMP_LEN) // CMP_STRIDE + 1
    xc = kcv.reshape(2 * NSA_KV_HEADS, B, S, HEAD_DIM)
    phi1 = jnp.stack([phi_k1_b, phi_v1_b]).astype(mx)
    hpad = jnp.zeros((CMP_HIDDEN, LANES - HEAD_DIM), f32)
    w2k = jnp.concatenate([phi_k2_b, hpad], axis=1).astype(mx)
    w2v_t = jnp.concatenate([phi_v2_b, hpad], axis=1).T.astype(mx)
    kc, vc_t = _compress(xc, pe_cmp_b.astype(f32), phi1, w2k, w2v_t, rpad(g_kc_b))

    c_i = np.arange(n_chunk)[None, :] * CMP_STRIDE
    j_i = np.arange(MAX_SLC_BLOCKS)[:, None] * SLC_LEN
    ovl = (c_i < j_i + SLC_LEN) & (c_i + CMP_LEN > j_i) & (c_i < n_cmp * CMP_STRIDE) & (j_i < S)
    ovl = jnp.asarray(ovl, f32).astype(mx)

    o_a = _dsa(B, S, qi_t, gw_t, qa_t, ki, ka, va_t)
    o_b = _nsa(B, S, gw_t, qb_t, ks, kw, vs_t, vw_t, kc, vc_t, ovl)

    out = _merge_ffn(x2, o_a, o_b, row(g_mix), w_gate, w_up_a.astype(mx), w_up_b.astype(mx), w_out.astype(mx),
                     row(g_mlp), w_ff1.astype(mx), w_ff2.astype(mx), tm=512)
    return out.reshape(B, S, D)


def kernel(x, g_mix, w_in, g_q_a, g_k_a, g_lat_a, w_uk_a, w_uv_a, g_q_b, g_kc_b, g_ks_b, g_kw_b, pe_cmp_b,
           phi_k1_b, phi_k2_b, phi_v1_b, phi_v2_b, w_up_a, w_up_b, w_out, g_mlp, w_ff1, w_ff2):
    params = (g_mix, w_in, g_q_a, g_k_a, g_lat_a, w_uk_a, w_uv_a, g_q_b, g_kc_b, g_ks_b, g_kw_b, pe_cmp_b,
              phi_k1_b, phi_k2_b, phi_v1_b, phi_v2_b, w_up_a, w_up_b, w_out, g_mlp, w_ff1, w_ff2)
    for l in range(g_mix.shape[0]):
        x = _layer(x, *[p[l] for p in params])
    return x
```

```python
import functools
import math

import numpy as np
import jax
import jax.numpy as jnp
from jax import lax
from jax.experimental import pallas as pl
from jax.experimental.pallas import tpu as pltpu

HEAD_DIM = 64
DSA_HEADS = 8
DSA_LATENT = 128
IDX_HEADS = 8
IDX_DIM = 32
DSA_TOPK_MAX = 256
NSA_HEADS = 8
NSA_KV_HEADS = 2
NSA_GROUP = NSA_HEADS // NSA_KV_HEADS
CMP_LEN = 32
CMP_STRIDE = 16
CMP_HIDDEN = 128
SLC_LEN = 64
SLC_TOPN = 16
WINDOW = 512
FORCE_BONUS = 1e6
Q_BLOCK = 128
RMS_EPS = 1e-6
NEG_INF = -1e30

WIDTH_A = DSA_HEADS * HEAD_DIM
WIDTH_B = NSA_HEADS * HEAD_DIM
NSA_KV_W = NSA_KV_HEADS * HEAD_DIM

LANES = 128
GATE_ROWS = 3 * NSA_HEADS
ALIBI_COL = HEAD_DIM
BIAS_COL = 96
MAX_SLC_BLOCKS = LANES - BIAS_COL
BLOCK_BIAS = 2.0 ** 100
LOG2E = math.log2(math.e)
MXU_DTYPE = jnp.bfloat16
VMEM_LIMIT = 52 * 1024 * 1024
DSA_WIDTH_CLASSES = 8
NSA_WIDTH_CLASSES = 8
TIE_CHUNK = 128
REDUCE_ROWS = 64
QK_SIDE_STEPS = 8

_NT = (((1,), (1,)), ((), ()))


def _alibi_slopes(n_heads):
    return [float(v) for v in np.asarray(
        2.0 ** (-8.0 * np.arange(1, n_heads + 1) / n_heads), dtype=np.float32)]


def _dot(a, b):
    return jnp.dot(a, b, preferred_element_type=jnp.float32)


def _dot_nt(a, b):
    return lax.dot_general(a, b, _NT, preferred_element_type=jnp.float32)


def _rms(x, g):
    ms = jnp.mean(x * x, axis=-1, keepdims=True)
    return x * lax.rsqrt(ms + RMS_EPS) * g


def _params(sem):
    return pltpu.CompilerParams(dimension_semantics=sem, vmem_limit_bytes=VMEM_LIMIT)


def _split3(c):
    c1 = c.astype(MXU_DTYPE).astype(jnp.float32)
    c2 = (c - c1).astype(MXU_DTYPE).astype(jnp.float32)
    c3 = (c - c1 - c2).astype(MXU_DTYPE).astype(jnp.float32)
    return c1, c2, c3


_T_KI = (0, 128)
_T_CA = (128, 256)
_T_KCV = (256, 512)
_T_KS = (512, 640)
_T_KW = (640, 768)
_T_COLS = 768
_F_QI = (0, 256)
_F_GW = (256, 384)
_F_QA = (384, 896)
_F_QB = (896, 1408)
_F_VS = (1408, 1536)
_F_VW = (1536, 1664)
_F_ROWS = 1664


def _proj_kernel(x_ref, gmix_ref, wtok_ref, wfeat_ref, glat_ref, wuk_ref, wuvt_ref, gka_ref, gks_ref, gkw_ref,
                 gqa_ref, gqb_ref,
                 ki_ref, ka_ref, vat_ref, kcv_ref, ks_ref, kw_ref, vst_ref, vwt_ref, qit_ref, gwt_ref,
                 qat_ref, qbt_ref, *, seq_len, w_idx_scale, slopes_a, slopes_b):
    f32 = jnp.float32
    x = x_ref[...]
    tm = x.shape[0]
    h = _rms(x, gmix_ref[...]).astype(MXU_DTYPE)

    def tok(c):
        return _dot(h, wtok_ref[:, c[0]:c[1]])

    def feat(r):
        return _dot_nt(wfeat_ref[r[0]:r[1], :], h)

    s0 = (pl.program_id(0) % (seq_len // tm)) * tm
    spos = s0 + lax.broadcasted_iota(jnp.int32, (tm, LANES), 0)
    lane = lax.broadcasted_iota(jnp.int32, (tm, LANES), 1)
    c1, c2, c3 = _split3(spos.astype(f32) * LOG2E)
    alibi_cols = jnp.where(lane == ALIBI_COL, c1,
                           jnp.where(lane == ALIBI_COL + 1, c2, jnp.where(lane == ALIBI_COL + 2, c3, 0.0)))
    block_onehot = jnp.where(lane - BIAS_COL == lax.shift_right_logical(spos, 6), 1.0, 0.0)

    def key_slab(raw, g):
        ms = jnp.sum(raw * raw, axis=-1, keepdims=True) * (1.0 / HEAD_DIM)
        return raw * lax.rsqrt(ms + RMS_EPS) * g

    ki_ref[...] = tok(_T_KI).astype(ki_ref.dtype)
    c = _rms(tok(_T_CA), glat_ref[...]).astype(MXU_DTYPE)
    ka_ref[...] = (key_slab(_dot(c, wuk_ref[...]), gka_ref[...]) + alibi_cols).astype(ka_ref.dtype)
    va_t = _dot_nt(wuvt_ref[...], c)
    frow = lax.broadcasted_iota(jnp.int32, va_t.shape, 0)
    vat_ref[...] = jnp.where(frow == HEAD_DIM, 1.0, va_t).astype(vat_ref.dtype)
    kcv = tok(_T_KCV)
    for j in range(2 * NSA_KV_HEADS):
        kcv_ref[j] = kcv[:, j * HEAD_DIM:(j + 1) * HEAD_DIM]
    low = lane < HEAD_DIM
    for cols, g_ref, out_ref, extra in ((_T_KS, gks_ref, ks_ref, alibi_cols + block_onehot),
                                        (_T_KW, gkw_ref, kw_ref, alibi_cols)):
        pair = tok(cols)
        for kh in range(NSA_KV_HEADS):
            raw = pair if kh == 0 else pltpu.roll(pair, HEAD_DIM, 1)
            slab = key_slab(jnp.where(low, raw, 0.0), g_ref[...]) + extra
            out_ref[:, kh * LANES:(kh + 1) * LANES] = slab.astype(out_ref.dtype)

    qit_ref[...] = feat(_F_QI).astype(qit_ref.dtype)
    gw = feat(_F_GW)
    grow = lax.broadcasted_iota(jnp.int32, gw.shape, 0)
    gwt_ref[...] = jnp.where(grow < GATE_ROWS, jax.nn.sigmoid(gw), gw * w_idx_scale)
    erow = lax.broadcasted_iota(jnp.int32, (HEAD_DIM, tm), 0)
    q_scale = HEAD_DIM ** -0.5 * LOG2E
    for rows, g_ref, out_ref, slopes in ((_F_QA, gqa_ref, qat_ref, slopes_a), (_F_QB, gqb_ref, qbt_ref, slopes_b)):
        q_all = feat(rows)
        for hd in range(len(slopes)):
            q = q_all[hd * HEAD_DIM:(hd + 1) * HEAD_DIM]
            ms = jnp.mean(q * q, axis=0, keepdims=True)
            q = q * lax.rsqrt(ms + RMS_EPS) * g_ref[...] * q_scale
            out_ref[hd * LANES:hd * LANES + HEAD_DIM, :] = q.astype(out_ref.dtype)
            extra = jnp.where(erow < 3, slopes[hd], 0.0)
            out_ref[hd * LANES + HEAD_DIM:(hd + 1) * LANES, :] = extra.astype(out_ref.dtype)
    ones_row = jnp.where(erow == 0, 1.0, 0.0)
    for rows, out_ref in ((_F_VS, vst_ref), (_F_VW, vwt_ref)):
        v_t = feat(rows)
        for kh in range(NSA_KV_HEADS):
            out_ref[kh * LANES:kh * LANES + HEAD_DIM, :] = v_t[kh * HEAD_DIM:(kh + 1) * HEAD_DIM].astype(out_ref.dtype)
            out_ref[kh * LANES + HEAD_DIM:(kh + 1) * LANES, :] = ones_row.astype(out_ref.dtype)


def _proj(x2, seq_len, g_mix, w_tok, w_feat, g_lat, w_uk, w_uv_t, gk_a, gks, gkw, gq_a, gq_b, tm):
    T, D = x2.shape
    tokm = lambda w: pl.BlockSpec((tm, w), lambda i: (i, 0))
    featm = lambda r: pl.BlockSpec((r, tm), lambda i: (0, i))
    full = lambda a: pl.BlockSpec(a.shape, lambda i: (0,) * a.ndim)
    f32 = jnp.float32
    mx = MXU_DTYPE
    out_shape = (
        jax.ShapeDtypeStruct((T, LANES), mx),
        jax.ShapeDtypeStruct((T, LANES), mx),
        jax.ShapeDtypeStruct((LANES, T), mx),
        jax.ShapeDtypeStruct((2 * NSA_KV_HEADS, T, HEAD_DIM), f32),
        jax.ShapeDtypeStruct((T, NSA_KV_HEADS * LANES), mx),
        jax.ShapeDtypeStruct((T, NSA_KV_HEADS * LANES), mx),
        jax.ShapeDtypeStruct((NSA_KV_HEADS * LANES, T), mx),
        jax.ShapeDtypeStruct((NSA_KV_HEADS * LANES, T), mx),
        jax.ShapeDtypeStruct((IDX_HEADS * IDX_DIM, T), mx),
        jax.ShapeDtypeStruct((LANES, T), f32),
        jax.ShapeDtypeStruct((DSA_HEADS * LANES, T), mx),
        jax.ShapeDtypeStruct((NSA_HEADS * LANES, T), mx),
    )
    out_specs = (
        tokm(LANES), tokm(LANES), featm(LANES),
        pl.BlockSpec((2 * NSA_KV_HEADS, tm, HEAD_DIM), lambda i: (0, i, 0)),
        tokm(NSA_KV_HEADS * LANES), tokm(NSA_KV_HEADS * LANES),
        featm(NSA_KV_HEADS * LANES), featm(NSA_KV_HEADS * LANES),
        featm(IDX_HEADS * IDX_DIM), featm(LANES), featm(DSA_HEADS * LANES), featm(NSA_HEADS * LANES),
    )
    kern = functools.partial(
        _proj_kernel, seq_len=seq_len, w_idx_scale=IDX_HEADS ** -0.5 * IDX_DIM ** -0.5,
        slopes_a=_alibi_slopes(DSA_HEADS), slopes_b=_alibi_slopes(NSA_HEADS))
    ins = (x2, g_mix, w_tok, w_feat, g_lat, w_uk, w_uv_t, gk_a, gks, gkw, gq_a, gq_b)
    return pl.pallas_call(
        kern,
        grid=(T // tm,),
        in_specs=[tokm(D)] + [full(a) for a in ins[1:]],
        out_specs=out_specs,
        out_shape=out_shape,
        compiler_params=_params(("parallel",)),
        name="proj",
    )(*ins)


def _compress_kernel(xk_ref, xv_ref, pe_ref, w1_ref, w2k_ref, w2vt_ref, gkc_ref, kc_ref, vct_ref):
    n = kc_ref.shape[0]

    def hidden(x_ref, kv):
        a = jnp.zeros((n, CMP_HIDDEN), jnp.float32)
        b = jnp.zeros((n, CMP_HIDDEN), jnp.float32)
        for l in range(CMP_STRIDE):
            x_l = x_ref[pl.ds(l, n, stride=CMP_STRIDE), :]
            for acc_is_b, row in ((False, l), (True, CMP_STRIDE + l)):
                lhs = (x_l + pe_ref[row:row + 1, :]).astype(MXU_DTYPE)
                part = _dot(lhs, w1_ref[kv, row * HEAD_DIM:(row + 1) * HEAD_DIM, :])
                if acc_is_b:
                    b = b + part
                else:
                    a = a + part
        return jax.nn.gelu(a + pltpu.roll(b, n - 1, 0)).astype(MXU_DTYPE)

    yk = _dot(hidden(xk_ref, 0), w2k_ref[...])
    ms = jnp.sum(yk * yk, axis=-1, keepdims=True) * (1.0 / HEAD_DIM)
    kc_ref[...] = (yk * lax.rsqrt(ms + RMS_EPS) * gkc_ref[...]).astype(kc_ref.dtype)
    vct_ref[...] = _dot_nt(w2vt_ref[...], hidden(xv_ref, 1)).astype(vct_ref.dtype)


def _compress(xc, pe, phi1, w2k, w2v_t, g_kc):
    _, B, S, w = xc.shape
    n = S // CMP_STRIDE
    full = lambda a: pl.BlockSpec(a.shape, lambda b, kh: (0,) * a.ndim)
    return pl.pallas_call(
        _compress_kernel,
        grid=(B, NSA_KV_HEADS),
        in_specs=[
            pl.BlockSpec((None, None, S, w), lambda b, kh: (kh, b, 0, 0)),
            pl.BlockSpec((None, None, S, w), lambda b, kh: (NSA_KV_HEADS + kh, b, 0, 0)),
            full(pe), full(phi1), full(w2k), full(w2v_t), full(g_kc),
        ],
        out_specs=(pl.BlockSpec((None, None, n, LANES), lambda b, kh: (b, kh, 0, 0)),
                   pl.BlockSpec((None, None, LANES, n), lambda b, kh: (b, kh, 0, 0))),
        out_shape=(jax.ShapeDtypeStruct((B, NSA_KV_HEADS, n, LANES), MXU_DTYPE),
                   jax.ShapeDtypeStruct((B, NSA_KV_HEADS, LANES, n), MXU_DTYPE)),
        compiler_params=_params(("parallel", "parallel")),
        name="compress",
    )(xc, xc, pe, phi1, w2k, w2v_t, g_kc)


_INT_MIN = -2 ** 31


def _key_to_float(u):
    key = u ^ jnp.int32(_INT_MIN)
    bits = jnp.where(key >= 0, key, key ^ jnp.int32(0x7FFFFFFF))
    return lax.bitcast_convert_type(bits, jnp.float32)


def _col_reduce(x, op, final):
    rows = REDUCE_ROWS
    acc = x
    if x.shape[0] > rows and x.shape[0] % rows == 0:
        acc = x[:rows]
        for j in range(1, x.shape[0] // rows):
            acc = op(acc, x[j * rows:(j + 1) * rows])
    return final(acc.astype(jnp.float32), axis=0, keepdims=True)


def _col_sum(x):
    return _col_reduce(x, jnp.add, jnp.sum)


def _col_max(x):
    return _col_reduce(x, jnp.maximum, jnp.max)


def _kth_largest(score, k, side_steps=0, side_work=None):
    def body(it, u):
        u_try = u | lax.shift_left(jnp.int32(1), 31 - it)
        cnt = _col_sum(jnp.where(score >= _key_to_float(u_try), 1.0, 0.0))
        return jnp.where(cnt >= k, u_try, u)

    def body_with_side_work(j, u):
        side_work(j)
        for i in range(32 // side_steps):
            u = body(j * (32 // side_steps) + i, u)
        return u

    u = jnp.zeros((1, score.shape[1]), jnp.int32)
    if side_steps:
        u = lax.fori_loop(0, side_steps, body_with_side_work, u)
    else:
        u = lax.fori_loop(0, 32, body, u)
    thr = _key_to_float(u)
    return jnp.where(thr != thr, -jnp.inf, thr)


def _lane_tile(x, n):
    return jnp.concatenate([x] * n, axis=1)


def _softmax_pv(parts):
    return _softmax_pv_staged([parts])[0]


def _softmax_pv_staged(problems):
    maxes = []
    for parts in problems:
        m = None
        for s, _ in parts:
            m_p = _col_max(s)
            m = m_p if m is None else jnp.maximum(m, m_p)
        maxes.append(m)
    outs = []
    for parts, m in zip(problems, maxes):
        out = None
        for s, v_t in parts:
            o_p = _dot(v_t, jnp.exp2(s - m).astype(MXU_DTYPE))
            out = o_p if out is None else out + o_p
        outs.append(out)
    return outs


def _width_classes(n_qblk, n_classes):
    n_cls = n_classes if n_qblk % n_classes == 0 else 1
    per = n_qblk // n_cls
    return [(c * per, (c + 1) * per, (c + 1) * per * Q_BLOCK) for c in range(n_cls)]


def _dsa_body(W, qit_ref, gwt_ref, qat_ref, ki_ref, ka_ref, vat_ref, o_ref, sel_ref, qk_ref,*, topk):
    f32 = jnp.float32
    QB = Q_BLOCK
    t = pl.program_id(1) * QB + lax.broadcasted_iota(jnp.int32, (1, QB), 1)
    spos = lax.broadcasted_iota(jnp.int32, (W, QB), 0)
    causal = spos <= t

    ki = ki_ref[:W, :]
    G = 4
    kpad = jnp.zeros((LANES - IDX_DIM, G * QB), MXU_DTYPE)
    score = jnp.zeros((W, QB), f32)
    for hg in range(IDX_HEADS // G):
        q_idx = jnp.concatenate(
            [qit_ref[(hg * G + g) * IDX_DIM:(hg * G + g + 1) * IDX_DIM, :] for g in range(G)], axis=1)
        logits = _dot(ki, jnp.concatenate([q_idx, kpad], axis=0))
        for g in range(G):
            w_h = gwt_ref[GATE_ROWS + hg * G + g:GATE_ROWS + hg * G + g + 1, :]
            score = score + jnp.maximum(logits[:, g * QB:(g + 1) * QB], 0.0) * w_h
    score = jnp.where(causal, score, -jnp.inf)

    q_all = jnp.concatenate([qat_ref[h * LANES:(h + 1) * LANES, :] for h in range(DSA_HEADS)], axis=1)
    chunk = W // QK_SIDE_STEPS

    def qk_chunk(it):
        r0 = pl.multiple_of(it * chunk, chunk)
        qk_ref[pl.ds(r0, chunk), :] = _dot(ka_ref[pl.ds(r0, chunk), :], q_all)

    lowest = float(jnp.finfo(jnp.float32).min)
    if W <= topk:
        qk_ref[:W, :] = _dot(ka_ref[:W, :], q_all)
        thr = jnp.full((1, QB), lowest, f32)
    else:
        thr = _kth_largest(score, float(topk), QK_SIDE_STEPS, qk_chunk)
        thr = jnp.maximum(thr, lowest)
    ge = score >= thr
    sel_ref[:W, :] = jnp.where(ge, 0.0, NEG_INF)

    def repair_ties():
        gt = score > thr
        eq = score == thr
        need = float(topk) - _col_sum(jnp.where(gt, 1.0, 0.0))
        cw = TIE_CHUNK
        r = lax.broadcasted_iota(jnp.int32, (cw, cw), 0)
        c = lax.broadcasted_iota(jnp.int32, (cw, cw), 1)
        lower = jnp.where(c <= r, 1.0, 0.0).astype(MXU_DTYPE)
        carry = jnp.zeros((1, QB), f32)
        for j in range(W // cw):
            sl = slice(j * cw, (j + 1) * cw)
            eq_j = jnp.where(eq[sl], 1.0, 0.0)
            prefix = _dot(lower, eq_j.astype(MXU_DTYPE)) + carry
            keep = gt[sl] | (eq[sl] & (prefix <= need))
            sel_ref[sl, :] = jnp.where(keep, 0.0, NEG_INF)
            carry = carry + jnp.sum(eq_j, axis=0, keepdims=True)

    if W > topk:
        pl.when(jnp.max(_col_sum(jnp.where(ge, 1.0, 0.0))) > float(topk))(repair_ties)

    G = 4
    va_t = vat_ref[:, :W]
    for hg in range(DSA_HEADS // G):
        s = qk_ref[:W, hg * G * QB:(hg + 1) * G * QB] + _lane_tile(sel_ref[:W, :], G)
        o_t = _softmax_pv([(s, va_t)])
        o_t = o_t[:HEAD_DIM] * (1.0 / o_t[HEAD_DIM:HEAD_DIM + 1])
        for g2 in range(G // 2):
            pair = jnp.concatenate([o_t[:, (2 * g2) * QB:(2 * g2 + 1) * QB],
                                    o_t[:, (2 * g2 + 1) * QB:(2 * g2 + 2) * QB]], axis=0)
            col = (hg * G + 2 * g2) * HEAD_DIM
            o_ref[:, col:col + 2 * HEAD_DIM] = pair.T.astype(o_ref.dtype)


def _dsa_kernel(qit_ref, gwt_ref, qat_ref, ki_ref, ka_ref, vat_ref, o_ref, sel_ref, qk_ref,*, topk, classes):
    qblk = pl.program_id(1)
    for lo, hi, W in classes:
        @pl.when((qblk >= lo) & (qblk < hi))
        def _(W=W):
            _dsa_body(W, qit_ref, gwt_ref, qat_ref, ki_ref, ka_ref, vat_ref, o_ref, sel_ref, qk_ref,topk=topk)


def _dsa(B, S, qi_t, gw_t, qa_t, ki, ka, va_t):
    nq = S // Q_BLOCK
    topk = min(DSA_TOPK_MAX, S // 4)
    fblk = lambda r: pl.BlockSpec((r, Q_BLOCK), lambda b, i: (0, b * nq + i))
    tseq = lambda w: pl.BlockSpec((S, w), lambda b, i: (b, 0))
    fseq = lambda r: pl.BlockSpec((r, S), lambda b, i: (0, b))
    return pl.pallas_call(
        functools.partial(_dsa_kernel, topk=topk, classes=_width_classes(nq, DSA_WIDTH_CLASSES)),
        grid=(B, nq),
        in_specs=[fblk(IDX_HEADS * IDX_DIM), fblk(LANES), fblk(DSA_HEADS * LANES),
                  tseq(LANES), tseq(LANES), fseq(LANES)],
        out_specs=pl.BlockSpec((Q_BLOCK, WIDTH_A), lambda b, i: (b * nq + i, 0)),
        out_shape=jax.ShapeDtypeStruct((B * S, WIDTH_A), MXU_DTYPE),
        scratch_shapes=[pltpu.VMEM((S, Q_BLOCK), jnp.float32),
                        pltpu.VMEM((S, DSA_HEADS * Q_BLOCK), jnp.float32)],
        compiler_params=_params(("parallel", "arbitrary")),
        name="dsa",
    )(qi_t, gw_t, qa_t, ki, ka, va_t)


def _nsa_body(W, first_blk, gwt_ref, qbt_ref, ks_ref, kw_ref, vst_ref, vwt_ref, kc_ref, vct_ref, ovl_ref, o_ref,
              *, seq_len):
    f32 = jnp.float32
    QB, G = Q_BLOCK, NSA_GROUP
    n_slc = seq_len // SLC_LEN
    n_top = min(SLC_TOPN, n_slc)
    win_len = WINDOW + QB
    nc = kc_ref.shape[1]

    t0 = pl.program_id(1) * QB
    t = t0 + lax.broadcasted_iota(jnp.int32, (1, QB), 1)
    t4 = _lane_tile(t, G)
    valid_c = lax.broadcasted_iota(jnp.int32, (nc, G * QB), 0) * CMP_STRIDE + (CMP_LEN - 1) <= t4
    past = first_blk * QB
    causal_tail = past + lax.broadcasted_iota(jnp.int32, (W - past, G * QB), 0) <= t4
    w0 = pl.multiple_of(jnp.maximum(t0 - WINDOW, 0), QB)
    full_window = past >= WINDOW
    if full_window:
        r_minus_lane = lax.broadcasted_iota(jnp.int32, (QB, G * QB), 0) - (t4 - t0)
        w_head_ok = r_minus_lane > 0
        w_tail_ok = r_minus_lane <= 0
    else:
        wd = t4 - (w0 + lax.broadcasted_iota(jnp.int32, (win_len, G * QB), 0))
        wvalid = (wd >= 0) & (wd < WINDOW)
    jblk = lax.broadcasted_iota(jnp.int32, (MAX_SLC_BLOCKS, QB), 0)
    cur = lax.shift_right_logical(t, 6)
    future = jblk > cur
    forced = (jblk == 0) | (jblk == cur) | (jblk == cur - 1)
    ovl = ovl_ref[...]

    KH = range(NSA_KV_HEADS)
    ksl = [slice(kh * LANES, (kh + 1) * LANES) for kh in KH]
    q = [jnp.concatenate([qbt_ref[(kh * G + g) * LANES:(kh * G + g + 1) * LANES, :] for g in range(G)], axis=1)
         for kh in KH]

    s_c = [jnp.where(valid_c, _dot(kc_ref[kh], q[kh]), NEG_INF) for kh in KH]
    s_w = [_dot(kw_ref[pl.ds(w0, win_len), ksl[kh]], q[kh]) for kh in KH]
    e_c = [jnp.where(valid_c, jnp.exp2(s_c[kh] - jnp.max(s_c[kh], axis=0, keepdims=True)), 0.0) for kh in KH]
    l_c = [jnp.sum(e_c[kh], axis=0, keepdims=True) for kh in KH]
    p_c = [e_c[kh] * (1.0 / jnp.where(l_c[kh] > 0.0, l_c[kh], 1.0)) for kh in KH]
    o_cmp = [_dot(vct_ref[kh], p_c[kh].astype(MXU_DTYPE)) for kh in KH]

    val = []
    for kh in KH:
        pc = p_c[kh]
        p_sum = pc[:, 0:QB] + pc[:, QB:2 * QB] + pc[:, 2 * QB:3 * QB] + pc[:, 3 * QB:4 * QB]
        p_hi = p_sum.astype(MXU_DTYPE)
        p_lo = (p_sum - p_hi.astype(f32)).astype(MXU_DTYPE)
        imp = _dot(ovl, p_hi) + _dot(ovl, p_lo)
        val.append(jnp.where(future, -jnp.inf, imp + jnp.where(forced, FORCE_BONUS, 0.0)))

    win_problems = []
    for kh in KH:
        v_w = vwt_ref[ksl[kh], pl.ds(w0, win_len)]
        if full_window:
            parts = [(jnp.where(w_head_ok, s_w[kh][:QB], NEG_INF), v_w[:, :QB]),
                     (s_w[kh][QB:WINDOW], v_w[:, QB:WINDOW]),
                     (jnp.where(w_tail_ok, s_w[kh][WINDOW:], NEG_INF), v_w[:, WINDOW:])]
        else:
            parts = [(jnp.where(wvalid, s_w[kh], NEG_INF), v_w)]
        win_problems.append(parts)
    o_win = _softmax_pv_staged(win_problems)

    rank = [jnp.zeros(val[kh].shape, f32) for kh in KH]
    for i in range(n_slc):
        for kh in KH:
            vi = val[kh][i:i + 1, :]
            rank[kh] = rank[kh] + jnp.where((vi > val[kh]) | ((vi == val[kh]) & (jblk > i)), 1.0, 0.0)

    s_s = []
    for kh in KH:
        bias = jnp.where((rank[kh] < n_top) & (jblk < n_slc), 0.0, -BLOCK_BIAS).astype(MXU_DTYPE)
        q_sel = jnp.concatenate([q[kh][:BIAS_COL], _lane_tile(bias, G)], axis=0)
        s_s.append(_dot(ks_ref[:W, ksl[kh]], q_sel))
    slc_problems = []
    for kh in KH:
        parts = [(jnp.where(causal_tail, s_s[kh][past:], NEG_INF), vst_ref[ksl[kh], past:W])]
        if past:
            parts.append((s_s[kh][:past], vst_ref[ksl[kh], :past]))
        slc_problems.append(parts)
    o_slc = _softmax_pv_staged(slc_problems)

    for kh in KH:
        inv_s = 1.0 / o_slc[kh][HEAD_DIM:HEAD_DIM + 1]
        inv_w = 1.0 / o_win[kh][HEAD_DIM:HEAD_DIM + 1]
        heads = []
        for g in range(G):
            h = kh * G + g
            ls = slice(g * QB, (g + 1) * QB)
            g0 = gwt_ref[3 * h + 0:3 * h + 1, :]
            g1 = gwt_ref[3 * h + 1:3 * h + 2, :] * inv_s[:, ls]
            g2 = gwt_ref[3 * h + 2:3 * h + 3, :] * inv_w[:, ls]
            heads.append(g0 * o_cmp[kh][:HEAD_DIM, ls] + g1 * o_slc[kh][:HEAD_DIM, ls]
                         + g2 * o_win[kh][:HEAD_DIM, ls])
        for g2_ in range(G // 2):
            pair = jnp.concatenate([heads[2 * g2_], heads[2 * g2_ + 1]], axis=0)
            col = (kh * G + 2 * g2_) * HEAD_DIM
            o_ref[:, col:col + 2 * HEAD_DIM] = pair.T.astype(o_ref.dtype)


def _nsa_kernel(gwt_ref, qbt_ref, ks_ref, kw_ref, vst_ref, vwt_ref, kc_ref, vct_ref, ovl_ref, o_ref,
                *, seq_len, classes):
    qblk = pl.program_id(1)
    for lo, hi, W in classes:
        @pl.when((qblk >= lo) & (qblk < hi))
        def _(W=W, lo=lo):
            _nsa_body(W, lo, gwt_ref, qbt_ref, ks_ref, kw_ref, vst_ref, vwt_ref, kc_ref, vct_ref, ovl_ref, o_ref,
                      seq_len=seq_len)


def _nsa(B, S, gw_t, qb_t, ks, kw, vs_t, vw_t, kc, vc_t, ovl):
    nq = S // Q_BLOCK
    nc = kc.shape[2]
    kvw = NSA_KV_HEADS * LANES
    fblk = lambda r: pl.BlockSpec((r, Q_BLOCK), lambda b, i: (0, b * nq + i))
    tseq = lambda w: pl.BlockSpec((S, w), lambda b, i: (b, 0))
    fseq = lambda r: pl.BlockSpec((r, S), lambda b, i: (0, b))
    return pl.pallas_call(
        functools.partial(_nsa_kernel, seq_len=S, classes=_width_classes(nq, NSA_WIDTH_CLASSES)),
        grid=(B, nq),
        in_specs=[fblk(LANES), fblk(NSA_HEADS * LANES), tseq(kvw), tseq(kvw), fseq(kvw), fseq(kvw),
                  pl.BlockSpec((None, NSA_KV_HEADS, nc, LANES), lambda b, i: (b, 0, 0, 0)),
                  pl.BlockSpec((None, NSA_KV_HEADS, LANES, nc), lambda b, i: (b, 0, 0, 0)),
                  pl.BlockSpec(ovl.shape, lambda b, i: (0, 0))],
        out_specs=pl.BlockSpec((Q_BLOCK, WIDTH_B), lambda b, i: (b * nq + i, 0)),
        out_shape=jax.ShapeDtypeStruct((B * S, WIDTH_B), MXU_DTYPE),
        compiler_params=_params(("parallel", "arbitrary")),
        name="nsa",
    )(gw_t, qb_t, ks, kw, vs_t, vw_t, kc, vc_t, ovl)


def _merge_ffn_kernel(x_ref, oa_ref, ob_ref, gmix_ref, wg_ref, wua_ref, wub_ref, wo_ref,
                      gmlp_ref, w1_ref, w2_ref, o_ref, *, chunk):
    x = x_ref[...]
    D = x.shape[1]
    h = _rms(x, gmix_ref[...]).astype(MXU_DTYPE)
    ga = jax.nn.sigmoid(_dot(h, wg_ref[:, :D]))
    gb = jax.nn.sigmoid(_dot(h, wg_ref[:, D:]))
    merged = ga * _dot(oa_ref[...], wua_ref[...]) + gb * _dot(ob_ref[...], wub_ref[...])
    x1 = x + _dot(merged.astype(MXU_DTYPE), wo_ref[...])
    h2 = _rms(x1, gmlp_ref[...]).astype(MXU_DTYPE)
    acc = x1
    for c in range(w1_ref.shape[1] // chunk):
        u = jnp.maximum(_dot(h2, w1_ref[:, c * chunk:(c + 1) * chunk]), 0.0)
        acc = acc + _dot((u * u).astype(MXU_DTYPE), w2_ref[c * chunk:(c + 1) * chunk, :])
    o_ref[...] = acc


def _merge_ffn(x2, oa, ob, g_mix, w_gate, w_up_a, w_up_b, w_out, g_mlp, w1, w2, tm):
    T, D = x2.shape
    row = lambda w: pl.BlockSpec((tm, w), lambda i: (i, 0))
    full = lambda a: pl.BlockSpec(a.shape, lambda i: (0,) * a.ndim, pipeline_mode=pl.Buffered(1))
    ins = (x2, oa, ob, g_mix, w_gate, w_up_a, w_up_b, w_out, g_mlp, w1, w2)
    return pl.pallas_call(
        functools.partial(_merge_ffn_kernel, chunk=1024),
        grid=(T // tm,),
        in_specs=[row(D), row(WIDTH_A), row(WIDTH_B)] + [full(a) for a in ins[3:]],
        out_specs=row(D),
        out_shape=jax.ShapeDtypeStruct((T, D), jnp.float32),
        compiler_params=_params(("parallel",)),
        name="merge_ffn",
    )(*ins)


def _layer(x, g_mix, w_in, g_q_a, g_k_a, g_lat_a, w_uk_a, w_uv_a, g_q_b, g_kc_b, g_ks_b, g_kw_b,
           pe_cmp_b, phi_k1_b, phi_k2_b, phi_v1_b, phi_v2_b, w_up_a, w_up_b, w_out, g_mlp, w_ff1, w_ff2):
    B, S, D = x.shape
    T = B * S
    f32 = jnp.float32
    mx = MXU_DTYPE
    n_slc = S // SLC_LEN
    assert n_slc <= MAX_SLC_BLOCKS and S % (DSA_WIDTH_CLASSES * TIE_CHUNK) == 0 and S >= WINDOW + Q_BLOCK
    col_sizes = (WIDTH_A, DSA_LATENT, IDX_HEADS * IDX_DIM, IDX_DIM, IDX_HEADS,
                 WIDTH_B, 6 * NSA_KV_W, 3 * NSA_HEADS, 2 * D)
    offs = np.cumsum((0,) + col_sizes)
    w_qa, w_ca, w_qi, w_ki, w_wi, w_qb, w_kvb, w_gb, w_gm = [
        w_in[:, offs[i]:offs[i + 1]] for i in range(len(col_sizes))]
    w_kc, w_vc, w_ks, w_vs, w_kw, w_vw = [w_kvb[:, j * NSA_KV_W:(j + 1) * NSA_KV_W] for j in range(6)]
    w_tok = jnp.concatenate(
        [w_ki, jnp.zeros((D, LANES - IDX_DIM), f32), w_ca, w_kc, w_vc, w_ks, w_kw], axis=1).astype(mx)
    assert w_tok.shape[1] == _T_COLS
    gw_pad = jnp.zeros((D, LANES - GATE_ROWS - IDX_HEADS), f32)
    w_feat = jnp.concatenate(
        [w_qi, w_gb, w_wi, gw_pad, w_qa, w_qb, w_vs, w_vw], axis=1).T.astype(mx)
    assert w_feat.shape[0] == _F_ROWS
    w_gate = w_gm.astype(mx)
    zpad = jnp.zeros((DSA_LATENT, LANES - HEAD_DIM), f32)
    w_uk = jnp.concatenate([w_uk_a, zpad], axis=1).astype(mx)
    w_uv_t = jnp.concatenate([w_uv_a, zpad], axis=1).T.astype(mx)
    row = lambda g: g.reshape(1, -1).astype(f32)
    rpad = lambda g: jnp.concatenate([row(g), jnp.zeros((1, LANES - HEAD_DIM), f32)], axis=1)
    colv = lambda g: g.reshape(-1, 1).astype(f32)

    x2 = x.reshape(T, D)
    (ki, ka, va_t, kcv, ks, kw, vs_t, vw_t, qi_t, gw_t, qa_t, qb_t) = _proj(
        x2, S, row(g_mix), w_tok, w_feat, row(g_lat_a), w_uk, w_uv_t, rpad(g_k_a), rpad(g_ks_b), rpad(g_kw_b),
        colv(g_q_a), colv(g_q_b), tm=512)

    n_chunk = S // CMP_STRIDE
    n_cmp = (S - CMP_LEN) // CMP_STRIDE + 1
    xc = kcv.reshape(2 * NSA_KV_HEADS, B, S, HEAD_DIM)
    phi1 = jnp.stack([phi_k1_b, phi_v1_b]).astype(mx)
    hpad = jnp.zeros((CMP_HIDDEN, LANES - HEAD_DIM), f32)
    w2k = jnp.concatenate([phi_k2_b, hpad], axis=1).astype(mx)
    w2v_t = jnp.concatenate([phi_v2_b, hpad], axis=1).T.astype(mx)
    kc, vc_t = _compress(xc, pe_cmp_b.astype(f32), phi1, w2k, w2v_t, rpad(g_kc_b))

    c_i = np.arange(n_chunk)[None, :] * CMP_STRIDE
    j_i = np.arange(MAX_SLC_BLOCKS)[:, None] * SLC_LEN
    ovl = (c_i < j_i + SLC_LEN) & (c_i + CMP_LEN > j_i) & (c_i < n_cmp * CMP_STRIDE) & (j_i < S)
    ovl = jnp.asarray(ovl, f32).astype(mx)

    o_a = _dsa(B, S, qi_t, gw_t, qa_t, ki, ka, va_t)
    o_b = _nsa(B, S, gw_t, qb_t, ks, kw, vs_t, vw_t, kc, vc_t, ovl)

    out = _merge_ffn(x2, o_a, o_b, row(g_mix), w_gate, w_up_a.astype(mx), w_up_b.astype(mx), w_out.astype(mx),
                     row(g_mlp), w_ff1.astype(mx), w_ff2.astype(mx), tm=512)
    return out.reshape(B, S, D)


def kernel(x, g_mix, w_in, g_q_a, g_k_a, g_lat_a, w_uk_a, w_uv_a, g_q_b, g_kc_b, g_ks_b, g_kw_b, pe_cmp_b,
           phi_k1_b, phi_k2_b, phi_v1_b, phi_v2_b, w_up_a, w_up_b, w_out, g_mlp, w_ff1, w_ff2):
    params = (g_mix, w_in, g_q_a, g_k_a, g_lat_a, w_uk_a, w_uv_a, g_q_b, g_kc_b, g_ks_b, g_kw_b, pe_cmp_b,
              phi_k1_b, phi_k2_b, phi_v1_b, phi_v2_b, w_up_a, w_up_b, w_out, g_mlp, w_ff1, w_ff2)
    for l in range(g_mix.shape[0]):
        x = _layer(x, *[p[l] for p in params])
    return x
```

```python
import functools
import math

import numpy as np
import jax
import jax.numpy as jnp
from jax import lax
from jax.experimental import pallas as pl
from jax.experimental.pallas import tpu as pltpu

HEAD_DIM = 64
DSA_HEADS = 8
DSA_LATENT = 128
IDX_HEADS = 8
IDX_DIM = 32
DSA_TOPK_MAX = 256
NSA_HEADS = 8
NSA_KV_HEADS = 2
NSA_GROUP = NSA_HEADS // NSA_KV_HEADS
CMP_LEN = 32
CMP_STRIDE = 16
CMP_HIDDEN = 128
SLC_LEN = 64
SLC_TOPN = 16
WINDOW = 512
FORCE_BONUS = 1e6
Q_BLOCK = 128
RMS_EPS = 1e-6
NEG_INF = -1e30

WIDTH_A = DSA_HEADS * HEAD_DIM
WIDTH_B = NSA_HEADS * HEAD_DIM
NSA_KV_W = NSA_KV_HEADS * HEAD_DIM

LANES = 128
GATE_ROWS = 3 * NSA_HEADS
ALIBI_COL = HEAD_DIM
BIAS_COL = 96
MAX_SLC_BLOCKS = LANES - BIAS_COL
BLOCK_BIAS = 2.0 ** 100
LOG2E = math.log2(math.e)
MXU_DTYPE = jnp.bfloat16
VMEM_LIMIT = 52 * 1024 * 1024
DSA_WIDTH_CLASSES = 8
NSA_WIDTH_CLASSES = 8
TIE_CHUNK = 128
REDUCE_ROWS = 64
QK_SIDE_STEPS = 8

_NT = (((1,), (1,)), ((), ()))


def _alibi_slopes(n_heads):
    return [float(v) for v in np.asarray(
        2.0 ** (-8.0 * np.arange(1, n_heads + 1) / n_heads), dtype=np.float32)]


def _dot(a, b):
    return jnp.dot(a, b, preferred_element_type=jnp.float32)


def _dot_nt(a, b):
    return lax.dot_general(a, b, _NT, preferred_element_type=jnp.float32)


def _rms(x, g):
    ms = jnp.mean(x * x, axis=-1, keepdims=True)
    return x * lax.rsqrt(ms + RMS_EPS) * g


def _params(sem):
    return pltpu.CompilerParams(dimension_semantics=sem, vmem_limit_bytes=VMEM_LIMIT)


def _split3(c):
    c1 = c.astype(MXU_DTYPE).astype(jnp.float32)
    c2 = (c - c1).astype(MXU_DTYPE).astype(jnp.float32)
    c3 = (c - c1 - c2).astype(MXU_DTYPE).astype(jnp.float32)
    return c1, c2, c3


_T_KI = (0, 128)
_T_CA = (128, 256)
_T_KCV = (256, 512)
_T_KS = (512, 640)
_T_KW = (640, 768)
_T_COLS = 768
_F_QI = (0, 256)
_F_GW = (256, 384)
_F_QA = (384, 896)
_F_QB = (896, 1408)
_F_VS = (1408, 1536)
_F_VW = (1536, 1664)
_F_ROWS = 1664


def _proj_kernel(x_ref, gmix_ref, wtok_ref, wfeat_ref, glat_ref, wuk_ref, wuvt_ref, gka_ref, gks_ref, gkw_ref,
                 gqa_ref, gqb_ref,
                 ki_ref, ka_ref, vat_ref, kcv_ref, ks_ref, kw_ref, vst_ref, vwt_ref, qit_ref, gwt_ref,
                 qat_ref, qbt_ref, *, seq_len, w_idx_scale, slopes_a, slopes_b):
    f32 = jnp.float32
    x = x_ref[...]
    tm = x.shape[0]
    h = _rms(x, gmix_ref[...]).astype(MXU_DTYPE)

    def tok(c):
        return _dot(h, wtok_ref[:, c[0]:c[1]])

    def feat(r):
        return _dot_nt(wfeat_ref[r[0]:r[1], :], h)

    s0 = (pl.program_id(0) % (seq_len // tm)) * tm
    spos = s0 + lax.broadcasted_iota(jnp.int32, (tm, LANES), 0)
    lane = lax.broadcasted_iota(jnp.int32, (tm, LANES), 1)
    c1, c2, c3 = _split3(spos.astype(f32) * LOG2E)
    alibi_cols = jnp.where(lane == ALIBI_COL, c1,
                           jnp.where(lane == ALIBI_COL + 1, c2, jnp.where(lane == ALIBI_COL + 2, c3, 0.0)))
    block_onehot = jnp.where(lane - BIAS_COL == lax.shift_right_logical(spos, 6), 1.0, 0.0)

    def key_slab(raw, g):
        ms = jnp.sum(raw * raw, axis=-1, keepdims=True) * (1.0 / HEAD_DIM)
        return raw * lax.rsqrt(ms + RMS_EPS) * g

    p_tok = {c: tok(c) for c in (_T_KI, _T_CA, _T_KCV, _T_KS, _T_KW)}
    p_feat = {r: feat(r) for r in (_F_QI, _F_GW, _F_QA, _F_QB, _F_VS, _F_VW)}

    ki_ref[...] = p_tok[_T_KI].astype(ki_ref.dtype)
    c = _rms(p_tok[_T_CA], glat_ref[...]).astype(MXU_DTYPE)
    ka_ref[...] = (key_slab(_dot(c, wuk_ref[...]), gka_ref[...]) + alibi_cols).astype(ka_ref.dtype)
    va_t = _dot_nt(wuvt_ref[...], c)
    frow = lax.broadcasted_iota(jnp.int32, va_t.shape, 0)
    vat_ref[...] = jnp.where(frow == HEAD_DIM, 1.0, va_t).astype(vat_ref.dtype)
    kcv = p_tok[_T_KCV]
    for j in range(2 * NSA_KV_HEADS):
        kcv_ref[j] = kcv[:, j * HEAD_DIM:(j + 1) * HEAD_DIM]
    low = lane < HEAD_DIM
    for cols, g_ref, out_ref, extra in ((_T_KS, gks_ref, ks_ref, alibi_cols + block_onehot),
                                        (_T_KW, gkw_ref, kw_ref, alibi_cols)):
        pair = p_tok[cols]
        for kh in range(NSA_KV_HEADS):
            raw = pair if kh == 0 else pltpu.roll(pair, HEAD_DIM, 1)
            slab = key_slab(jnp.where(low, raw, 0.0), g_ref[...]) + extra
            out_ref[:, kh * LANES:(kh + 1) * LANES] = slab.astype(out_ref.dtype)

    qit_ref[...] = p_feat[_F_QI].astype(qit_ref.dtype)
    gw = p_feat[_F_GW]
    grow = lax.broadcasted_iota(jnp.int32, gw.shape, 0)
    gwt_ref[...] = jnp.where(grow < GATE_ROWS, jax.nn.sigmoid(gw), gw * w_idx_scale)
    erow = lax.broadcasted_iota(jnp.int32, (HEAD_DIM, tm), 0)
    q_scale = HEAD_DIM ** -0.5 * LOG2E
    for rows, g_ref, out_ref, slopes in ((_F_QA, gqa_ref, qat_ref, slopes_a), (_F_QB, gqb_ref, qbt_ref, slopes_b)):
        q_all = p_feat[rows]
        for hd in range(len(slopes)):
            q = q_all[hd * HEAD_DIM:(hd + 1) * HEAD_DIM]
            ms = jnp.mean(q * q, axis=0, keepdims=True)
            q = q * lax.rsqrt(ms + RMS_EPS) * g_ref[...] * q_scale
            out_ref[hd * LANES:hd * LANES + HEAD_DIM, :] = q.astype(out_ref.dtype)
            extra = jnp.where(erow < 3, slopes[hd], 0.0)
            out_ref[hd * LANES + HEAD_DIM:(hd + 1) * LANES, :] = extra.astype(out_ref.dtype)
    ones_row = jnp.where(erow == 0, 1.0, 0.0)
    for rows, out_ref in ((_F_VS, vst_ref), (_F_VW, vwt_ref)):
        v_t = p_feat[rows]
        for kh in range(NSA_KV_HEADS):
            out_ref[kh * LANES:kh * LANES + HEAD_DIM, :] = v_t[kh * HEAD_DIM:(kh + 1) * HEAD_DIM].astype(out_ref.dtype)
            out_ref[kh * LANES + HEAD_DIM:(kh + 1) * LANES, :] = ones_row.astype(out_ref.dtype)


def _proj(x2, seq_len, g_mix, w_tok, w_feat, g_lat, w_uk, w_uv_t, gk_a, gks, gkw, gq_a, gq_b, tm):
    T, D = x2.shape
    tokm = lambda w: pl.BlockSpec((tm, w), lambda i: (i, 0))
    featm = lambda r: pl.BlockSpec((r, tm), lambda i: (0, i))
    full = lambda a: pl.BlockSpec(a.shape, lambda i: (0,) * a.ndim)
    f32 = jnp.float32
    mx = MXU_DTYPE
    out_shape = (
        jax.ShapeDtypeStruct((T, LANES), mx),
        jax.ShapeDtypeStruct((T, LANES), mx),
        jax.ShapeDtypeStruct((LANES, T), mx),
        jax.ShapeDtypeStruct((2 * NSA_KV_HEADS, T, HEAD_DIM), f32),
        jax.ShapeDtypeStruct((T, NSA_KV_HEADS * LANES), mx),
        jax.ShapeDtypeStruct((T, NSA_KV_HEADS * LANES), mx),
        jax.ShapeDtypeStruct((NSA_KV_HEADS * LANES, T), mx),
        jax.ShapeDtypeStruct((NSA_KV_HEADS * LANES, T), mx),
        jax.ShapeDtypeStruct((IDX_HEADS * IDX_DIM, T), mx),
        jax.ShapeDtypeStruct((LANES, T), f32),
        jax.ShapeDtypeStruct((DSA_HEADS * LANES, T), mx),
        jax.ShapeDtypeStruct((NSA_HEADS * LANES, T), mx),
    )
    out_specs = (
        tokm(LANES), tokm(LANES), featm(LANES),
        pl.BlockSpec((2 * NSA_KV_HEADS, tm, HEAD_DIM), lambda i: (0, i, 0)),
        tokm(NSA_KV_HEADS * LANES), tokm(NSA_KV_HEADS * LANES),
        featm(NSA_KV_HEADS * LANES), featm(NSA_KV_HEADS * LANES),
        featm(IDX_HEADS * IDX_DIM), featm(LANES), featm(DSA_HEADS * LANES), featm(NSA_HEADS * LANES),
    )
    kern = functools.partial(
        _proj_kernel, seq_len=seq_len, w_idx_scale=IDX_HEADS ** -0.5 * IDX_DIM ** -0.5,
        slopes_a=_alibi_slopes(DSA_HEADS), slopes_b=_alibi_slopes(NSA_HEADS))
    ins = (x2, g_mix, w_tok, w_feat, g_lat, w_uk, w_uv_t, gk_a, gks, gkw, gq_a, gq_b)
    return pl.pallas_call(
        kern,
        grid=(T // tm,),
        in_specs=[tokm(D)] + [full(a) for a in ins[1:]],
        out_specs=out_specs,
        out_shape=out_shape,
        compiler_params=_params(("parallel",)),
        name="proj",
    )(*ins)


def _compress_kernel(xk_ref, xv_ref, pe_ref, w1_ref, w2k_ref, w2vt_ref, gkc_ref, kc_ref, vct_ref):
    n = kc_ref.shape[0]

    def hidden(x_ref, kv):
        a = jnp.zeros((n, CMP_HIDDEN), jnp.float32)
        b = jnp.zeros((n, CMP_HIDDEN), jnp.float32)
        for l in range(CMP_STRIDE):
            x_l = x_ref[pl.ds(l, n, stride=CMP_STRIDE), :]
            for acc_is_b, row in ((False, l), (True, CMP_STRIDE + l)):
                lhs = (x_l + pe_ref[row:row + 1, :]).astype(MXU_DTYPE)
                part = _dot(lhs, w1_ref[kv, row * HEAD_DIM:(row + 1) * HEAD_DIM, :])
                if acc_is_b:
                    b = b + part
                else:
                    a = a + part
        return jax.nn.gelu(a + pltpu.roll(b, n - 1, 0)).astype(MXU_DTYPE)

    yk = _dot(hidden(xk_ref, 0), w2k_ref[...])
    ms = jnp.sum(yk * yk, axis=-1, keepdims=True) * (1.0 / HEAD_DIM)
    kc_ref[...] = (yk * lax.rsqrt(ms + RMS_EPS) * gkc_ref[...]).astype(kc_ref.dtype)
    vct_ref[...] = _dot_nt(w2vt_ref[...], hidden(xv_ref, 1)).astype(vct_ref.dtype)


def _compress(xc, pe, phi1, w2k, w2v_t, g_kc):
    _, B, S, w = xc.shape
    n = S // CMP_STRIDE
    full = lambda a: pl.BlockSpec(a.shape, lambda b, kh: (0,) * a.ndim)
    return pl.pallas_call(
        _compress_kernel,
        grid=(B, NSA_KV_HEADS),
        in_specs=[
            pl.BlockSpec((None, None, S, w), lambda b, kh: (kh, b, 0, 0)),
            pl.BlockSpec((None, None, S, w), lambda b, kh: (NSA_KV_HEADS + kh, b, 0, 0)),
            full(pe), full(phi1), full(w2k), full(w2v_t), full(g_kc),
        ],
        out_specs=(pl.BlockSpec((None, None, n, LANES), lambda b, kh: (b, kh, 0, 0)),
                   pl.BlockSpec((None, None, LANES, n), lambda b, kh: (b, kh, 0, 0))),
        out_shape=(jax.ShapeDtypeStruct((B, NSA_KV_HEADS, n, LANES), MXU_DTYPE),
                   jax.ShapeDtypeStruct((B, NSA_KV_HEADS, LANES, n), MXU_DTYPE)),
        compiler_params=_params(("parallel", "parallel")),
        name="compress",
    )(xc, xc, pe, phi1, w2k, w2v_t, g_kc)


_INT_MIN = -2 ** 31


def _key_to_float(u):
    key = u ^ jnp.int32(_INT_MIN)
    bits = jnp.where(key >= 0, key, key ^ jnp.int32(0x7FFFFFFF))
    return lax.bitcast_convert_type(bits, jnp.float32)


def _col_reduce(x, op, final):
    rows = REDUCE_ROWS
    acc = x
    if x.shape[0] > rows and x.shape[0] % rows == 0:
        acc = x[:rows]
        for j in range(1, x.shape[0] // rows):
            acc = op(acc, x[j * rows:(j + 1) * rows])
    return final(acc.astype(jnp.float32), axis=0, keepdims=True)


def _col_sum(x):
    return _col_reduce(x, jnp.add, jnp.sum)


def _col_max(x):
    return _col_reduce(x, jnp.maximum, jnp.max)


def _kth_largest(score, k, side_steps=0, side_work=None):
    def body(it, u):
        u_try = u | lax.shift_left(jnp.int32(1), 31 - it)
        cnt = _col_sum(jnp.where(score >= _key_to_float(u_try), 1.0, 0.0))
        return jnp.where(cnt >= k, u_try, u)

    def body_with_side_work(j, u):
        side_work(j)
        for i in range(32 // side_steps):
            u = body(j * (32 // side_steps) + i, u)
        return u

    u = jnp.zeros((1, score.shape[1]), jnp.int32)
    if side_steps:
        u = lax.fori_loop(0, side_steps, body_with_side_work, u)
    else:
        u = lax.fori_loop(0, 32, body, u)
    thr = _key_to_float(u)
    return jnp.where(thr != thr, -jnp.inf, thr)


def _lane_tile(x, n):
    return jnp.concatenate([x] * n, axis=1)


def _softmax_pv(parts):
    return _softmax_pv_staged([parts])[0]


def _softmax_pv_staged(problems):
    maxes = []
    for parts in problems:
        m = None
        for s, _ in parts:
            m_p = _col_max(s)
            m = m_p if m is None else jnp.maximum(m, m_p)
        maxes.append(m)
    outs = []
    for parts, m in zip(problems, maxes):
        out = None
        for s, v_t in parts:
            o_p = _dot(v_t, jnp.exp2(s - m).astype(MXU_DTYPE))
            out = o_p if out is None else out + o_p
        outs.append(out)
    return outs


def _width_classes(n_qblk, n_classes):
    n_cls = n_classes if n_qblk % n_classes == 0 else 1
    per = n_qblk // n_cls
    return [(c * per, (c + 1) * per, (c + 1) * per * Q_BLOCK) for c in range(n_cls)]


def _dsa_body(W, qit_ref, gwt_ref, qat_ref, ki_ref, ka_ref, vat_ref, o_ref, sel_ref, qk_ref,*, topk):
    f32 = jnp.float32
    QB = Q_BLOCK
    t = pl.program_id(1) * QB + lax.broadcasted_iota(jnp.int32, (1, QB), 1)
    spos = lax.broadcasted_iota(jnp.int32, (W, QB), 0)
    causal = spos <= t

    ki = ki_ref[:W, :]
    G = 4
    kpad = jnp.zeros((LANES - IDX_DIM, G * QB), MXU_DTYPE)
    score = jnp.zeros((W, QB), f32)
    for hg in range(IDX_HEADS // G):
        q_idx = jnp.concatenate(
            [qit_ref[(hg * G + g) * IDX_DIM:(hg * G + g + 1) * IDX_DIM, :] for g in range(G)], axis=1)
        logits = _dot(ki, jnp.concatenate([q_idx, kpad], axis=0))
        for g in range(G):
            w_h = gwt_ref[GATE_ROWS + hg * G + g:GATE_ROWS + hg * G + g + 1, :]
            score = score + jnp.maximum(logits[:, g * QB:(g + 1) * QB], 0.0) * w_h
    score = jnp.where(causal, score, -jnp.inf)

    q_all = jnp.concatenate([qat_ref[h * LANES:(h + 1) * LANES, :] for h in range(DSA_HEADS)], axis=1)
    chunk = W // QK_SIDE_STEPS

    def qk_chunk(it):
        r0 = pl.multiple_of(it * chunk, chunk)
        qk_ref[pl.ds(r0, chunk), :] = _dot(ka_ref[pl.ds(r0, chunk), :], q_all)

    lowest = float(jnp.finfo(jnp.float32).min)
    if W <= topk:
        qk_ref[:W, :] = _dot(ka_ref[:W, :], q_all)
        thr = jnp.full((1, QB), lowest, f32)
    else:
        thr = _kth_largest(score, float(topk), QK_SIDE_STEPS, qk_chunk)
        thr = jnp.maximum(thr, lowest)
    ge = score >= thr
    sel_ref[:W, :] = jnp.where(ge, 0.0, NEG_INF)

    def repair_ties():
        gt = score > thr
        eq = score == thr
        need = float(topk) - _col_sum(jnp.where(gt, 1.0, 0.0))
        cw = TIE_CHUNK
        r = lax.broadcasted_iota(jnp.int32, (cw, cw), 0)
        c = lax.broadcasted_iota(jnp.int32, (cw, cw), 1)
        lower = jnp.where(c <= r, 1.0, 0.0).astype(MXU_DTYPE)
        carry = jnp.zeros((1, QB), f32)
        for j in range(W // cw):
            sl = slice(j * cw, (j + 1) * cw)
            eq_j = jnp.where(eq[sl], 1.0, 0.0)
            prefix = _dot(lower, eq_j.astype(MXU_DTYPE)) + carry
            keep = gt[sl] | (eq[sl] & (prefix <= need))
            sel_ref[sl, :] = jnp.where(keep, 0.0, NEG_INF)
            carry = carry + jnp.sum(eq_j, axis=0, keepdims=True)

    if W > topk:
        pl.when(jnp.max(_col_sum(jnp.where(ge, 1.0, 0.0))) > float(topk))(repair_ties)

    G = 4
    va_t = vat_ref[:, :W]
    for hg in range(DSA_HEADS // G):
        s = qk_ref[:W, hg * G * QB:(hg + 1) * G * QB] + _lane_tile(sel_ref[:W, :], G)
        o_t = _softmax_pv([(s, va_t)])
        o_t = o_t[:HEAD_DIM] * (1.0 / o_t[HEAD_DIM:HEAD_DIM + 1])
        for g2 in range(G // 2):
            pair = jnp.concatenate([o_t[:, (2 * g2) * QB:(2 * g2 + 1) * QB],
                                    o_t[:, (2 * g2 + 1) * QB:(2 * g2 + 2) * QB]], axis=0)
            col = (hg * G + 2 * g2) * HEAD_DIM
            o_ref[:, col:col + 2 * HEAD_DIM] = pair.T.astype(o_ref.dtype)


def _dsa_kernel(qit_ref, gwt_ref, qat_ref, ki_ref, ka_ref, vat_ref, o_ref, sel_ref, qk_ref,*, topk, classes):
    qblk = pl.program_id(1)
    for lo, hi, W in classes:
        @pl.when((qblk >= lo) & (qblk < hi))
        def _(W=W):
            _dsa_body(W, qit_ref, gwt_ref, qat_ref, ki_ref, ka_ref, vat_ref, o_ref, sel_ref, qk_ref,topk=topk)


def _dsa(B, S, qi_t, gw_t, qa_t, ki, ka, va_t):
    nq = S // Q_BLOCK
    topk = min(DSA_TOPK_MAX, S // 4)
    fblk = lambda r: pl.BlockSpec((r, Q_BLOCK), lambda b, i: (0, b * nq + i))
    tseq = lambda w: pl.BlockSpec((S, w), lambda b, i: (b, 0))
    fseq = lambda r: pl.BlockSpec((r, S), lambda b, i: (0, b))
    return pl.pallas_call(
        functools.partial(_dsa_kernel, topk=topk, classes=_width_classes(nq, DSA_WIDTH_CLASSES)),
        grid=(B, nq),
        in_specs=[fblk(IDX_HEADS * IDX_DIM), fblk(LANES), fblk(DSA_HEADS * LANES),
                  tseq(LANES), tseq(LANES), fseq(LANES)],
        out_specs=pl.BlockSpec((Q_BLOCK, WIDTH_A), lambda b, i: (b * nq + i, 0)),
        out_shape=jax.ShapeDtypeStruct((B * S, WIDTH_A), MXU_DTYPE),
        scratch_shapes=[pltpu.VMEM((S, Q_BLOCK), jnp.float32),
                        pltpu.VMEM((S, DSA_HEADS * Q_BLOCK), jnp.float32)],
        compiler_params=_params(("parallel", "arbitrary")),
        name="dsa",
    )(qi_t, gw_t, qa_t, ki, ka, va_t)


def _nsa_body(W, first_blk, gwt_ref, qbt_ref, ks_ref, kw_ref, vst_ref, vwt_ref, kc_ref, vct_ref, ovl_ref, o_ref,
              *, seq_len):
    f32 = jnp.float32
    QB, G = Q_BLOCK, NSA_GROUP
    n_slc = seq_len // SLC_LEN
    n_top = min(SLC_TOPN, n_slc)
    win_len = WINDOW + QB
    nc = kc_ref.shape[1]

    t0 = pl.program_id(1) * QB
    t = t0 + lax.broadcasted_iota(jnp.int32, (1, QB), 1)
    t4 = _lane_tile(t, G)
    valid_c = lax.broadcasted_iota(jnp.int32, (nc, G * QB), 0) * CMP_STRIDE + (CMP_LEN - 1) <= t4
    past = first_blk * QB
    causal_tail = past + lax.broadcasted_iota(jnp.int32, (W - past, G * QB), 0) <= t4
    w0 = pl.multiple_of(jnp.maximum(t0 - WINDOW, 0), QB)
    full_window = past >= WINDOW
    if full_window:
        r_minus_lane = lax.broadcasted_iota(jnp.int32, (QB, G * QB), 0) - (t4 - t0)
        w_head_ok = r_minus_lane > 0
        w_tail_ok = r_minus_lane <= 0
    else:
        wd = t4 - (w0 + lax.broadcasted_iota(jnp.int32, (win_len, G * QB), 0))
        wvalid = (wd >= 0) & (wd < WINDOW)
    jblk = lax.broadcasted_iota(jnp.int32, (MAX_SLC_BLOCKS, QB), 0)
    cur = lax.shift_right_logical(t, 6)
    future = jblk > cur
    forced = (jblk == 0) | (jblk == cur) | (jblk == cur - 1)
    ovl = ovl_ref[...]

    KH = range(NSA_KV_HEADS)
    ksl = [slice(kh * LANES, (kh + 1) * LANES) for kh in KH]
    q = [jnp.concatenate([qbt_ref[(kh * G + g) * LANES:(kh * G + g + 1) * LANES, :] for g in range(G)], axis=1)
         for kh in KH]

    s_c = [jnp.where(valid_c, _dot(kc_ref[kh], q[kh]), NEG_INF) for kh in KH]
    s_w = [_dot(kw_ref[pl.ds(w0, win_len), ksl[kh]], q[kh]) for kh in KH]
    e_c = [jnp.where(valid_c, jnp.exp2(s_c[kh] - jnp.max(s_c[kh], axis=0, keepdims=True)), 0.0) for kh in KH]
    l_c = [jnp.sum(e_c[kh], axis=0, keepdims=True) for kh in KH]
    p_c = [e_c[kh] * (1.0 / jnp.where(l_c[kh] > 0.0, l_c[kh], 1.0)) for kh in KH]
    o_cmp = [_dot(vct_ref[kh], p_c[kh].astype(MXU_DTYPE)) for kh in KH]

    val = []
    for kh in KH:
        pc = p_c[kh]
        p_sum = pc[:, 0:QB] + pc[:, QB:2 * QB] + pc[:, 2 * QB:3 * QB] + pc[:, 3 * QB:4 * QB]
        p_hi = p_sum.astype(MXU_DTYPE)
        p_lo = (p_sum - p_hi.astype(f32)).astype(MXU_DTYPE)
        imp = _dot(ovl, p_hi) + _dot(ovl, p_lo)
        val.append(jnp.where(future, -jnp.inf, imp + jnp.where(forced, FORCE_BONUS, 0.0)))

    win_problems = []
    for kh in KH:
        v_w = vwt_ref[ksl[kh], pl.ds(w0, win_len)]
        if full_window:
            parts = [(jnp.where(w_head_ok, s_w[kh][:QB], NEG_INF), v_w[:, :QB]),
                     (s_w[kh][QB:WINDOW], v_w[:, QB:WINDOW]),
                     (jnp.where(w_tail_ok, s_w[kh][WINDOW:], NEG_INF), v_w[:, WINDOW:])]
        else:
            parts = [(jnp.where(wvalid, s_w[kh], NEG_INF), v_w)]
        win_problems.append(parts)
    o_win = _softmax_pv_staged(win_problems)

    rank = [jnp.zeros(val[kh].shape, f32) for kh in KH]
    for i in range(min(n_slc, W // SLC_LEN)):
        for kh in KH:
            vi = val[kh][i:i + 1, :]
            rank[kh] = rank[kh] + jnp.where((vi > val[kh]) | ((vi == val[kh]) & (jblk > i)), 1.0, 0.0)

    s_s = []
    for kh in KH:
        bias = jnp.where((rank[kh] < n_top) & (jblk < n_slc), 0.0, -BLOCK_BIAS).astype(MXU_DTYPE)
        q_sel = jnp.concatenate([q[kh][:BIAS_COL], _lane_tile(bias, G)], axis=0)
        s_s.append(_dot(ks_ref[:W, ksl[kh]], q_sel))
    slc_problems = []
    for kh in KH:
        parts = [(jnp.where(causal_tail, s_s[kh][past:], NEG_INF), vst_ref[ksl[kh], past:W])]
        if past:
            parts.append((s_s[kh][:past], vst_ref[ksl[kh], :past]))
        slc_problems.append(parts)
    o_slc = _softmax_pv_staged(slc_problems)

    for kh in KH:
        inv_s = 1.0 / o_slc[kh][HEAD_DIM:HEAD_DIM + 1]
        inv_w = 1.0 / o_win[kh][HEAD_DIM:HEAD_DIM + 1]
        heads = []
        for g in range(G):
            h = kh * G + g
            ls = slice(g * QB, (g + 1) * QB)
            g0 = gwt_ref[3 * h + 0:3 * h + 1, :]
            g1 = gwt_ref[3 * h + 1:3 * h + 2, :] * inv_s[:, ls]
            g2 = gwt_ref[3 * h + 2:3 * h + 3, :] * inv_w[:, ls]
            heads.append(g0 * o_cmp[kh][:HEAD_DIM, ls] + g1 * o_slc[kh][:HEAD_DIM, ls]
                         + g2 * o_win[kh][:HEAD_DIM, ls])
        for g2_ in range(G // 2):
            pair = jnp.concatenate([heads[2 * g2_], heads[2 * g2_ + 1]], axis=0)
            col = (kh * G + 2 * g2_) * HEAD_DIM
            o_ref[:, col:col + 2 * HEAD_DIM] = pair.T.astype(o_ref.dtype)


def _nsa_kernel(gwt_ref, qbt_ref, ks_ref, kw_ref, vst_ref, vwt_ref, kc_ref, vct_ref, ovl_ref, o_ref,
                *, seq_len, classes):
    qblk = pl.program_id(1)
    for lo, hi, W in classes:
        @pl.when((qblk >= lo) & (qblk < hi))
        def _(W=W, lo=lo):
            _nsa_body(W, lo, gwt_ref, qbt_ref, ks_ref, kw_ref, vst_ref, vwt_ref, kc_ref, vct_ref, ovl_ref, o_ref,
                      seq_len=seq_len)


def _nsa(B, S, gw_t, qb_t, ks, kw, vs_t, vw_t, kc, vc_t, ovl):
    nq = S // Q_BLOCK
    nc = kc.shape[2]
    kvw = NSA_KV_HEADS * LANES
    fblk = lambda r: pl.BlockSpec((r, Q_BLOCK), lambda b, i: (0, b * nq + i))
    tseq = lambda w: pl.BlockSpec((S, w), lambda b, i: (b, 0))
    fseq = lambda r: pl.BlockSpec((r, S), lambda b, i: (0, b))
    return pl.pallas_call(
        functools.partial(_nsa_kernel, seq_len=S, classes=_width_classes(nq, NSA_WIDTH_CLASSES)),
        grid=(B, nq),
        in_specs=[fblk(LANES), fblk(NSA_HEADS * LANES), tseq(kvw), tseq(kvw), fseq(kvw), fseq(kvw),
                  pl.BlockSpec((None, NSA_KV_HEADS, nc, LANES), lambda b, i: (b, 0, 0, 0)),
                  pl.BlockSpec((None, NSA_KV_HEADS, LANES, nc), lambda b, i: (b, 0, 0, 0)),
                  pl.BlockSpec(ovl.shape, lambda b, i: (0, 0))],
        out_specs=pl.BlockSpec((Q_BLOCK, WIDTH_B), lambda b, i: (b * nq + i, 0)),
        out_shape=jax.ShapeDtypeStruct((B * S, WIDTH_B), MXU_DTYPE),
        compiler_params=_params(("parallel", "arbitrary")),
        name="nsa",
    )(gw_t, qb_t, ks, kw, vs_t, vw_t, kc, vc_t, ovl)


def _merge_ffn_kernel(x_ref, oa_ref, ob_ref, gmix_ref, wg_ref, wua_ref, wub_ref, wo_ref,
                      gmlp_ref, w1_ref, w2_ref, o_ref, *, chunk):
    x = x_ref[...]
    D = x.shape[1]
    h = _rms(x, gmix_ref[...]).astype(MXU_DTYPE)
    ga = jax.nn.sigmoid(_dot(h, wg_ref[:, :D]))
    gb = jax.nn.sigmoid(_dot(h, wg_ref[:, D:]))
    merged = ga * _dot(oa_ref[...], wua_ref[...]) + gb * _dot(ob_ref[...], wub_ref[...])
    x1 = x + _dot(merged.astype(MXU_DTYPE), wo_ref[...])
    h2 = _rms(x1, gmlp_ref[...]).astype(MXU_DTYPE)
    acc = x1
    for c in range(w1_ref.shape[1] // chunk):
        u = jnp.maximum(_dot(h2, w1_ref[:, c * chunk:(c + 1) * chunk]), 0.0)
        acc = acc + _dot((u * u).astype(MXU_DTYPE), w2_ref[c * chunk:(c + 1) * chunk, :])
    o_ref[...] = acc


def _merge_ffn(x2, oa, ob, g_mix, w_gate, w_up_a, w_up_b, w_out, g_mlp, w1, w2, tm):
    T, D = x2.shape
    row = lambda w: pl.BlockSpec((tm, w), lambda i: (i, 0))
    full = lambda a: pl.BlockSpec(a.shape, lambda i: (0,) * a.ndim, pipeline_mode=pl.Buffered(1))
    ins = (x2, oa, ob, g_mix, w_gate, w_up_a, w_up_b, w_out, g_mlp, w1, w2)
    return pl.pallas_call(
        functools.partial(_merge_ffn_kernel, chunk=1024),
        grid=(T // tm,),
        in_specs=[row(D), row(WIDTH_A), row(WIDTH_B)] + [full(a) for a in ins[3:]],
        out_specs=row(D),
        out_shape=jax.ShapeDtypeStruct((T, D), jnp.float32),
        compiler_params=_params(("parallel",)),
        name="merge_ffn",
    )(*ins)


def _layer(x, g_mix, w_in, g_q_a, g_k_a, g_lat_a, w_uk_a, w_uv_a, g_q_b, g_kc_b, g_ks_b, g_kw_b,
           pe_cmp_b, phi_k1_b, phi_k2_b, phi_v1_b, phi_v2_b, w_up_a, w_up_b, w_out, g_mlp, w_ff1, w_ff2):
    B, S, D = x.shape
    T = B * S
    f32 = jnp.float32
    mx = MXU_DTYPE
    n_slc = S // SLC_LEN
    assert n_slc <= MAX_SLC_BLOCKS and S % (DSA_WIDTH_CLASSES * TIE_CHUNK) == 0 and S >= WINDOW + Q_BLOCK
    col_sizes = (WIDTH_A, DSA_LATENT, IDX_HEADS * IDX_DIM, IDX_DIM, IDX_HEADS,
                 WIDTH_B, 6 * NSA_KV_W, 3 * NSA_HEADS, 2 * D)
    offs = np.cumsum((0,) + col_sizes)
    w_qa, w_ca, w_qi, w_ki, w_wi, w_qb, w_kvb, w_gb, w_gm = [
        w_in[:, offs[i]:offs[i + 1]] for i in range(len(col_sizes))]
    w_kc, w_vc, w_ks, w_vs, w_kw, w_vw = [w_kvb[:, j * NSA_KV_W:(j + 1) * NSA_KV_W] for j in range(6)]
    w_tok = jnp.concatenate(
        [w_ki, jnp.zeros((D, LANES - IDX_DIM), f32), w_ca, w_kc, w_vc, w_ks, w_kw], axis=1).astype(mx)
    assert w_tok.shape[1] == _T_COLS
    gw_pad = jnp.zeros((D, LANES - GATE_ROWS - IDX_HEADS), f32)
    w_feat = jnp.concatenate(
        [w_qi, w_gb, w_wi, gw_pad, w_qa, w_qb, w_vs, w_vw], axis=1).T.astype(mx)
    assert w_feat.shape[0] == _F_ROWS
    w_gate = w_gm.astype(mx)
    zpad = jnp.zeros((DSA_LATENT, LANES - HEAD_DIM), f32)
    w_uk = jnp.concatenate([w_uk_a, zpad], axis=1).astype(mx)
    w_uv_t = jnp.concatenate([w_uv_a, zpad], axis=1).T.astype(mx)
    row = lambda g: g.reshape(1, -1).astype(f32)
    rpad = lambda g: jnp.concatenate([row(g), jnp.zeros((1, LANES - HEAD_DIM), f32)], axis=1)
    colv = lambda g: g.reshape(-1, 1).astype(f32)

    x2 = x.reshape(T, D)
    (ki, ka, va_t, kcv, ks, kw, vs_t, vw_t, qi_t, gw_t, qa_t, qb_t) = _proj(
        x2, S, row(g_mix), w_tok, w_feat, row(g_lat_a), w_uk, w_uv_t, rpad(g_k_a), rpad(g_ks_b), rpad(g_kw_b),
        colv(g_q_a), colv(g_q_b), tm=512)

    n_chunk = S // CMP_STRIDE
    n_cmp = (S - CMP_LEN) // CMP_STRIDE + 1
    xc = kcv.reshape(2 * NSA_KV_HEADS, B, S, HEAD_DIM)
    phi1 = jnp.stack([phi_k1_b, phi_v1_b]).astype(mx)
    hpad = jnp.zeros((CMP_HIDDEN, LANES - HEAD_DIM), f32)
    w2k = jnp.concatenate([phi_k2_b, hpad], axis=1).astype(mx)
    w2v_t = jnp.concatenate([phi_v2_b, hpad], axis=1).T.astype(mx)
    kc, vc_t = _compress(xc, pe_cmp_b.astype(f32), phi1, w2k, w2v_t, rpad(g_kc_b))

    c_i = np.arange(n_chunk)[None, :] * CMP_STRIDE
    j_i = np.arange(MAX_SLC_BLOCKS)[:, None] * SLC_LEN
    ovl = (c_i < j_i + SLC_LEN) & (c_i + CMP_LEN > j_i) & (c_i < n_cmp * CMP_STRIDE) & (j_i < S)
    ovl = jnp.asarray(ovl, f32).astype(mx)

    o_a = _dsa(B, S, qi_t, gw_t, qa_t, ki, ka, va_t)
    o_b = _nsa(B, S, gw_t, qb_t, ks, kw, vs_t, vw_t, kc, vc_t, ovl)

    out = _merge_ffn(x2, o_a, o_b, row(g_mix), w_gate, w_up_a.astype(mx), w_up_b.astype(mx), w_out.astype(mx),
                     row(g_mlp), w_ff1.astype(mx), w_ff2.astype(mx), tm=512)
    return out.reshape(B, S, D)


def kernel(x, g_mix, w_in, g_q_a, g_k_a, g_lat_a, w_uk_a, w_uv_a, g_q_b, g_kc_b, g_ks_b, g_kw_b, pe_cmp_b,
           phi_k1_b, phi_k2_b, phi_v1_b, phi_v2_b, w_up_a, w_up_b, w_out, g_mlp, w_ff1, w_ff2):
    params = (g_mix, w_in, g_q_a, g_k_a, g_lat_a, w_uk_a, w_uv_a, g_q_b, g_kc_b, g_ks_b, g_kw_b, pe_cmp_b,
              phi_k1_b, phi_k2_b, phi_v1_b, phi_v2_b, w_up_a, w_up_b, w_out, g_mlp, w_ff1, w_ff2)
    for l in range(g_mix.shape[0]):
        x = _layer(x, *[p[l] for p in params])
    return x
```

```python
import functools
import math

import numpy as np
import jax
import jax.numpy as jnp
from jax import lax
from jax.experimental import pallas as pl
from jax.experimental.pallas import tpu as pltpu

HEAD_DIM = 64
DSA_HEADS = 8
DSA_LATENT = 128
IDX_HEADS = 8
IDX_DIM = 32
DSA_TOPK_MAX = 256
NSA_HEADS = 8
NSA_KV_HEADS = 2
NSA_GROUP = NSA_HEADS // NSA_KV_HEADS
CMP_LEN = 32
CMP_STRIDE = 16
CMP_HIDDEN = 128
SLC_LEN = 64
SLC_TOPN = 16
WINDOW = 512
FORCE_BONUS = 1e6
Q_BLOCK = 128
RMS_EPS = 1e-6
NEG_INF = -1e30

WIDTH_A = DSA_HEADS * HEAD_DIM
WIDTH_B = NSA_HEADS * HEAD_DIM
NSA_KV_W = NSA_KV_HEADS * HEAD_DIM

LANES = 128
GATE_ROWS = 3 * NSA_HEADS
ALIBI_COL = HEAD_DIM
BIAS_COL = 96
MAX_SLC_BLOCKS = LANES - BIAS_COL
BLOCK_BIAS = 2.0 ** 100
LOG2E = math.log2(math.e)
MXU_DTYPE = jnp.bfloat16
VMEM_LIMIT = 52 * 1024 * 1024
DSA_WIDTH_CLASSES = 8
NSA_WIDTH_CLASSES = 8
TIE_CHUNK = 128
REDUCE_ROWS = 64
QK_SIDE_STEPS = 8

_NT = (((1,), (1,)), ((), ()))


def _alibi_slopes(n_heads):
    return [float(v) for v in np.asarray(
        2.0 ** (-8.0 * np.arange(1, n_heads + 1) / n_heads), dtype=np.float32)]


def _dot(a, b):
    return jnp.dot(a, b, preferred_element_type=jnp.float32)


def _dot_nt(a, b):
    return lax.dot_general(a, b, _NT, preferred_element_type=jnp.float32)


def _rms(x, g):
    ms = jnp.mean(x * x, axis=-1, keepdims=True)
    return x * lax.rsqrt(ms + RMS_EPS) * g


def _params(sem):
    return pltpu.CompilerParams(dimension_semantics=sem, vmem_limit_bytes=VMEM_LIMIT)


def _split3(c):
    c1 = c.astype(MXU_DTYPE).astype(jnp.float32)
    c2 = (c - c1).astype(MXU_DTYPE).astype(jnp.float32)
    c3 = (c - c1 - c2).astype(MXU_DTYPE).astype(jnp.float32)
    return c1, c2, c3


_T_KI = (0, 128)
_T_CA = (128, 256)
_T_KCV = (256, 512)
_T_KS = (512, 640)
_T_KW = (640, 768)
_T_COLS = 768
_F_QI = (0, 256)
_F_GW = (256, 384)
_F_QA = (384, 896)
_F_QB = (896, 1408)
_F_VS = (1408, 1536)
_F_VW = (1536, 1664)
_F_ROWS = 1664


def _proj_kernel(x_ref, gmix_ref, wtok_ref, wfeat_ref, glat_ref, wuk_ref, wuvt_ref, gka_ref, gks_ref, gkw_ref,
                 gqa_ref, gqb_ref,
                 ki_ref, ka_ref, vat_ref, kcv_ref, ks_ref, kw_ref, vst_ref, vwt_ref, qit_ref, gwt_ref,
                 qat_ref, qbt_ref, *, seq_len, w_idx_scale, slopes_a, slopes_b):
    f32 = jnp.float32
    x = x_ref[...]
    tm = x.shape[0]
    h = _rms(x, gmix_ref[...]).astype(MXU_DTYPE)

    def tok(c):
        return _dot(h, wtok_ref[:, c[0]:c[1]])

    def feat(r):
        return _dot_nt(wfeat_ref[r[0]:r[1], :], h)

    s0 = (pl.program_id(0) % (seq_len // tm)) * tm
    spos = s0 + lax.broadcasted_iota(jnp.int32, (tm, LANES), 0)
    lane = lax.broadcasted_iota(jnp.int32, (tm, LANES), 1)
    c1, c2, c3 = _split3(spos.astype(f32) * LOG2E)
    alibi_cols = jnp.where(lane == ALIBI_COL, c1,
                           jnp.where(lane == ALIBI_COL + 1, c2, jnp.where(lane == ALIBI_COL + 2, c3, 0.0)))
    block_onehot = jnp.where(lane - BIAS_COL == lax.shift_right_logical(spos, 6), 1.0, 0.0)

    def key_slab(raw, g):
        ms = jnp.sum(raw * raw, axis=-1, keepdims=True) * (1.0 / HEAD_DIM)
        return raw * lax.rsqrt(ms + RMS_EPS) * g

    p_tok = {c: tok(c) for c in (_T_KI, _T_CA, _T_KCV, _T_KS, _T_KW)}
    p_feat = {r: feat(r) for r in (_F_QI, _F_GW, _F_QA, _F_QB, _F_VS, _F_VW)}

    ki_ref[...] = p_tok[_T_KI].astype(ki_ref.dtype)
    c = _rms(p_tok[_T_CA], glat_ref[...]).astype(MXU_DTYPE)
    ka_ref[...] = (key_slab(_dot(c, wuk_ref[...]), gka_ref[...]) + alibi_cols).astype(ka_ref.dtype)
    va_t = _dot_nt(wuvt_ref[...], c)
    frow = lax.broadcasted_iota(jnp.int32, va_t.shape, 0)
    vat_ref[...] = jnp.where(frow == HEAD_DIM, 1.0, va_t).astype(vat_ref.dtype)
    kcv = p_tok[_T_KCV]
    for j in range(2 * NSA_KV_HEADS):
        kcv_ref[j] = kcv[:, j * HEAD_DIM:(j + 1) * HEAD_DIM]
    low = lane < HEAD_DIM
    for cols, g_ref, out_ref, extra in ((_T_KS, gks_ref, ks_ref, alibi_cols + block_onehot),
                                        (_T_KW, gkw_ref, kw_ref, alibi_cols)):
        pair = p_tok[cols]
        for kh in range(NSA_KV_HEADS):
            raw = pair if kh == 0 else pltpu.roll(pair, HEAD_DIM, 1)
            slab = key_slab(jnp.where(low, raw, 0.0), g_ref[...]) + extra
            out_ref[:, kh * LANES:(kh + 1) * LANES] = slab.astype(out_ref.dtype)

    qit_ref[...] = p_feat[_F_QI].astype(qit_ref.dtype)
    gw = p_feat[_F_GW]
    grow = lax.broadcasted_iota(jnp.int32, gw.shape, 0)
    gwt_ref[...] = jnp.where(grow < GATE_ROWS, jax.nn.sigmoid(gw), gw * w_idx_scale)
    erow = lax.broadcasted_iota(jnp.int32, (HEAD_DIM, tm), 0)
    q_scale = HEAD_DIM ** -0.5 * LOG2E
    for rows, g_ref, out_ref, slopes in ((_F_QA, gqa_ref, qat_ref, slopes_a), (_F_QB, gqb_ref, qbt_ref, slopes_b)):
        q_all = p_feat[rows]
        for hd in range(len(slopes)):
            q = q_all[hd * HEAD_DIM:(hd + 1) * HEAD_DIM]
            ms = jnp.mean(q * q, axis=0, keepdims=True)
            q = q * lax.rsqrt(ms + RMS_EPS) * g_ref[...] * q_scale
            out_ref[hd * LANES:hd * LANES + HEAD_DIM, :] = q.astype(out_ref.dtype)
            extra = jnp.where(erow < 3, slopes[hd], 0.0)
            out_ref[hd * LANES + HEAD_DIM:(hd + 1) * LANES, :] = extra.astype(out_ref.dtype)
    ones_row = jnp.where(erow == 0, 1.0, 0.0)
    for rows, out_ref in ((_F_VS, vst_ref), (_F_VW, vwt_ref)):
        v_t = p_feat[rows]
        for kh in range(NSA_KV_HEADS):
            out_ref[kh * LANES:kh * LANES + HEAD_DIM, :] = v_t[kh * HEAD_DIM:(kh + 1) * HEAD_DIM].astype(out_ref.dtype)
            out_ref[kh * LANES + HEAD_DIM:(kh + 1) * LANES, :] = ones_row.astype(out_ref.dtype)


def _proj(x2, seq_len, g_mix, w_tok, w_feat, g_lat, w_uk, w_uv_t, gk_a, gks, gkw, gq_a, gq_b, tm):
    T, D = x2.shape
    tokm = lambda w: pl.BlockSpec((tm, w), lambda i: (i, 0))
    featm = lambda r: pl.BlockSpec((r, tm), lambda i: (0, i))
    full = lambda a: pl.BlockSpec(a.shape, lambda i: (0,) * a.ndim)
    f32 = jnp.float32
    mx = MXU_DTYPE
    out_shape = (
        jax.ShapeDtypeStruct((T, LANES), mx),
        jax.ShapeDtypeStruct((T, LANES), mx),
        jax.ShapeDtypeStruct((LANES, T), mx),
        jax.ShapeDtypeStruct((2 * NSA_KV_HEADS, T, HEAD_DIM), f32),
        jax.ShapeDtypeStruct((T, NSA_KV_HEADS * LANES), mx),
        jax.ShapeDtypeStruct((T, NSA_KV_HEADS * LANES), mx),
        jax.ShapeDtypeStruct((NSA_KV_HEADS * LANES, T), mx),
        jax.ShapeDtypeStruct((NSA_KV_HEADS * LANES, T), mx),
        jax.ShapeDtypeStruct((IDX_HEADS * IDX_DIM, T), mx),
        jax.ShapeDtypeStruct((LANES, T), f32),
        jax.ShapeDtypeStruct((DSA_HEADS * LANES, T), mx),
        jax.ShapeDtypeStruct((NSA_HEADS * LANES, T), mx),
    )
    out_specs = (
        tokm(LANES), tokm(LANES), featm(LANES),
        pl.BlockSpec((2 * NSA_KV_HEADS, tm, HEAD_DIM), lambda i: (0, i, 0)),
        tokm(NSA_KV_HEADS * LANES), tokm(NSA_KV_HEADS * LANES),
        featm(NSA_KV_HEADS * LANES), featm(NSA_KV_HEADS * LANES),
        featm(IDX_HEADS * IDX_DIM), featm(LANES), featm(DSA_HEADS * LANES), featm(NSA_HEADS * LANES),
    )
    kern = functools.partial(
        _proj_kernel, seq_len=seq_len, w_idx_scale=IDX_HEADS ** -0.5 * IDX_DIM ** -0.5,
        slopes_a=_alibi_slopes(DSA_HEADS), slopes_b=_alibi_slopes(NSA_HEADS))
    ins = (x2, g_mix, w_tok, w_feat, g_lat, w_uk, w_uv_t, gk_a, gks, gkw, gq_a, gq_b)
    return pl.pallas_call(
        kern,
        grid=(T // tm,),
        in_specs=[tokm(D)] + [full(a) for a in ins[1:]],
        out_specs=out_specs,
        out_shape=out_shape,
        compiler_params=_params(("parallel",)),
        name="proj",
    )(*ins)


def _compress_kernel(xk_ref, xv_ref, pe_ref, w1_ref, w2k_ref, w2vt_ref, gkc_ref, kc_ref, vct_ref):
    n = kc_ref.shape[0]

    def hidden(x_ref, kv):
        a = jnp.zeros((n, CMP_HIDDEN), jnp.float32)
        b = jnp.zeros((n, CMP_HIDDEN), jnp.float32)
        for l in range(CMP_STRIDE):
            x_l = x_ref[pl.ds(l, n, stride=CMP_STRIDE), :]
            for acc_is_b, row in ((False, l), (True, CMP_STRIDE + l)):
                lhs = (x_l + pe_ref[row:row + 1, :]).astype(MXU_DTYPE)
                part = _dot(lhs, w1_ref[kv, row * HEAD_DIM:(row + 1) * HEAD_DIM, :])
                if acc_is_b:
                    b = b + part
                else:
                    a = a + part
        return jax.nn.gelu(a + pltpu.roll(b, n - 1, 0)).astype(MXU_DTYPE)

    yk = _dot(hidden(xk_ref, 0), w2k_ref[...])
    ms = jnp.sum(yk * yk, axis=-1, keepdims=True) * (1.0 / HEAD_DIM)
    kc_ref[...] = (yk * lax.rsqrt(ms + RMS_EPS) * gkc_ref[...]).astype(kc_ref.dtype)
    vct_ref[...] = _dot_nt(w2vt_ref[...], hidden(xv_ref, 1)).astype(vct_ref.dtype)


def _compress(xc, pe, phi1, w2k, w2v_t, g_kc):
    _, B, S, w = xc.shape
    n = S // CMP_STRIDE
    full = lambda a: pl.BlockSpec(a.shape, lambda b, kh: (0,) * a.ndim)
    return pl.pallas_call(
        _compress_kernel,
        grid=(B, NSA_KV_HEADS),
        in_specs=[
            pl.BlockSpec((None, None, S, w), lambda b, kh: (kh, b, 0, 0)),
            pl.BlockSpec((None, None, S, w), lambda b, kh: (NSA_KV_HEADS + kh, b, 0, 0)),
            full(pe), full(phi1), full(w2k), full(w2v_t), full(g_kc),
        ],
        out_specs=(pl.BlockSpec((None, None, n, LANES), lambda b, kh: (b, kh, 0, 0)),
                   pl.BlockSpec((None, None, LANES, n), lambda b, kh: (b, kh, 0, 0))),
        out_shape=(jax.ShapeDtypeStruct((B, NSA_KV_HEADS, n, LANES), MXU_DTYPE),
                   jax.ShapeDtypeStruct((B, NSA_KV_HEADS, LANES, n), MXU_DTYPE)),
        compiler_params=_params(("parallel", "parallel")),
        name="compress",
    )(xc, xc, pe, phi1, w2k, w2v_t, g_kc)


_INT_MIN = -2 ** 31


def _key_to_float(u):
    key = u ^ jnp.int32(_INT_MIN)
    bits = jnp.where(key >= 0, key, key ^ jnp.int32(0x7FFFFFFF))
    return lax.bitcast_convert_type(bits, jnp.float32)


def _col_reduce(x, op, final):
    rows = REDUCE_ROWS
    acc = x
    if x.shape[0] > rows and x.shape[0] % rows == 0:
        acc = x[:rows]
        for j in range(1, x.shape[0] // rows):
            acc = op(acc, x[j * rows:(j + 1) * rows])
    return final(acc.astype(jnp.float32), axis=0, keepdims=True)


def _col_sum(x):
    return _col_reduce(x, jnp.add, jnp.sum)


def _col_max(x):
    return _col_reduce(x, jnp.maximum, jnp.max)


def _kth_largest(score, k, side_steps=0, side_work=None):
    def body(it, u):
        u_try = u | lax.shift_left(jnp.int32(1), 31 - it)
        cnt = _col_sum(jnp.where(score >= _key_to_float(u_try), 1.0, 0.0))
        return jnp.where(cnt >= k, u_try, u)

    def body_with_side_work(j, u):
        side_work(j)
        for i in range(32 // side_steps):
            u = body(j * (32 // side_steps) + i, u)
        return u

    u = jnp.zeros((1, score.shape[1]), jnp.int32)
    if side_steps:
        u = lax.fori_loop(0, side_steps, body_with_side_work, u)
    else:
        u = lax.fori_loop(0, 32, body, u)
    thr = _key_to_float(u)
    return jnp.where(thr != thr, -jnp.inf, thr)


def _lane_tile(x, n):
    return jnp.concatenate([x] * n, axis=1)


def _softmax_pv(parts):
    return _softmax_pv_staged([parts])[0]


def _softmax_pv_staged(problems):
    maxes = []
    for parts in problems:
        m = None
        for s, _ in parts:
            m_p = _col_max(s)
            m = m_p if m is None else jnp.maximum(m, m_p)
        maxes.append(m)
    outs = []
    for parts, m in zip(problems, maxes):
        out = None
        for s, v_t in parts:
            o_p = _dot(v_t, jnp.exp2(s - m).astype(MXU_DTYPE))
            out = o_p if out is None else out + o_p
        outs.append(out)
    return outs


def _width_classes(n_qblk, n_classes):
    n_cls = n_classes if n_qblk % n_classes == 0 else 1
    per = n_qblk // n_cls
    return [(c * per, (c + 1) * per, (c + 1) * per * Q_BLOCK) for c in range(n_cls)]


def _dsa_body(W, qit_ref, gwt_ref, qat_ref, ki_ref, ka_ref, vat_ref, o_ref, sel_ref, qk_ref,*, topk):
    f32 = jnp.float32
    QB = Q_BLOCK
    t = pl.program_id(1) * QB + lax.broadcasted_iota(jnp.int32, (1, QB), 1)
    spos = lax.broadcasted_iota(jnp.int32, (W, QB), 0)
    causal = spos <= t

    ki = ki_ref[:W, :]
    G = 4
    kpad = jnp.zeros((LANES - IDX_DIM, G * QB), MXU_DTYPE)
    score = jnp.zeros((W, QB), f32)
    for hg in range(IDX_HEADS // G):
        q_idx = jnp.concatenate(
            [qit_ref[(hg * G + g) * IDX_DIM:(hg * G + g + 1) * IDX_DIM, :] for g in range(G)], axis=1)
        logits = _dot(ki, jnp.concatenate([q_idx, kpad], axis=0))
        for g in range(G):
            w_h = gwt_ref[GATE_ROWS + hg * G + g:GATE_ROWS + hg * G + g + 1, :]
            score = score + jnp.maximum(logits[:, g * QB:(g + 1) * QB], 0.0) * w_h
    score = jnp.where(causal, score, -jnp.inf)

    q_all = jnp.concatenate([qat_ref[h * LANES:(h + 1) * LANES, :] for h in range(DSA_HEADS)], axis=1)
    chunk = W // QK_SIDE_STEPS

    def qk_chunk(it):
        r0 = pl.multiple_of(it * chunk, chunk)
        qk_ref[pl.ds(r0, chunk), :] = _dot(ka_ref[pl.ds(r0, chunk), :], q_all)

    lowest = float(jnp.finfo(jnp.float32).min)
    if W <= topk:
        qk_ref[:W, :] = _dot(ka_ref[:W, :], q_all)
        thr = jnp.full((1, QB), lowest, f32)
    else:
        thr = _kth_largest(score, float(topk), QK_SIDE_STEPS, qk_chunk)
        thr = jnp.maximum(thr, lowest)
    ge = score >= thr
    sel_ref[:W, :] = jnp.where(ge, 0.0, NEG_INF)

    def repair_ties():
        gt = score > thr
        eq = score == thr
        need = float(topk) - _col_sum(jnp.where(gt, 1.0, 0.0))
        cw = TIE_CHUNK
        r = lax.broadcasted_iota(jnp.int32, (cw, cw), 0)
        c = lax.broadcasted_iota(jnp.int32, (cw, cw), 1)
        lower = jnp.where(c <= r, 1.0, 0.0).astype(MXU_DTYPE)
        carry = jnp.zeros((1, QB), f32)
        for j in range(W // cw):
            sl = slice(j * cw, (j + 1) * cw)
            eq_j = jnp.where(eq[sl], 1.0, 0.0)
            prefix = _dot(lower, eq_j.astype(MXU_DTYPE)) + carry
            keep = gt[sl] | (eq[sl] & (prefix <= need))
            sel_ref[sl, :] = jnp.where(keep, 0.0, NEG_INF)
            carry = carry + jnp.sum(eq_j, axis=0, keepdims=True)

    if W > topk:
        pl.when(jnp.max(_col_sum(jnp.where(ge, 1.0, 0.0))) > float(topk))(repair_ties)

    G = 4
    va_t = vat_ref[:, :W]
    for hg in range(DSA_HEADS // G):
        s = qk_ref[:W, hg * G * QB:(hg + 1) * G * QB] + _lane_tile(sel_ref[:W, :], G)
        o_t = _softmax_pv([(s, va_t)])
        o_t = o_t[:HEAD_DIM] * (1.0 / o_t[HEAD_DIM:HEAD_DIM + 1])
        for g2 in range(G // 2):
            pair = jnp.concatenate([o_t[:, (2 * g2) * QB:(2 * g2 + 1) * QB],
                                    o_t[:, (2 * g2 + 1) * QB:(2 * g2 + 2) * QB]], axis=0)
            col = (hg * G + 2 * g2) * HEAD_DIM
            o_ref[:, col:col + 2 * HEAD_DIM] = pair.T.astype(o_ref.dtype)


def _dsa_kernel(qit_ref, gwt_ref, qat_ref, ki_ref, ka_ref, vat_ref, o_ref, sel_ref, qk_ref,*, topk, classes):
    qblk = pl.program_id(1)
    for lo, hi, W in classes:
        @pl.when((qblk >= lo) & (qblk < hi))
        def _(W=W):
            _dsa_body(W, qit_ref, gwt_ref, qat_ref, ki_ref, ka_ref, vat_ref, o_ref, sel_ref, qk_ref,topk=topk)


def _dsa(B, S, qi_t, gw_t, qa_t, ki, ka, va_t):
    nq = S // Q_BLOCK
    topk = min(DSA_TOPK_MAX, S // 4)
    fblk = lambda r: pl.BlockSpec((r, Q_BLOCK), lambda b, i: (0, b * nq + i))
    tseq = lambda w: pl.BlockSpec((S, w), lambda b, i: (b, 0))
    fseq = lambda r: pl.BlockSpec((r, S), lambda b, i: (0, b))
    return pl.pallas_call(
        functools.partial(_dsa_kernel, topk=topk, classes=_width_classes(nq, DSA_WIDTH_CLASSES)),
        grid=(B, nq),
        in_specs=[fblk(IDX_HEADS * IDX_DIM), fblk(LANES), fblk(DSA_HEADS * LANES),
                  tseq(LANES), tseq(LANES), fseq(LANES)],
        out_specs=pl.BlockSpec((Q_BLOCK, WIDTH_A), lambda b, i: (b * nq + i, 0)),
        out_shape=jax.ShapeDtypeStruct((B * S, WIDTH_A), MXU_DTYPE),
        scratch_shapes=[pltpu.VMEM((S, Q_BLOCK), jnp.float32),
                        pltpu.VMEM((S, DSA_HEADS * Q_BLOCK), jnp.float32)],
        compiler_params=_params(("parallel", "arbitrary")),
        name="dsa",
    )(qi_t, gw_t, qa_t, ki, ka, va_t)


def _nsa_body(W, first_blk, gwt_ref, qbt_ref, ks_ref, kw_ref, vst_ref, vwt_ref, kc_ref, vct_ref, ovl_ref, o_ref,
              *, seq_len):
    f32 = jnp.float32
    QB, G = Q_BLOCK, NSA_GROUP
    n_slc = seq_len // SLC_LEN
    n_top = min(SLC_TOPN, n_slc)
    win_len = WINDOW + QB
    nc = kc_ref.shape[1]

    t0 = pl.program_id(1) * QB
    t = t0 + lax.broadcasted_iota(jnp.int32, (1, QB), 1)
    t4 = _lane_tile(t, G)
    valid_c = lax.broadcasted_iota(jnp.int32, (nc, G * QB), 0) * CMP_STRIDE + (CMP_LEN - 1) <= t4
    past = first_blk * QB
    causal_tail = past + lax.broadcasted_iota(jnp.int32, (W - past, G * QB), 0) <= t4
    w0 = pl.multiple_of(jnp.maximum(t0 - WINDOW, 0), QB)
    full_window = past >= WINDOW
    if full_window:
        r_minus_lane = lax.broadcasted_iota(jnp.int32, (QB, G * QB), 0) - (t4 - t0)
        w_head_ok = r_minus_lane > 0
        w_tail_ok = r_minus_lane <= 0
    else:
        wd = t4 - (w0 + lax.broadcasted_iota(jnp.int32, (win_len, G * QB), 0))
        wvalid = (wd >= 0) & (wd < WINDOW)
    jblk = lax.broadcasted_iota(jnp.int32, (MAX_SLC_BLOCKS, QB), 0)
    cur = lax.shift_right_logical(t, 6)
    future = jblk > cur
    forced = (jblk == 0) | (jblk == cur) | (jblk == cur - 1)
    ovl = ovl_ref[...]

    KH = range(NSA_KV_HEADS)
    ksl = [slice(kh * LANES, (kh + 1) * LANES) for kh in KH]
    q = [jnp.concatenate([qbt_ref[(kh * G + g) * LANES:(kh * G + g + 1) * LANES, :] for g in range(G)], axis=1)
         for kh in KH]

    s_c = [jnp.where(valid_c, _dot(kc_ref[kh], q[kh]), NEG_INF) for kh in KH]
    s_w = [_dot(kw_ref[pl.ds(w0, win_len), ksl[kh]], q[kh]) for kh in KH]
    e_c = [jnp.where(valid_c, jnp.exp2(s_c[kh] - jnp.max(s_c[kh], axis=0, keepdims=True)), 0.0) for kh in KH]
    l_c = [jnp.sum(e_c[kh], axis=0, keepdims=True) for kh in KH]
    p_c = [e_c[kh] * (1.0 / jnp.where(l_c[kh] > 0.0, l_c[kh], 1.0)) for kh in KH]
    o_cmp = [_dot(vct_ref[kh], p_c[kh].astype(MXU_DTYPE)) for kh in KH]

    val = []
    for kh in KH:
        pc = p_c[kh]
        p_sum = pc[:, 0:QB] + pc[:, QB:2 * QB] + pc[:, 2 * QB:3 * QB] + pc[:, 3 * QB:4 * QB]
        p_hi = p_sum.astype(MXU_DTYPE)
        p_lo = (p_sum - p_hi.astype(f32)).astype(MXU_DTYPE)
        imp = _dot(ovl, p_hi) + _dot(ovl, p_lo)
        val.append(jnp.where(future, -jnp.inf, imp + jnp.where(forced, FORCE_BONUS, 0.0)))

    win_problems = []
    for kh in KH:
        v_w = vwt_ref[ksl[kh], pl.ds(w0, win_len)]
        if full_window:
            parts = [(jnp.where(w_head_ok, s_w[kh][:QB], NEG_INF), v_w[:, :QB]),
                     (s_w[kh][QB:WINDOW], v_w[:, QB:WINDOW]),
                     (jnp.where(w_tail_ok, s_w[kh][WINDOW:], NEG_INF), v_w[:, WINDOW:])]
        else:
            parts = [(jnp.where(wvalid, s_w[kh], NEG_INF), v_w)]
        win_problems.append(parts)

    rank = [jnp.zeros(val[kh].shape, f32) for kh in KH]
    for i in range(min(n_slc, W // SLC_LEN)):
        for kh in KH:
            vi = val[kh][i:i + 1, :]
            rank[kh] = rank[kh] + jnp.where((vi > val[kh]) | ((vi == val[kh]) & (jblk > i)), 1.0, 0.0)

    s_s = []
    for kh in KH:
        bias = jnp.where((rank[kh] < n_top) & (jblk < n_slc), 0.0, -BLOCK_BIAS).astype(MXU_DTYPE)
        q_sel = jnp.concatenate([q[kh][:BIAS_COL], _lane_tile(bias, G)], axis=0)
        s_s.append(_dot(ks_ref[:W, ksl[kh]], q_sel))
    o_win = _softmax_pv_staged(win_problems)
    slc_problems = []
    for kh in KH:
        parts = [(jnp.where(causal_tail, s_s[kh][past:], NEG_INF), vst_ref[ksl[kh], past:W])]
        if past:
            parts.append((s_s[kh][:past], vst_ref[ksl[kh], :past]))
        slc_problems.append(parts)
    o_slc = _softmax_pv_staged(slc_problems)

    for kh in KH:
        inv_s = 1.0 / o_slc[kh][HEAD_DIM:HEAD_DIM + 1]
        inv_w = 1.0 / o_win[kh][HEAD_DIM:HEAD_DIM + 1]
        heads = []
        for g in range(G):
            h = kh * G + g
            ls = slice(g * QB, (g + 1) * QB)
            g0 = gwt_ref[3 * h + 0:3 * h + 1, :]
            g1 = gwt_ref[3 * h + 1:3 * h + 2, :] * inv_s[:, ls]
            g2 = gwt_ref[3 * h + 2:3 * h + 3, :] * inv_w[:, ls]
            heads.append(g0 * o_cmp[kh][:HEAD_DIM, ls] + g1 * o_slc[kh][:HEAD_DIM, ls]
                         + g2 * o_win[kh][:HEAD_DIM, ls])
        for g2_ in range(G // 2):
            pair = jnp.concatenate([heads[2 * g2_], heads[2 * g2_ + 1]], axis=0)
            col = (kh * G + 2 * g2_) * HEAD_DIM
            o_ref[:, col:col + 2 * HEAD_DIM] = pair.T.astype(o_ref.dtype)


def _nsa_kernel(gwt_ref, qbt_ref, ks_ref, kw_ref, vst_ref, vwt_ref, kc_ref, vct_ref, ovl_ref, o_ref,
                *, seq_len, classes):
    qblk = pl.program_id(1)
    for lo, hi, W in classes:
        @pl.when((qblk >= lo) & (qblk < hi))
        def _(W=W, lo=lo):
            _nsa_body(W, lo, gwt_ref, qbt_ref, ks_ref, kw_ref, vst_ref, vwt_ref, kc_ref, vct_ref, ovl_ref, o_ref,
                      seq_len=seq_len)


def _nsa(B, S, gw_t, qb_t, ks, kw, vs_t, vw_t, kc, vc_t, ovl):
    nq = S // Q_BLOCK
    nc = kc.shape[2]
    kvw = NSA_KV_HEADS * LANES
    fblk = lambda r: pl.BlockSpec((r, Q_BLOCK), lambda b, i: (0, b * nq + i))
    tseq = lambda w: pl.BlockSpec((S, w), lambda b, i: (b, 0))
    fseq = lambda r: pl.BlockSpec((r, S), lambda b, i: (0, b))
    return pl.pallas_call(
        functools.partial(_nsa_kernel, seq_len=S, classes=_width_classes(nq, NSA_WIDTH_CLASSES)),
        grid=(B, nq),
        in_specs=[fblk(LANES), fblk(NSA_HEADS * LANES), tseq(kvw), tseq(kvw), fseq(kvw), fseq(kvw),
                  pl.BlockSpec((None, NSA_KV_HEADS, nc, LANES), lambda b, i: (b, 0, 0, 0)),
                  pl.BlockSpec((None, NSA_KV_HEADS, LANES, nc), lambda b, i: (b, 0, 0, 0)),
                  pl.BlockSpec(ovl.shape, lambda b, i: (0, 0))],
        out_specs=pl.BlockSpec((Q_BLOCK, WIDTH_B), lambda b, i: (b * nq + i, 0)),
        out_shape=jax.ShapeDtypeStruct((B * S, WIDTH_B), MXU_DTYPE),
        compiler_params=_params(("parallel", "arbitrary")),
        name="nsa",
    )(gw_t, qb_t, ks, kw, vs_t, vw_t, kc, vc_t, ovl)


def _merge_ffn_kernel(x_ref, oa_ref, ob_ref, gmix_ref, wg_ref, wua_ref, wub_ref, wo_ref,
                      gmlp_ref, w1_ref, w2_ref, o_ref, *, chunk):
    x = x_ref[...]
    D = x.shape[1]
    h = _rms(x, gmix_ref[...]).astype(MXU_DTYPE)
    ga = jax.nn.sigmoid(_dot(h, wg_ref[:, :D]))
    gb = jax.nn.sigmoid(_dot(h, wg_ref[:, D:]))
    merged = ga * _dot(oa_ref[...], wua_ref[...]) + gb * _dot(ob_ref[...], wub_ref[...])
    x1 = x + _dot(merged.astype(MXU_DTYPE), wo_ref[...])
    h2 = _rms(x1, gmlp_ref[...]).astype(MXU_DTYPE)
    acc = x1
    for c in range(w1_ref.shape[1] // chunk):
        u = jnp.maximum(_dot(h2, w1_ref[:, c * chunk:(c + 1) * chunk]), 0.0)
        acc = acc + _dot((u * u).astype(MXU_DTYPE), w2_ref[c * chunk:(c + 1) * chunk, :])
    o_ref[...] = acc


def _merge_ffn(x2, oa, ob, g_mix, w_gate, w_up_a, w_up_b, w_out, g_mlp, w1, w2, tm):
    T, D = x2.shape
    row = lambda w: pl.BlockSpec((tm, w), lambda i: (i, 0))
    full = lambda a: pl.BlockSpec(a.shape, lambda i: (0,) * a.ndim, pipeline_mode=pl.Buffered(1))
    ins = (x2, oa, ob, g_mix, w_gate, w_up_a, w_up_b, w_out, g_mlp, w1, w2)
    return pl.pallas_call(
        functools.partial(_merge_ffn_kernel, chunk=1024),
        grid=(T // tm,),
        in_specs=[row(D), row(WIDTH_A), row(WIDTH_B)] + [full(a) for a in ins[3:]],
        out_specs=row(D),
        out_shape=jax.ShapeDtypeStruct((T, D), jnp.float32),
        compiler_params=_params(("parallel",)),
        name="merge_ffn",
    )(*ins)


def _layer(x, g_mix, w_in, g_q_a, g_k_a, g_lat_a, w_uk_a, w_uv_a, g_q_b, g_kc_b, g_ks_b, g_kw_b,
           pe_cmp_b, phi_k1_b, phi_k2_b, phi_v1_b, phi_v2_b, w_up_a, w_up_b, w_out, g_mlp, w_ff1, w_ff2):
    B, S, D = x.shape
    T = B * S
    f32 = jnp.float32
    mx = MXU_DTYPE
    n_slc = S // SLC_LEN
    assert n_slc <= MAX_SLC_BLOCKS and S % (DSA_WIDTH_CLASSES * TIE_CHUNK) == 0 and S >= WINDOW + Q_BLOCK
    col_sizes = (WIDTH_A, DSA_LATENT, IDX_HEADS * IDX_DIM, IDX_DIM, IDX_HEADS,
                 WIDTH_B, 6 * NSA_KV_W, 3 * NSA_HEADS, 2 * D)
    offs = np.cumsum((0,) + col_sizes)
    w_qa, w_ca, w_qi, w_ki, w_wi, w_qb, w_kvb, w_gb, w_gm = [
        w_in[:, offs[i]:offs[i + 1]] for i in range(len(col_sizes))]
    w_kc, w_vc, w_ks, w_vs, w_kw, w_vw = [w_kvb[:, j * NSA_KV_W:(j + 1) * NSA_KV_W] for j in range(6)]
    w_tok = jnp.concatenate(
        [w_ki, jnp.zeros((D, LANES - IDX_DIM), f32), w_ca, w_kc, w_vc, w_ks, w_kw], axis=1).astype(mx)
    assert w_tok.shape[1] == _T_COLS
    gw_pad = jnp.zeros((D, LANES - GATE_ROWS - IDX_HEADS), f32)
    w_feat = jnp.concatenate(
        [w_qi, w_gb, w_wi, gw_pad, w_qa, w_qb, w_vs, w_vw], axis=1).T.astype(mx)
    assert w_feat.shape[0] == _F_ROWS
    w_gate = w_gm.astype(mx)
    zpad = jnp.zeros((DSA_LATENT, LANES - HEAD_DIM), f32)
    w_uk = jnp.concatenate([w_uk_a, zpad], axis=1).astype(mx)
    w_uv_t = jnp.concatenate([w_uv_a, zpad], axis=1).T.astype(mx)
    row = lambda g: g.reshape(1, -1).astype(f32)
    rpad = lambda g: jnp.concatenate([row(g), jnp.zeros((1, LANES - HEAD_DIM), f32)], axis=1)
    colv = lambda g: g.reshape(-1, 1).astype(f32)

    x2 = x.reshape(T, D)
    (ki, ka, va_t, kcv, ks, kw, vs_t, vw_t, qi_t, gw_t, qa_t, qb_t) = _proj(
        x2, S, row(g_mix), w_tok, w_feat, row(g_lat_a), w_uk, w_uv_t, rpad(g_k_a), rpad(g_ks_b), rpad(g_kw_b),
        colv(g_q_a), colv(g_q_b), tm=512)

    n_chunk = S // CMP_STRIDE
    n_cmp = (S - CMP_LEN) // CMP_STRIDE + 1
    xc = kcv.reshape(2 * NSA_KV_HEADS, B, S, HEAD_DIM)
    phi1 = jnp.stack([phi_k1_b, phi_v1_b]).astype(mx)
    hpad = jnp.zeros((CMP_HIDDEN, LANES - HEAD_DIM), f32)
    w2k = jnp.concatenate([phi_k2_b, hpad], axis=1).astype(mx)
    w2v_t = jnp.concatenate([phi_v2_b, hpad], axis=1).T.astype(mx)
    kc, vc_t = _compress(xc, pe_cmp_b.astype(f32), phi1, w2k, w2v_t, rpad(g_kc_b))

    c_i = np.arange(n_chunk)[None, :] * CMP_STRIDE
    j_i = np.arange(MAX_SLC_BLOCKS)[:, None] * SLC_LEN
    ovl = (c_i < j_i + SLC_LEN) & (c_i + CMP_LEN > j_i) & (c_i < n_cmp * CMP_STRIDE) & (j_i < S)
    ovl = jnp.asarray(ovl, f32).astype(mx)

    o_a = _dsa(B, S, qi_t, gw_t, qa_t, ki, ka, va_t)
    o_b = _nsa(B, S, gw_t, qb_t, ks, kw, vs_t, vw_t, kc, vc_t, ovl)

    out = _merge_ffn(x2, o_a, o_b, row(g_mix), w_gate, w_up_a.astype(mx), w_up_b.astype(mx), w_out.astype(mx),
                     row(g_mlp), w_ff1.astype(mx), w_ff2.astype(mx), tm=512)
    return out.reshape(B, S, D)


def kernel(x, g_mix, w_in, g_q_a, g_k_a, g_lat_a, w_uk_a, w_uv_a, g_q_b, g_kc_b, g_ks_b, g_kw_b, pe_cmp_b,
           phi_k1_b, phi_k2_b, phi_v1_b, phi_v2_b, w_up_a, w_up_b, w_out, g_mlp, w_ff1, w_ff2):
    params = (g_mix, w_in, g_q_a, g_k_a, g_lat_a, w_uk_a, w_uv_a, g_q_b, g_kc_b, g_ks_b, g_kw_b, pe_cmp_b,
              phi_k1_b, phi_k2_b, phi_v1_b, phi_v2_b, w_up_a, w_up_b, w_out, g_mlp, w_ff1, w_ff2)
    for l in range(g_mix.shape[0]):
        x = _layer(x, *[p[l] for p in params])
    return x
```

```python
import functools
import math

import numpy as np
import jax
import jax.numpy as jnp
from jax import lax
from jax.experimental import pallas as pl
from jax.experimental.pallas import tpu as pltpu

HEAD_DIM = 64
DSA_HEADS = 8
DSA_LATENT = 128
IDX_HEADS = 8
IDX_DIM = 32
DSA_TOPK_MAX = 256
NSA_HEADS = 8
NSA_KV_HEADS = 2
NSA_GROUP = NSA_HEADS // NSA_KV_HEADS
CMP_LEN = 32
CMP_STRIDE = 16
CMP_HIDDEN = 128
SLC_LEN = 64
SLC_TOPN = 16
WINDOW = 512
FORCE_BONUS = 1e6
Q_BLOCK = 128
RMS_EPS = 1e-6
NEG_INF = -1e30

WIDTH_A = DSA_HEADS * HEAD_DIM
WIDTH_B = NSA_HEADS * HEAD_DIM
NSA_KV_W = NSA_KV_HEADS * HEAD_DIM

LANES = 128
GATE_ROWS = 3 * NSA_HEADS
ALIBI_COL = HEAD_DIM
BIAS_COL = 96
MAX_SLC_BLOCKS = LANES - BIAS_COL
BLOCK_BIAS = 2.0 ** 100
LOG2E = math.log2(math.e)
MXU_DTYPE = jnp.bfloat16
VMEM_LIMIT = 52 * 1024 * 1024
DSA_WIDTH_CLASSES = 8
NSA_WIDTH_CLASSES = 8
TIE_CHUNK = 128
REDUCE_ROWS = 64
QK_SIDE_STEPS = 4

_NT = (((1,), (1,)), ((), ()))


def _alibi_slopes(n_heads):
    return [float(v) for v in np.asarray(
        2.0 ** (-8.0 * np.arange(1, n_heads + 1) / n_heads), dtype=np.float32)]


def _dot(a, b):
    return jnp.dot(a, b, preferred_element_type=jnp.float32)


def _dot_nt(a, b):
    return lax.dot_general(a, b, _NT, preferred_element_type=jnp.float32)


def _rms(x, g):
    ms = jnp.mean(x * x, axis=-1, keepdims=True)
    return x * lax.rsqrt(ms + RMS_EPS) * g


def _params(sem):
    return pltpu.CompilerParams(dimension_semantics=sem, vmem_limit_bytes=VMEM_LIMIT)


def _split3(c):
    c1 = c.astype(MXU_DTYPE).astype(jnp.float32)
    c2 = (c - c1).astype(MXU_DTYPE).astype(jnp.float32)
    c3 = (c - c1 - c2).astype(MXU_DTYPE).astype(jnp.float32)
    return c1, c2, c3


_T_KI = (0, 128)
_T_CA = (128, 256)
_T_KCV = (256, 512)
_T_KS = (512, 640)
_T_KW = (640, 768)
_T_COLS = 768
_F_QI = (0, 256)
_F_GW = (256, 384)
_F_QA = (384, 896)
_F_QB = (896, 1408)
_F_VS = (1408, 1536)
_F_VW = (1536, 1664)
_F_ROWS = 1664


def _proj_kernel(x_ref, gmix_ref, wtok_ref, wfeat_ref, glat_ref, wuk_ref, wuvt_ref, gka_ref, gks_ref, gkw_ref,
                 gqa_ref, gqb_ref,
                 ki_ref, ka_ref, vat_ref, kcv_ref, ks_ref, kw_ref, vst_ref, vwt_ref, qit_ref, gwt_ref,
                 qat_ref, qbt_ref, *, seq_len, w_idx_scale, slopes_a, slopes_b):
    f32 = jnp.float32
    x = x_ref[...]
    tm = x.shape[0]
    h = _rms(x, gmix_ref[...]).astype(MXU_DTYPE)

    def tok(c):
        return _dot(h, wtok_ref[:, c[0]:c[1]])

    def feat(r):
        return _dot_nt(wfeat_ref[r[0]:r[1], :], h)

    s0 = (pl.program_id(0) % (seq_len // tm)) * tm
    spos = s0 + lax.broadcasted_iota(jnp.int32, (tm, LANES), 0)
    lane = lax.broadcasted_iota(jnp.int32, (tm, LANES), 1)
    c1, c2, c3 = _split3(spos.astype(f32) * LOG2E)
    alibi_cols = jnp.where(lane == ALIBI_COL, c1,
                           jnp.where(lane == ALIBI_COL + 1, c2, jnp.where(lane == ALIBI_COL + 2, c3, 0.0)))
    block_onehot = jnp.where(lane - BIAS_COL == lax.shift_right_logical(spos, 6), 1.0, 0.0)

    def key_slab(raw, g):
        ms = jnp.sum(raw * raw, axis=-1, keepdims=True) * (1.0 / HEAD_DIM)
        return raw * lax.rsqrt(ms + RMS_EPS) * g

    p_tok = {c: tok(c) for c in (_T_KI, _T_CA, _T_KCV, _T_KS, _T_KW)}
    p_feat = {r: feat(r) for r in (_F_QI, _F_GW, _F_QA, _F_QB, _F_VS, _F_VW)}

    ki_ref[...] = p_tok[_T_KI].astype(ki_ref.dtype)
    c = _rms(p_tok[_T_CA], glat_ref[...]).astype(MXU_DTYPE)
    ka_ref[...] = (key_slab(_dot(c, wuk_ref[...]), gka_ref[...]) + alibi_cols).astype(ka_ref.dtype)
    va_t = _dot_nt(wuvt_ref[...], c)
    frow = lax.broadcasted_iota(jnp.int32, va_t.shape, 0)
    vat_ref[...] = jnp.where(frow == HEAD_DIM, 1.0, va_t).astype(vat_ref.dtype)
    kcv = p_tok[_T_KCV]
    for j in range(2 * NSA_KV_HEADS):
        kcv_ref[j] = kcv[:, j * HEAD_DIM:(j + 1) * HEAD_DIM]
    low = lane < HEAD_DIM
    for cols, g_ref, out_ref, extra in ((_T_KS, gks_ref, ks_ref, alibi_cols + block_onehot),
                                        (_T_KW, gkw_ref, kw_ref, alibi_cols)):
        pair = p_tok[cols]
        for kh in range(NSA_KV_HEADS):
            raw = pair if kh == 0 else pltpu.roll(pair, HEAD_DIM, 1)
            slab = key_slab(jnp.where(low, raw, 0.0), g_ref[...]) + extra
            out_ref[:, kh * LANES:(kh + 1) * LANES] = slab.astype(out_ref.dtype)

    qit_ref[...] = p_feat[_F_QI].astype(qit_ref.dtype)
    gw = p_feat[_F_GW]
    grow = lax.broadcasted_iota(jnp.int32, gw.shape, 0)
    gwt_ref[...] = jnp.where(grow < GATE_ROWS, jax.nn.sigmoid(gw), gw * w_idx_scale)
    erow = lax.broadcasted_iota(jnp.int32, (HEAD_DIM, tm), 0)
    q_scale = HEAD_DIM ** -0.5 * LOG2E
    for rows, g_ref, out_ref, slopes in ((_F_QA, gqa_ref, qat_ref, slopes_a), (_F_QB, gqb_ref, qbt_ref, slopes_b)):
        q_all = p_feat[rows]
        for hd in range(len(slopes)):
            q = q_all[hd * HEAD_DIM:(hd + 1) * HEAD_DIM]
            ms = jnp.mean(q * q, axis=0, keepdims=True)
            q = q * lax.rsqrt(ms + RMS_EPS) * g_ref[...] * q_scale
            out_ref[hd * LANES:hd * LANES + HEAD_DIM, :] = q.astype(out_ref.dtype)
            extra = jnp.where(erow < 3, slopes[hd], 0.0)
            out_ref[hd * LANES + HEAD_DIM:(hd + 1) * LANES, :] = extra.astype(out_ref.dtype)
    ones_row = jnp.where(erow == 0, 1.0, 0.0)
    for rows, out_ref in ((_F_VS, vst_ref), (_F_VW, vwt_ref)):
        v_t = p_feat[rows]
        for kh in range(NSA_KV_HEADS):
            out_ref[kh * LANES:kh * LANES + HEAD_DIM, :] = v_t[kh * HEAD_DIM:(kh + 1) * HEAD_DIM].astype(out_ref.dtype)
            out_ref[kh * LANES + HEAD_DIM:(kh + 1) * LANES, :] = ones_row.astype(out_ref.dtype)


def _proj(x2, seq_len, g_mix, w_tok, w_feat, g_lat, w_uk, w_uv_t, gk_a, gks, gkw, gq_a, gq_b, tm):
    T, D = x2.shape
    tokm = lambda w: pl.BlockSpec((tm, w), lambda i: (i, 0))
    featm = lambda r: pl.BlockSpec((r, tm), lambda i: (0, i))
    full = lambda a: pl.BlockSpec(a.shape, lambda i: (0,) * a.ndim)
    f32 = jnp.float32
    mx = MXU_DTYPE
    out_shape = (
        jax.ShapeDtypeStruct((T, LANES), mx),
        jax.ShapeDtypeStruct((T, LANES), mx),
        jax.ShapeDtypeStruct((LANES, T), mx),
        jax.ShapeDtypeStruct((2 * NSA_KV_HEADS, T, HEAD_DIM), f32),
        jax.ShapeDtypeStruct((T, NSA_KV_HEADS * LANES), mx),
        jax.ShapeDtypeStruct((T, NSA_KV_HEADS * LANES), mx),
        jax.ShapeDtypeStruct((NSA_KV_HEADS * LANES, T), mx),
        jax.ShapeDtypeStruct((NSA_KV_HEADS * LANES, T), mx),
        jax.ShapeDtypeStruct((IDX_HEADS * IDX_DIM, T), mx),
        jax.ShapeDtypeStruct((LANES, T), f32),
        jax.ShapeDtypeStruct((DSA_HEADS * LANES, T), mx),
        jax.ShapeDtypeStruct((NSA_HEADS * LANES, T), mx),
    )
    out_specs = (
        tokm(LANES), tokm(LANES), featm(LANES),
        pl.BlockSpec((2 * NSA_KV_HEADS, tm, HEAD_DIM), lambda i: (0, i, 0)),
        tokm(NSA_KV_HEADS * LANES), tokm(NSA_KV_HEADS * LANES),
        featm(NSA_KV_HEADS * LANES), featm(NSA_KV_HEADS * LANES),
        featm(IDX_HEADS * IDX_DIM), featm(LANES), featm(DSA_HEADS * LANES), featm(NSA_HEADS * LANES),
    )
    kern = functools.partial(
        _proj_kernel, seq_len=seq_len, w_idx_scale=IDX_HEADS ** -0.5 * IDX_DIM ** -0.5,
        slopes_a=_alibi_slopes(DSA_HEADS), slopes_b=_alibi_slopes(NSA_HEADS))
    ins = (x2, g_mix, w_tok, w_feat, g_lat, w_uk, w_uv_t, gk_a, gks, gkw, gq_a, gq_b)
    return pl.pallas_call(
        kern,
        grid=(T // tm,),
        in_specs=[tokm(D)] + [full(a) for a in ins[1:]],
        out_specs=out_specs,
        out_shape=out_shape,
        compiler_params=_params(("parallel",)),
        name="proj",
    )(*ins)


def _compress_kernel(xk_ref, xv_ref, pe_ref, w1_ref, w2k_ref, w2vt_ref, gkc_ref, kc_ref, vct_ref):
    n = kc_ref.shape[0]

    def hidden(x_ref, kv):
        a = jnp.zeros((n, CMP_HIDDEN), jnp.float32)
        b = jnp.zeros((n, CMP_HIDDEN), jnp.float32)
        for l in range(CMP_STRIDE):
            x_l = x_ref[pl.ds(l, n, stride=CMP_STRIDE), :]
            for acc_is_b, row in ((False, l), (True, CMP_STRIDE + l)):
                lhs = (x_l + pe_ref[row:row + 1, :]).astype(MXU_DTYPE)
                part = _dot(lhs, w1_ref[kv, row * HEAD_DIM:(row + 1) * HEAD_DIM, :])
                if acc_is_b:
                    b = b + part
                else:
                    a = a + part
        return jax.nn.gelu(a + pltpu.roll(b, n - 1, 0)).astype(MXU_DTYPE)

    yk = _dot(hidden(xk_ref, 0), w2k_ref[...])
    ms = jnp.sum(yk * yk, axis=-1, keepdims=True) * (1.0 / HEAD_DIM)
    kc_ref[...] = (yk * lax.rsqrt(ms + RMS_EPS) * gkc_ref[...]).astype(kc_ref.dtype)
    vct_ref[...] = _dot_nt(w2vt_ref[...], hidden(xv_ref, 1)).astype(vct_ref.dtype)


def _compress(xc, pe, phi1, w2k, w2v_t, g_kc):
    _, B, S, w = xc.shape
    n = S // CMP_STRIDE
    full = lambda a: pl.BlockSpec(a.shape, lambda b, kh: (0,) * a.ndim)
    return pl.pallas_call(
        _compress_kernel,
        grid=(B, NSA_KV_HEADS),
        in_specs=[
            pl.BlockSpec((None, None, S, w), lambda b, kh: (kh, b, 0, 0)),
            pl.BlockSpec((None, None, S, w), lambda b, kh: (NSA_KV_HEADS + kh, b, 0, 0)),
            full(pe), full(phi1), full(w2k), full(w2v_t), full(g_kc),
        ],
        out_specs=(pl.BlockSpec((None, None, n, LANES), lambda b, kh: (b, kh, 0, 0)),
                   pl.BlockSpec((None, None, LANES, n), lambda b, kh: (b, kh, 0, 0))),
        out_shape=(jax.ShapeDtypeStruct((B, NSA_KV_HEADS, n, LANES), MXU_DTYPE),
                   jax.ShapeDtypeStruct((B, NSA_KV_HEADS, LANES, n), MXU_DTYPE)),
        compiler_params=_params(("parallel", "parallel")),
        name="compress",
    )(xc, xc, pe, phi1, w2k, w2v_t, g_kc)


_INT_MIN = -2 ** 31


def _key_to_float(u):
    key = u ^ jnp.int32(_INT_MIN)
    bits = jnp.where(key >= 0, key, key ^ jnp.int32(0x7FFFFFFF))
    return lax.bitcast_convert_type(bits, jnp.float32)


def _col_reduce(x, op, final):
    rows = REDUCE_ROWS
    acc = x
    if x.shape[0] > rows and x.shape[0] % rows == 0:
        acc = x[:rows]
        for j in range(1, x.shape[0] // rows):
            acc = op(acc, x[j * rows:(j + 1) * rows])
    return final(acc.astype(jnp.float32), axis=0, keepdims=True)


def _col_sum(x):
    return _col_reduce(x, jnp.add, jnp.sum)


def _col_max(x):
    return _col_reduce(x, jnp.maximum, jnp.max)


def _kth_largest(score, k, side_steps=0, side_work=None):
    def body(it, u):
        u_try = u | lax.shift_left(jnp.int32(1), 31 - it)
        cnt = _col_sum(jnp.where(score >= _key_to_float(u_try), 1.0, 0.0))
        return jnp.where(cnt >= k, u_try, u)

    def body_with_side_work(j, u):
        side_work(j)
        for i in range(32 // side_steps):
            u = body(j * (32 // side_steps) + i, u)
        return u

    u = jnp.zeros((1, score.shape[1]), jnp.int32)
    if side_steps:
        u = lax.fori_loop(0, side_steps, body_with_side_work, u)
    else:
        u = lax.fori_loop(0, 32, body, u)
    thr = _key_to_float(u)
    return jnp.where(thr != thr, -jnp.inf, thr)


def _lane_tile(x, n):
    return jnp.concatenate([x] * n, axis=1)


def _softmax_pv(parts):
    return _softmax_pv_staged([parts])[0]


def _softmax_pv_staged(problems):
    maxes = []
    for parts in problems:
        m = None
        for s, _ in parts:
            m_p = _col_max(s)
            m = m_p if m is None else jnp.maximum(m, m_p)
        maxes.append(m)
    outs = []
    for parts, m in zip(problems, maxes):
        out = None
        for s, v_t in parts:
            o_p = _dot(v_t, jnp.exp2(s - m).astype(MXU_DTYPE))
            out = o_p if out is None else out + o_p
        outs.append(out)
    return outs


def _width_classes(n_qblk, n_classes):
    n_cls = n_classes if n_qblk % n_classes == 0 else 1
    per = n_qblk // n_cls
    return [(c * per, (c + 1) * per, (c + 1) * per * Q_BLOCK) for c in range(n_cls)]


def _dsa_body(W, qit_ref, gwt_ref, qat_ref, ki_ref, ka_ref, vat_ref, o_ref, sel_ref, qk_ref,*, topk):
    f32 = jnp.float32
    QB = Q_BLOCK
    t = pl.program_id(1) * QB + lax.broadcasted_iota(jnp.int32, (1, QB), 1)
    spos = lax.broadcasted_iota(jnp.int32, (W, QB), 0)
    causal = spos <= t

    ki = ki_ref[:W, :]
    G = 4
    kpad = jnp.zeros((LANES - IDX_DIM, G * QB), MXU_DTYPE)
    score = jnp.zeros((W, QB), f32)
    for hg in range(IDX_HEADS // G):
        q_idx = jnp.concatenate(
            [qit_ref[(hg * G + g) * IDX_DIM:(hg * G + g + 1) * IDX_DIM, :] for g in range(G)], axis=1)
        logits = _dot(ki, jnp.concatenate([q_idx, kpad], axis=0))
        for g in range(G):
            w_h = gwt_ref[GATE_ROWS + hg * G + g:GATE_ROWS + hg * G + g + 1, :]
            score = score + jnp.maximum(logits[:, g * QB:(g + 1) * QB], 0.0) * w_h
    score = jnp.where(causal, score, -jnp.inf)

    q_all = jnp.concatenate([qat_ref[h * LANES:(h + 1) * LANES, :] for h in range(DSA_HEADS)], axis=1)
    chunk = W // QK_SIDE_STEPS

    def qk_chunk(it):
        r0 = pl.multiple_of(it * chunk, chunk)
        qk_ref[pl.ds(r0, chunk), :] = _dot(ka_ref[pl.ds(r0, chunk), :], q_all)

    lowest = float(jnp.finfo(jnp.float32).min)
    if W <= topk:
        qk_ref[:W, :] = _dot(ka_ref[:W, :], q_all)
        thr = jnp.full((1, QB), lowest, f32)
    else:
        thr = _kth_largest(score, float(topk), QK_SIDE_STEPS, qk_chunk)
        thr = jnp.maximum(thr, lowest)
    ge = score >= thr
    sel_ref[:W, :] = jnp.where(ge, 0.0, NEG_INF)

    def repair_ties():
        gt = score > thr
        eq = score == thr
        need = float(topk) - _col_sum(jnp.where(gt, 1.0, 0.0))
        cw = TIE_CHUNK
        r = lax.broadcasted_iota(jnp.int32, (cw, cw), 0)
        c = lax.broadcasted_iota(jnp.int32, (cw, cw), 1)
        lower = jnp.where(c <= r, 1.0, 0.0).astype(MXU_DTYPE)
        carry = jnp.zeros((1, QB), f32)
        for j in range(W // cw):
            sl = slice(j * cw, (j + 1) * cw)
            eq_j = jnp.where(eq[sl], 1.0, 0.0)
            prefix = _dot(lower, eq_j.astype(MXU_DTYPE)) + carry
            keep = gt[sl] | (eq[sl] & (prefix <= need))
            sel_ref[sl, :] = jnp.where(keep, 0.0, NEG_INF)
            carry = carry + jnp.sum(eq_j, axis=0, keepdims=True)

    if W > topk:
        pl.when(jnp.max(_col_sum(jnp.where(ge, 1.0, 0.0))) > float(topk))(repair_ties)

    G = 4
    va_t = vat_ref[:, :W]
    for hg in range(DSA_HEADS // G):
        s = qk_ref[:W, hg * G * QB:(hg + 1) * G * QB] + _lane_tile(sel_ref[:W, :], G)
        o_t = _softmax_pv([(s, va_t)])
        o_t = o_t[:HEAD_DIM] * (1.0 / o_t[HEAD_DIM:HEAD_DIM + 1])
        for g2 in range(G // 2):
            pair = jnp.concatenate([o_t[:, (2 * g2) * QB:(2 * g2 + 1) * QB],
                                    o_t[:, (2 * g2 + 1) * QB:(2 * g2 + 2) * QB]], axis=0)
            col = (hg * G + 2 * g2) * HEAD_DIM
            o_ref[:, col:col + 2 * HEAD_DIM] = pair.T.astype(o_ref.dtype)


def _dsa_kernel(qit_ref, gwt_ref, qat_ref, ki_ref, ka_ref, vat_ref, o_ref, sel_ref, qk_ref,*, topk, classes):
    qblk = pl.program_id(1)
    for lo, hi, W in classes:
        @pl.when((qblk >= lo) & (qblk < hi))
        def _(W=W):
            _dsa_body(W, qit_ref, gwt_ref, qat_ref, ki_ref, ka_ref, vat_ref, o_ref, sel_ref, qk_ref,topk=topk)


def _dsa(B, S, qi_t, gw_t, qa_t, ki, ka, va_t):
    nq = S // Q_BLOCK
    topk = min(DSA_TOPK_MAX, S // 4)
    fblk = lambda r: pl.BlockSpec((r, Q_BLOCK), lambda b, i: (0, b * nq + i))
    tseq = lambda w: pl.BlockSpec((S, w), lambda b, i: (b, 0))
    fseq = lambda r: pl.BlockSpec((r, S), lambda b, i: (0, b))
    return pl.pallas_call(
        functools.partial(_dsa_kernel, topk=topk, classes=_width_classes(nq, DSA_WIDTH_CLASSES)),
        grid=(B, nq),
        in_specs=[fblk(IDX_HEADS * IDX_DIM), fblk(LANES), fblk(DSA_HEADS * LANES),
                  tseq(LANES), tseq(LANES), fseq(LANES)],
        out_specs=pl.BlockSpec((Q_BLOCK, WIDTH_A), lambda b, i: (b * nq + i, 0)),
        out_shape=jax.ShapeDtypeStruct((B * S, WIDTH_A), MXU_DTYPE),
        scratch_shapes=[pltpu.VMEM((S, Q_BLOCK), jnp.float32),
                        pltpu.VMEM((S, DSA_HEADS * Q_BLOCK), jnp.float32)],
        compiler_params=_params(("parallel", "arbitrary")),
        name="dsa",
    )(qi_t, gw_t, qa_t, ki, ka, va_t)


def _nsa_body(W, first_blk, gwt_ref, qbt_ref, ks_ref, kw_ref, vst_ref, vwt_ref, kc_ref, vct_ref, ovl_ref, o_ref,
              *, seq_len):
    f32 = jnp.float32
    QB, G = Q_BLOCK, NSA_GROUP
    n_slc = seq_len // SLC_LEN
    n_top = min(SLC_TOPN, n_slc)
    win_len = WINDOW + QB
    nc = kc_ref.shape[1]

    t0 = pl.program_id(1) * QB
    t = t0 + lax.broadcasted_iota(jnp.int32, (1, QB), 1)
    t4 = _lane_tile(t, G)
    valid_c = lax.broadcasted_iota(jnp.int32, (nc, G * QB), 0) * CMP_STRIDE + (CMP_LEN - 1) <= t4
    past = first_blk * QB
    causal_tail = past + lax.broadcasted_iota(jnp.int32, (W - past, G * QB), 0) <= t4
    w0 = pl.multiple_of(jnp.maximum(t0 - WINDOW, 0), QB)
    full_window = past >= WINDOW
    if full_window:
        r_minus_lane = lax.broadcasted_iota(jnp.int32, (QB, G * QB), 0) - (t4 - t0)
        w_head_ok = r_minus_lane > 0
        w_tail_ok = r_minus_lane <= 0
    else:
        wd = t4 - (w0 + lax.broadcasted_iota(jnp.int32, (win_len, G * QB), 0))
        wvalid = (wd >= 0) & (wd < WINDOW)
    jblk = lax.broadcasted_iota(jnp.int32, (MAX_SLC_BLOCKS, QB), 0)
    cur = lax.shift_right_logical(t, 6)
    future = jblk > cur
    forced = (jblk == 0) | (jblk == cur) | (jblk == cur - 1)
    ovl = ovl_ref[...]

    KH = range(NSA_KV_HEADS)
    ksl = [slice(kh * LANES, (kh + 1) * LANES) for kh in KH]
    q = [jnp.concatenate([qbt_ref[(kh * G + g) * LANES:(kh * G + g + 1) * LANES, :] for g in range(G)], axis=1)
         for kh in KH]

    s_c = [jnp.where(valid_c, _dot(kc_ref[kh], q[kh]), NEG_INF) for kh in KH]
    s_w = [_dot(kw_ref[pl.ds(w0, win_len), ksl[kh]], q[kh]) for kh in KH]
    e_c = [jnp.where(valid_c, jnp.exp2(s_c[kh] - jnp.max(s_c[kh], axis=0, keepdims=True)), 0.0) for kh in KH]
    l_c = [jnp.sum(e_c[kh], axis=0, keepdims=True) for kh in KH]
    p_c = [e_c[kh] * (1.0 / jnp.where(l_c[kh] > 0.0, l_c[kh], 1.0)) for kh in KH]
    o_cmp = [_dot(vct_ref[kh], p_c[kh].astype(MXU_DTYPE)) for kh in KH]

    val = []
    for kh in KH:
        pc = p_c[kh]
        p_sum = pc[:, 0:QB] + pc[:, QB:2 * QB] + pc[:, 2 * QB:3 * QB] + pc[:, 3 * QB:4 * QB]
        p_hi = p_sum.astype(MXU_DTYPE)
        p_lo = (p_sum - p_hi.astype(f32)).astype(MXU_DTYPE)
        imp = _dot(ovl, p_hi) + _dot(ovl, p_lo)
        val.append(jnp.where(future, -jnp.inf, imp + jnp.where(forced, FORCE_BONUS, 0.0)))

    win_problems = []
    for kh in KH:
        v_w = vwt_ref[ksl[kh], pl.ds(w0, win_len)]
        if full_window:
            parts = [(jnp.where(w_head_ok, s_w[kh][:QB], NEG_INF), v_w[:, :QB]),
                     (s_w[kh][QB:WINDOW], v_w[:, QB:WINDOW]),
                     (jnp.where(w_tail_ok, s_w[kh][WINDOW:], NEG_INF), v_w[:, WINDOW:])]
        else:
            parts = [(jnp.where(wvalid, s_w[kh], NEG_INF), v_w)]
        win_problems.append(parts)

    rank = [jnp.zeros(val[kh].shape, f32) for kh in KH]
    for i in range(min(n_slc, W // SLC_LEN)):
        for kh in KH:
            vi = val[kh][i:i + 1, :]
            rank[kh] = rank[kh] + jnp.where((vi > val[kh]) | ((vi == val[kh]) & (jblk > i)), 1.0, 0.0)

    s_s = []
    for kh in KH:
        bias = jnp.where((rank[kh] < n_top) & (jblk < n_slc), 0.0, -BLOCK_BIAS).astype(MXU_DTYPE)
        q_sel = jnp.concatenate([q[kh][:BIAS_COL], _lane_tile(bias, G)], axis=0)
        s_s.append(_dot(ks_ref[:W, ksl[kh]], q_sel))
    o_win = _softmax_pv_staged(win_problems)
    slc_problems = []
    for kh in KH:
        parts = [(jnp.where(causal_tail, s_s[kh][past:], NEG_INF), vst_ref[ksl[kh], past:W])]
        if past:
            parts.append((s_s[kh][:past], vst_ref[ksl[kh], :past]))
        slc_problems.append(parts)
    o_slc = _softmax_pv_staged(slc_problems)

    for kh in KH:
        inv_s = 1.0 / o_slc[kh][HEAD_DIM:HEAD_DIM + 1]
        inv_w = 1.0 / o_win[kh][HEAD_DIM:HEAD_DIM + 1]
        heads = []
        for g in range(G):
            h = kh * G + g
            ls = slice(g * QB, (g + 1) * QB)
            g0 = gwt_ref[3 * h + 0:3 * h + 1, :]
            g1 = gwt_ref[3 * h + 1:3 * h + 2, :] * inv_s[:, ls]
            g2 = gwt_ref[3 * h + 2:3 * h + 3, :] * inv_w[:, ls]
            heads.append(g0 * o_cmp[kh][:HEAD_DIM, ls] + g1 * o_slc[kh][:HEAD_DIM, ls]
                         + g2 * o_win[kh][:HEAD_DIM, ls])
        for g2_ in range(G // 2):
            pair = jnp.concatenate([heads[2 * g2_], heads[2 * g2_ + 1]], axis=0)
            col = (kh * G + 2 * g2_) * HEAD_DIM
            o_ref[:, col:col + 2 * HEAD_DIM] = pair.T.astype(o_ref.dtype)


def _nsa_kernel(gwt_ref, qbt_ref, ks_ref, kw_ref, vst_ref, vwt_ref, kc_ref, vct_ref, ovl_ref, o_ref,
                *, seq_len, classes):
    qblk = pl.program_id(1)
    for lo, hi, W in classes:
        @pl.when((qblk >= lo) & (qblk < hi))
        def _(W=W, lo=lo):
            _nsa_body(W, lo, gwt_ref, qbt_ref, ks_ref, kw_ref, vst_ref, vwt_ref, kc_ref, vct_ref, ovl_ref, o_ref,
                      seq_len=seq_len)


def _nsa(B, S, gw_t, qb_t, ks, kw, vs_t, vw_t, kc, vc_t, ovl):
    nq = S // Q_BLOCK
    nc = kc.shape[2]
    kvw = NSA_KV_HEADS * LANES
    fblk = lambda r: pl.BlockSpec((r, Q_BLOCK), lambda b, i: (0, b * nq + i))
    tseq = lambda w: pl.BlockSpec((S, w), lambda b, i: (b, 0))
    fseq = lambda r: pl.BlockSpec((r, S), lambda b, i: (0, b))
    return pl.pallas_call(
        functools.partial(_nsa_kernel, seq_len=S, classes=_width_classes(nq, NSA_WIDTH_CLASSES)),
        grid=(B, nq),
        in_specs=[fblk(LANES), fblk(NSA_HEADS * LANES), tseq(kvw), tseq(kvw), fseq(kvw), fseq(kvw),
                  pl.BlockSpec((None, NSA_KV_HEADS, nc, LANES), lambda b, i: (b, 0, 0, 0)),
                  pl.BlockSpec((None, NSA_KV_HEADS, LANES, nc), lambda b, i: (b, 0, 0, 0)),
                  pl.BlockSpec(ovl.shape, lambda b, i: (0, 0))],
        out_specs=pl.BlockSpec((Q_BLOCK, WIDTH_B), lambda b, i: (b * nq + i, 0)),
        out_shape=jax.ShapeDtypeStruct((B * S, WIDTH_B), MXU_DTYPE),
        compiler_params=_params(("parallel", "arbitrary")),
        name="nsa",
    )(gw_t, qb_t, ks, kw, vs_t, vw_t, kc, vc_t, ovl)


def _merge_ffn_kernel(x_ref, oa_ref, ob_ref, gmix_ref, wg_ref, wua_ref, wub_ref, wo_ref,
                      gmlp_ref, w1_ref, w2_ref, o_ref, *, chunk):
    x = x_ref[...]
    D = x.shape[1]
    h = _rms(x, gmix_ref[...]).astype(MXU_DTYPE)
    ga = jax.nn.sigmoid(_dot(h, wg_ref[:, :D]))
    gb = jax.nn.sigmoid(_dot(h, wg_ref[:, D:]))
    merged = ga * _dot(oa_ref[...], wua_ref[...]) + gb * _dot(ob_ref[...], wub_ref[...])
    x1 = x + _dot(merged.astype(MXU_DTYPE), wo_ref[...])
    h2 = _rms(x1, gmlp_ref[...]).astype(MXU_DTYPE)
    acc = x1
    for c in range(w1_ref.shape[1] // chunk):
        u = jnp.maximum(_dot(h2, w1_ref[:, c * chunk:(c + 1) * chunk]), 0.0)
        acc = acc + _dot((u * u).astype(MXU_DTYPE), w2_ref[c * chunk:(c + 1) * chunk, :])
    o_ref[...] = acc


def _merge_ffn(x2, oa, ob, g_mix, w_gate, w_up_a, w_up_b, w_out, g_mlp, w1, w2, tm):
    T, D = x2.shape
    row = lambda w: pl.BlockSpec((tm, w), lambda i: (i, 0))
    full = lambda a: pl.BlockSpec(a.shape, lambda i: (0,) * a.ndim, pipeline_mode=pl.Buffered(1))
    ins = (x2, oa, ob, g_mix, w_gate, w_up_a, w_up_b, w_out, g_mlp, w1, w2)
    return pl.pallas_call(
        functools.partial(_merge_ffn_kernel, chunk=1024),
        grid=(T // tm,),
        in_specs=[row(D), row(WIDTH_A), row(WIDTH_B)] + [full(a) for a in ins[3:]],
        out_specs=row(D),
        out_shape=jax.ShapeDtypeStruct((T, D), jnp.float32),
        compiler_params=_params(("parallel",)),
        name="merge_ffn",
    )(*ins)


def _layer(x, g_mix, w_in, g_q_a, g_k_a, g_lat_a, w_uk_a, w_uv_a, g_q_b, g_kc_b, g_ks_b, g_kw_b,
           pe_cmp_b, phi_k1_b, phi_k2_b, phi_v1_b, phi_v2_b, w_up_a, w_up_b, w_out, g_mlp, w_ff1, w_ff2):
    B, S, D = x.shape
    T = B * S
    f32 = jnp.float32
    mx = MXU_DTYPE
    n_slc = S // SLC_LEN
    assert n_slc <= MAX_SLC_BLOCKS and S % (DSA_WIDTH_CLASSES * TIE_CHUNK) == 0 and S >= WINDOW + Q_BLOCK
    col_sizes = (WIDTH_A, DSA_LATENT, IDX_HEADS * IDX_DIM, IDX_DIM, IDX_HEADS,
                 WIDTH_B, 6 * NSA_KV_W, 3 * NSA_HEADS, 2 * D)
    offs = np.cumsum((0,) + col_sizes)
    w_qa, w_ca, w_qi, w_ki, w_wi, w_qb, w_kvb, w_gb, w_gm = [
        w_in[:, offs[i]:offs[i + 1]] for i in range(len(col_sizes))]
    w_kc, w_vc, w_ks, w_vs, w_kw, w_vw = [w_kvb[:, j * NSA_KV_W:(j + 1) * NSA_KV_W] for j in range(6)]
    w_tok = jnp.concatenate(
        [w_ki, jnp.zeros((D, LANES - IDX_DIM), f32), w_ca, w_kc, w_vc, w_ks, w_kw], axis=1).astype(mx)
    assert w_tok.shape[1] == _T_COLS
    gw_pad = jnp.zeros((D, LANES - GATE_ROWS - IDX_HEADS), f32)
    w_feat = jnp.concatenate(
        [w_qi, w_gb, w_wi, gw_pad, w_qa, w_qb, w_vs, w_vw], axis=1).T.astype(mx)
    assert w_feat.shape[0] == _F_ROWS
    w_gate = w_gm.astype(mx)
    zpad = jnp.zeros((DSA_LATENT, LANES - HEAD_DIM), f32)
    w_uk = jnp.concatenate([w_uk_a, zpad], axis=1).astype(mx)
    w_uv_t = jnp.concatenate([w_uv_a, zpad], axis=1).T.astype(mx)
    row = lambda g: g.reshape(1, -1).astype(f32)
    rpad = lambda g: jnp.concatenate([row(g), jnp.zeros((1, LANES - HEAD_DIM), f32)], axis=1)
    colv = lambda g: g.reshape(-1, 1).astype(f32)

    x2 = x.reshape(T, D)
    (ki, ka, va_t, kcv, ks, kw, vs_t, vw_t, qi_t, gw_t, qa_t, qb_t) = _proj(
        x2, S, row(g_mix), w_tok, w_feat, row(g_lat_a), w_uk, w_uv_t, rpad(g_k_a), rpad(g_ks_b), rpad(g_kw_b),
        colv(g_q_a), colv(g_q_b), tm=512)

    n_chunk = S // CMP_STRIDE
    n_cmp = (S - CMP_LEN) // CMP_STRIDE + 1
    xc = kcv.reshape(2 * NSA_KV_HEADS, B, S, HEAD_DIM)
    phi1 = jnp.stack([phi_k1_b, phi_v1_b]).astype(mx)
    hpad = jnp.zeros((CMP_HIDDEN, LANES - HEAD_DIM), f32)
    w2k = jnp.concatenate([phi_k2_b, hpad], axis=1).astype(mx)
    w2v_t = jnp.concatenate([phi_v2_b, hpad], axis=1).T.astype(mx)
    kc, vc_t = _compress(xc, pe_cmp_b.astype(f32), phi1, w2k, w2v_t, rpad(g_kc_b))

    c_i = np.arange(n_chunk)[None, :] * CMP_STRIDE
    j_i = np.arange(MAX_SLC_BLOCKS)[:, None] * SLC_LEN
    ovl = (c_i < j_i + SLC_LEN) & (c_i + CMP_LEN > j_i) & (c_i < n_cmp * CMP_STRIDE) & (j_i < S)
    ovl = jnp.asarray(ovl, f32).astype(mx)

    o_a = _dsa(B, S, qi_t, gw_t, qa_t, ki, ka, va_t)
    o_b = _nsa(B, S, gw_t, qb_t, ks, kw, vs_t, vw_t, kc, vc_t, ovl)

    out = _merge_ffn(x2, o_a, o_b, row(g_mix), w_gate, w_up_a.astype(mx), w_up_b.astype(mx), w_out.astype(mx),
                     row(g_mlp), w_ff1.astype(mx), w_ff2.astype(mx), tm=512)
    return out.reshape(B, S, D)


def kernel(x, g_mix, w_in, g_q_a, g_k_a, g_lat_a, w_uk_a, w_uv_a, g_q_b, g_kc_b, g_ks_b, g_kw_b, pe_cmp_b,
           phi_k1_b, phi_k2_b, phi_v1_b, phi_v2_b, w_up_a, w_up_b, w_out, g_mlp, w_ff1, w_ff2):
    params = (g_mix, w_in, g_q_a, g_k_a, g_lat_a, w_uk_a, w_uv_a, g_q_b, g_kc_b, g_ks_b, g_kw_b, pe_cmp_b,
              phi_k1_b, phi_k2_b, phi_v1_b, phi_v2_b, w_up_a, w_up_b, w_out, g_mlp, w_ff1, w_ff2)
    for l in range(g_mix.shape[0]):
        x = _layer(x, *[p[l] for p in params])
    return x
```

```python
import functools
import math

import numpy as np
import jax
import jax.numpy as jnp
from jax import lax
from jax.experimental import pallas as pl
from jax.experimental.pallas import tpu as pltpu

HEAD_DIM = 64
DSA_HEADS = 8
DSA_LATENT = 128
IDX_HEADS = 8
IDX_DIM = 32
DSA_TOPK_MAX = 256
NSA_HEADS = 8
NSA_KV_HEADS = 2
NSA_GROUP = NSA_HEADS // NSA_KV_HEADS
CMP_LEN = 32
CMP_STRIDE = 16
CMP_HIDDEN = 128
SLC_LEN = 64
SLC_TOPN = 16
WINDOW = 512
FORCE_BONUS = 1e6
Q_BLOCK = 128
RMS_EPS = 1e-6
NEG_INF = -1e30

WIDTH_A = DSA_HEADS * HEAD_DIM
WIDTH_B = NSA_HEADS * HEAD_DIM
NSA_KV_W = NSA_KV_HEADS * HEAD_DIM

LANES = 128
GATE_ROWS = 3 * NSA_HEADS
ALIBI_COL = HEAD_DIM
BIAS_COL = 96
MAX_SLC_BLOCKS = LANES - BIAS_COL
BLOCK_BIAS = 2.0 ** 100
LOG2E = math.log2(math.e)
MXU_DTYPE = jnp.bfloat16
VMEM_LIMIT = 52 * 1024 * 1024
DSA_WIDTH_CLASSES = 8
NSA_WIDTH_CLASSES = 8
TIE_CHUNK = 128
REDUCE_ROWS = 64
QK_SIDE_STEPS = 2

_NT = (((1,), (1,)), ((), ()))


def _alibi_slopes(n_heads):
    return [float(v) for v in np.asarray(
        2.0 ** (-8.0 * np.arange(1, n_heads + 1) / n_heads), dtype=np.float32)]


def _dot(a, b):
    return jnp.dot(a, b, preferred_element_type=jnp.float32)


def _dot_nt(a, b):
    return lax.dot_general(a, b, _NT, preferred_element_type=jnp.float32)


def _rms(x, g):
    ms = jnp.mean(x * x, axis=-1, keepdims=True)
    return x * lax.rsqrt(ms + RMS_EPS) * g


def _params(sem):
    return pltpu.CompilerParams(dimension_semantics=sem, vmem_limit_bytes=VMEM_LIMIT)


def _split3(c):
    c1 = c.astype(MXU_DTYPE).astype(jnp.float32)
    c2 = (c - c1).astype(MXU_DTYPE).astype(jnp.float32)
    c3 = (c - c1 - c2).astype(MXU_DTYPE).astype(jnp.float32)
    return c1, c2, c3


_T_KI = (0, 128)
_T_CA = (128, 256)
_T_KCV = (256, 512)
_T_KS = (512, 640)
_T_KW = (640, 768)
_T_COLS = 768
_F_QI = (0, 256)
_F_GW = (256, 384)
_F_QA = (384, 896)
_F_QB = (896, 1408)
_F_VS = (1408, 1536)
_F_VW = (1536, 1664)
_F_ROWS = 1664


def _proj_kernel(x_ref, gmix_ref, wtok_ref, wfeat_ref, glat_ref, wuk_ref, wuvt_ref, gka_ref, gks_ref, gkw_ref,
                 gqa_ref, gqb_ref,
                 ki_ref, ka_ref, vat_ref, kcv_ref, ks_ref, kw_ref, vst_ref, vwt_ref, qit_ref, gwt_ref,
                 qat_ref, qbt_ref, *, seq_len, w_idx_scale, slopes_a, slopes_b):
    f32 = jnp.float32
    x = x_ref[...]
    tm = x.shape[0]
    h = _rms(x, gmix_ref[...]).astype(MXU_DTYPE)

    def tok(c):
        return _dot(h, wtok_ref[:, c[0]:c[1]])

    def feat(r):
        return _dot_nt(wfeat_ref[r[0]:r[1], :], h)

    s0 = (pl.program_id(0) % (seq_len // tm)) * tm
    spos = s0 + lax.broadcasted_iota(jnp.int32, (tm, LANES), 0)
    lane = lax.broadcasted_iota(jnp.int32, (tm, LANES), 1)
    c1, c2, c3 = _split3(spos.astype(f32) * LOG2E)
    alibi_cols = jnp.where(lane == ALIBI_COL, c1,
                           jnp.where(lane == ALIBI_COL + 1, c2, jnp.where(lane == ALIBI_COL + 2, c3, 0.0)))
    block_onehot = jnp.where(lane - BIAS_COL == lax.shift_right_logical(spos, 6), 1.0, 0.0)

    def key_slab(raw, g):
        ms = jnp.sum(raw * raw, axis=-1, keepdims=True) * (1.0 / HEAD_DIM)
        return raw * lax.rsqrt(ms + RMS_EPS) * g

    p_tok = {c: tok(c) for c in (_T_KI, _T_CA, _T_KCV, _T_KS, _T_KW)}
    p_feat = {r: feat(r) for r in (_F_QI, _F_GW, _F_QA, _F_QB, _F_VS, _F_VW)}

    ki_ref[...] = p_tok[_T_KI].astype(ki_ref.dtype)
    c = _rms(p_tok[_T_CA], glat_ref[...]).astype(MXU_DTYPE)
    ka_ref[...] = (key_slab(_dot(c, wuk_ref[...]), gka_ref[...]) + alibi_cols).astype(ka_ref.dtype)
    va_t = _dot_nt(wuvt_ref[...], c)
    frow = lax.broadcasted_iota(jnp.int32, va_t.shape, 0)
    vat_ref[...] = jnp.where(frow == HEAD_DIM, 1.0, va_t).astype(vat_ref.dtype)
    kcv = p_tok[_T_KCV]
    for j in range(2 * NSA_KV_HEADS):
        kcv_ref[j] = kcv[:, j * HEAD_DIM:(j + 1) * HEAD_DIM]
    low = lane < HEAD_DIM
    for cols, g_ref, out_ref, extra in ((_T_KS, gks_ref, ks_ref, alibi_cols + block_onehot),
                                        (_T_KW, gkw_ref, kw_ref, alibi_cols)):
        pair = p_tok[cols]
        for kh in range(NSA_KV_HEADS):
            raw = pair if kh == 0 else pltpu.roll(pair, HEAD_DIM, 1)
            slab = key_slab(jnp.where(low, raw, 0.0), g_ref[...]) + extra
            out_ref[:, kh * LANES:(kh + 1) * LANES] = slab.astype(out_ref.dtype)

    qit_ref[...] = p_feat[_F_QI].astype(qit_ref.dtype)
    gw = p_feat[_F_GW]
    grow = lax.broadcasted_iota(jnp.int32, gw.shape, 0)
    gwt_ref[...] = jnp.where(grow < GATE_ROWS, jax.nn.sigmoid(gw), gw * w_idx_scale)
    erow = lax.broadcasted_iota(jnp.int32, (HEAD_DIM, tm), 0)
    q_scale = HEAD_DIM ** -0.5 * LOG2E
    for rows, g_ref, out_ref, slopes in ((_F_QA, gqa_ref, qat_ref, slopes_a), (_F_QB, gqb_ref, qbt_ref, slopes_b)):
        q_all = p_feat[rows]
        for hd in range(len(slopes)):
            q = q_all[hd * HEAD_DIM:(hd + 1) * HEAD_DIM]
            ms = jnp.mean(q * q, axis=0, keepdims=True)
            q = q * lax.rsqrt(ms + RMS_EPS) * g_ref[...] * q_scale
            out_ref[hd * LANES:hd * LANES + HEAD_DIM, :] = q.astype(out_ref.dtype)
            extra = jnp.where(erow < 3, slopes[hd], 0.0)
            out_ref[hd * LANES + HEAD_DIM:(hd + 1) * LANES, :] = extra.astype(out_ref.dtype)
    ones_row = jnp.where(erow == 0, 1.0, 0.0)
    for rows, out_ref in ((_F_VS, vst_ref), (_F_VW, vwt_ref)):
        v_t = p_feat[rows]
        for kh in range(NSA_KV_HEADS):
            out_ref[kh * LANES:kh * LANES + HEAD_DIM, :] = v_t[kh * HEAD_DIM:(kh + 1) * HEAD_DIM].astype(out_ref.dtype)
            out_ref[kh * LANES + HEAD_DIM:(kh + 1) * LANES, :] = ones_row.astype(out_ref.dtype)


def _proj(x2, seq_len, g_mix, w_tok, w_feat, g_lat, w_uk, w_uv_t, gk_a, gks, gkw, gq_a, gq_b, tm):
    T, D = x2.shape
    tokm = lambda w: pl.BlockSpec((tm, w), lambda i: (i, 0))
    featm = lambda r: pl.BlockSpec((r, tm), lambda i: (0, i))
    full = lambda a: pl.BlockSpec(a.shape, lambda i: (0,) * a.ndim)
    f32 = jnp.float32
    mx = MXU_DTYPE
    out_shape = (
        jax.ShapeDtypeStruct((T, LANES), mx),
        jax.ShapeDtypeStruct((T, LANES), mx),
        jax.ShapeDtypeStruct((LANES, T), mx),
        jax.ShapeDtypeStruct((2 * NSA_KV_HEADS, T, HEAD_DIM), f32),
        jax.ShapeDtypeStruct((T, NSA_KV_HEADS * LANES), mx),
        jax.ShapeDtypeStruct((T, NSA_KV_HEADS * LANES), mx),
        jax.ShapeDtypeStruct((NSA_KV_HEADS * LANES, T), mx),
        jax.ShapeDtypeStruct((NSA_KV_HEADS * LANES, T), mx),
        jax.ShapeDtypeStruct((IDX_HEADS * IDX_DIM, T), mx),
        jax.ShapeDtypeStruct((LANES, T), f32),
        jax.ShapeDtypeStruct((DSA_HEADS * LANES, T), mx),
        jax.ShapeDtypeStruct((NSA_HEADS * LANES, T), mx),
    )
    out_specs = (
        tokm(LANES), tokm(LANES), featm(LANES),
        pl.BlockSpec((2 * NSA_KV_HEADS, tm, HEAD_DIM), lambda i: (0, i, 0)),
        tokm(NSA_KV_HEADS * LANES), tokm(NSA_KV_HEADS * LANES),
        featm(NSA_KV_HEADS * LANES), featm(NSA_KV_HEADS * LANES),
        featm(IDX_HEADS * IDX_DIM), featm(LANES), featm(DSA_HEADS * LANES), featm(NSA_HEADS * LANES),
    )
    kern = functools.partial(
        _proj_kernel, seq_len=seq_len, w_idx_scale=IDX_HEADS ** -0.5 * IDX_DIM ** -0.5,
        slopes_a=_alibi_slopes(DSA_HEADS), slopes_b=_alibi_slopes(NSA_HEADS))
    ins = (x2, g_mix, w_tok, w_feat, g_lat, w_uk, w_uv_t, gk_a, gks, gkw, gq_a, gq_b)
    return pl.pallas_call(
        kern,
        grid=(T // tm,),
        in_specs=[tokm(D)] + [full(a) for a in ins[1:]],
        out_specs=out_specs,
        out_shape=out_shape,
        compiler_params=_params(("parallel",)),
        name="proj",
    )(*ins)


def _compress_kernel(xk_ref, xv_ref, pe_ref, w1_ref, w2k_ref, w2vt_ref, gkc_ref, kc_ref, vct_ref):
    n = kc_ref.shape[0]

    def hidden(x_ref, kv):
        a = jnp.zeros((n, CMP_HIDDEN), jnp.float32)
        b = jnp.zeros((n, CMP_HIDDEN), jnp.float32)
        for l in range(CMP_STRIDE):
            x_l = x_ref[pl.ds(l, n, stride=CMP_STRIDE), :]
            for acc_is_b, row in ((False, l), (True, CMP_STRIDE + l)):
                lhs = (x_l + pe_ref[row:row + 1, :]).astype(MXU_DTYPE)
                part = _dot(lhs, w1_ref[kv, row * HEAD_DIM:(row + 1) * HEAD_DIM, :])
                if acc_is_b:
                    b = b + part
                else:
                    a = a + part
        return jax.nn.gelu(a + pltpu.roll(b, n - 1, 0)).astype(MXU_DTYPE)

    yk = _dot(hidden(xk_ref, 0), w2k_ref[...])
    ms = jnp.sum(yk * yk, axis=-1, keepdims=True) * (1.0 / HEAD_DIM)
    kc_ref[...] = (yk * lax.rsqrt(ms + RMS_EPS) * gkc_ref[...]).astype(kc_ref.dtype)
    vct_ref[...] = _dot_nt(w2vt_ref[...], hidden(xv_ref, 1)).astype(vct_ref.dtype)


def _compress(xc, pe, phi1, w2k, w2v_t, g_kc):
    _, B, S, w = xc.shape
    n = S // CMP_STRIDE
    full = lambda a: pl.BlockSpec(a.shape, lambda b, kh: (0,) * a.ndim)
    return pl.pallas_call(
        _compress_kernel,
        grid=(B, NSA_KV_HEADS),
        in_specs=[
            pl.BlockSpec((None, None, S, w), lambda b, kh: (kh, b, 0, 0)),
            pl.BlockSpec((None, None, S, w), lambda b, kh: (NSA_KV_HEADS + kh, b, 0, 0)),
            full(pe), full(phi1), full(w2k), full(w2v_t), full(g_kc),
        ],
        out_specs=(pl.BlockSpec((None, None, n, LANES), lambda b, kh: (b, kh, 0, 0)),
                   pl.BlockSpec((None, None, LANES, n), lambda b, kh: (b, kh, 0, 0))),
        out_shape=(jax.ShapeDtypeStruct((B, NSA_KV_HEADS, n, LANES), MXU_DTYPE),
                   jax.ShapeDtypeStruct((B, NSA_KV_HEADS, LANES, n), MXU_DTYPE)),
        compiler_params=_params(("parallel", "parallel")),
        name="compress",
    )(xc, xc, pe, phi1, w2k, w2v_t, g_kc)


_INT_MIN = -2 ** 31


def _key_to_float(u):
    key = u ^ jnp.int32(_INT_MIN)
    bits = jnp.where(key >= 0, key, key ^ jnp.int32(0x7FFFFFFF))
    return lax.bitcast_convert_type(bits, jnp.float32)


def _col_reduce(x, op, final):
    rows = REDUCE_ROWS
    acc = x
    if x.shape[0] > rows and x.shape[0] % rows == 0:
        acc = x[:rows]
        for j in range(1, x.shape[0] // rows):
            acc = op(acc, x[j * rows:(j + 1) * rows])
    return final(acc.astype(jnp.float32), axis=0, keepdims=True)


def _col_sum(x):
    return _col_reduce(x, jnp.add, jnp.sum)


def _col_max(x):
    return _col_reduce(x, jnp.maximum, jnp.max)


def _kth_largest(score, k, side_steps=0, side_work=None):
    def body(it, u):
        u_try = u | lax.shift_left(jnp.int32(1), 31 - it)
        cnt = _col_sum(jnp.where(score >= _key_to_float(u_try), 1.0, 0.0))
        return jnp.where(cnt >= k, u_try, u)

    def body_with_side_work(j, u):
        side_work(j)
        for i in range(32 // side_steps):
            u = body(j * (32 // side_steps) + i, u)
        return u

    u = jnp.zeros((1, score.shape[1]), jnp.int32)
    if side_steps:
        u = lax.fori_loop(0, side_steps, body_with_side_work, u)
    else:
        u = lax.fori_loop(0, 32, body, u)
    thr = _key_to_float(u)
    return jnp.where(thr != thr, -jnp.inf, thr)


def _lane_tile(x, n):
    return jnp.concatenate([x] * n, axis=1)


def _softmax_pv(parts):
    return _softmax_pv_staged([parts])[0]


def _softmax_pv_staged(problems):
    maxes = []
    for parts in problems:
        m = None
        for s, _ in parts:
            m_p = _col_max(s)
            m = m_p if m is None else jnp.maximum(m, m_p)
        maxes.append(m)
    outs = []
    for parts, m in zip(problems, maxes):
        out = None
        for s, v_t in parts:
            o_p = _dot(v_t, jnp.exp2(s - m).astype(MXU_DTYPE))
            out = o_p if out is None else out + o_p
        outs.append(out)
    return outs


def _width_classes(n_qblk, n_classes):
    n_cls = n_classes if n_qblk % n_classes == 0 else 1
    per = n_qblk // n_cls
    return [(c * per, (c + 1) * per, (c + 1) * per * Q_BLOCK) for c in range(n_cls)]


def _dsa_body(W, qit_ref, gwt_ref, qat_ref, ki_ref, ka_ref, vat_ref, o_ref, sel_ref, qk_ref,*, topk):
    f32 = jnp.float32
    QB = Q_BLOCK
    t = pl.program_id(1) * QB + lax.broadcasted_iota(jnp.int32, (1, QB), 1)
    spos = lax.broadcasted_iota(jnp.int32, (W, QB), 0)
    causal = spos <= t

    ki = ki_ref[:W, :]
    G = 4
    kpad = jnp.zeros((LANES - IDX_DIM, G * QB), MXU_DTYPE)
    score = jnp.zeros((W, QB), f32)
    for hg in range(IDX_HEADS // G):
        q_idx = jnp.concatenate(
            [qit_ref[(hg * G + g) * IDX_DIM:(hg * G + g + 1) * IDX_DIM, :] for g in range(G)], axis=1)
        logits = _dot(ki, jnp.concatenate([q_idx, kpad], axis=0))
        for g in range(G):
            w_h = gwt_ref[GATE_ROWS + hg * G + g:GATE_ROWS + hg * G + g + 1, :]
            score = score + jnp.maximum(logits[:, g * QB:(g + 1) * QB], 0.0) * w_h
    score = jnp.where(causal, score, -jnp.inf)

    q_all = jnp.concatenate([qat_ref[h * LANES:(h + 1) * LANES, :] for h in range(DSA_HEADS)], axis=1)
    chunk = W // QK_SIDE_STEPS

    def qk_chunk(it):
        r0 = pl.multiple_of(it * chunk, chunk)
        qk_ref[pl.ds(r0, chunk), :] = _dot(ka_ref[pl.ds(r0, chunk), :], q_all)

    lowest = float(jnp.finfo(jnp.float32).min)
    if W <= topk:
        qk_ref[:W, :] = _dot(ka_ref[:W, :], q_all)
        thr = jnp.full((1, QB), lowest, f32)
    else:
        thr = _kth_largest(score, float(topk), QK_SIDE_STEPS, qk_chunk)
        thr = jnp.maximum(thr, lowest)
    ge = score >= thr
    sel_ref[:W, :] = jnp.where(ge, 0.0, NEG_INF)

    def repair_ties():
        gt = score > thr
        eq = score == thr
        need = float(topk) - _col_sum(jnp.where(gt, 1.0, 0.0))
        cw = TIE_CHUNK
        r = lax.broadcasted_iota(jnp.int32, (cw, cw), 0)
        c = lax.broadcasted_iota(jnp.int32, (cw, cw), 1)
        lower = jnp.where(c <= r, 1.0, 0.0).astype(MXU_DTYPE)
        carry = jnp.zeros((1, QB), f32)
        for j in range(W // cw):
            sl = slice(j * cw, (j + 1) * cw)
            eq_j = jnp.where(eq[sl], 1.0, 0.0)
            prefix = _dot(lower, eq_j.astype(MXU_DTYPE)) + carry
            keep = gt[sl] | (eq[sl] & (prefix <= need))
            sel_ref[sl, :] = jnp.where(keep, 0.0, NEG_INF)
            carry = carry + jnp.sum(eq_j, axis=0, keepdims=True)

    if W > topk:
        pl.when(jnp.max(_col_sum(jnp.where(ge, 1.0, 0.0))) > float(topk))(repair_ties)

    G = 4
    va_t = vat_ref[:, :W]
    for hg in range(DSA_HEADS // G):
        s = qk_ref[:W, hg * G * QB:(hg + 1) * G * QB] + _lane_tile(sel_ref[:W, :], G)
        o_t = _softmax_pv([(s, va_t)])
        o_t = o_t[:HEAD_DIM] * (1.0 / o_t[HEAD_DIM:HEAD_DIM + 1])
        for g2 in range(G // 2):
            pair = jnp.concatenate([o_t[:, (2 * g2) * QB:(2 * g2 + 1) * QB],
                                    o_t[:, (2 * g2 + 1) * QB:(2 * g2 + 2) * QB]], axis=0)
            col = (hg * G + 2 * g2) * HEAD_DIM
            o_ref[:, col:col + 2 * HEAD_DIM] = pair.T.astype(o_ref.dtype)


def _dsa_kernel(qit_ref, gwt_ref, qat_ref, ki_ref, ka_ref, vat_ref, o_ref, sel_ref, qk_ref,*, topk, classes):
    qblk = pl.program_id(1)
    for lo, hi, W in classes:
        @pl.when((qblk >= lo) & (qblk < hi))
        def _(W=W):
            _dsa_body(W, qit_ref, gwt_ref, qat_ref, ki_ref, ka_ref, vat_ref, o_ref, sel_ref, qk_ref,topk=topk)


def _dsa(B, S, qi_t, gw_t, qa_t, ki, ka, va_t):
    nq = S // Q_BLOCK
    topk = min(DSA_TOPK_MAX, S // 4)
    fblk = lambda r: pl.BlockSpec((r, Q_BLOCK), lambda b, i: (0, b * nq + i))
    tseq = lambda w: pl.BlockSpec((S, w), lambda b, i: (b, 0))
    fseq = lambda r: pl.BlockSpec((r, S), lambda b, i: (0, b))
    return pl.pallas_call(
        functools.partial(_dsa_kernel, topk=topk, classes=_width_classes(nq, DSA_WIDTH_CLASSES)),
        grid=(B, nq),
        in_specs=[fblk(IDX_HEADS * IDX_DIM), fblk(LANES), fblk(DSA_HEADS * LANES),
                  tseq(LANES), tseq(LANES), fseq(LANES)],
        out_specs=pl.BlockSpec((Q_BLOCK, WIDTH_A), lambda b, i: (b * nq + i, 0)),
        out_shape=jax.ShapeDtypeStruct((B * S, WIDTH_A), MXU_DTYPE),
        scratch_shapes=[pltpu.VMEM((S, Q_BLOCK), jnp.float32),
                        pltpu.VMEM((S, DSA_HEADS * Q_BLOCK), jnp.float32)],
        compiler_params=_params(("parallel", "arbitrary")),
        name="dsa",
    )(qi_t, gw_t, qa_t, ki, ka, va_t)


def _nsa_body(W, first_blk, gwt_ref, qbt_ref, ks_ref, kw_ref, vst_ref, vwt_ref, kc_ref, vct_ref, ovl_ref, o_ref,
              *, seq_len):
    f32 = jnp.float32
    QB, G = Q_BLOCK, NSA_GROUP
    n_slc = seq_len // SLC_LEN
    n_top = min(SLC_TOPN, n_slc)
    win_len = WINDOW + QB
    nc = kc_ref.shape[1]

    t0 = pl.program_id(1) * QB
    t = t0 + lax.broadcasted_iota(jnp.int32, (1, QB), 1)
    t4 = _lane_tile(t, G)
    valid_c = lax.broadcasted_iota(jnp.int32, (nc, G * QB), 0) * CMP_STRIDE + (CMP_LEN - 1) <= t4
    past = first_blk * QB
    causal_tail = past + lax.broadcasted_iota(jnp.int32, (W - past, G * QB), 0) <= t4
    w0 = pl.multiple_of(jnp.maximum(t0 - WINDOW, 0), QB)
    full_window = past >= WINDOW
    if full_window:
        r_minus_lane = lax.broadcasted_iota(jnp.int32, (QB, G * QB), 0) - (t4 - t0)
        w_head_ok = r_minus_lane > 0
        w_tail_ok = r_minus_lane <= 0
    else:
        wd = t4 - (w0 + lax.broadcasted_iota(jnp.int32, (win_len, G * QB), 0))
        wvalid = (wd >= 0) & (wd < WINDOW)
    jblk = lax.broadcasted_iota(jnp.int32, (MAX_SLC_BLOCKS, QB), 0)
    cur = lax.shift_right_logical(t, 6)
    future = jblk > cur
    forced = (jblk == 0) | (jblk == cur) | (jblk == cur - 1)
    ovl = ovl_ref[...]

    KH = range(NSA_KV_HEADS)
    ksl = [slice(kh * LANES, (kh + 1) * LANES) for kh in KH]
    q = [jnp.concatenate([qbt_ref[(kh * G + g) * LANES:(kh * G + g + 1) * LANES, :] for g in range(G)], axis=1)
         for kh in KH]

    s_c = [jnp.where(valid_c, _dot(kc_ref[kh], q[kh]), NEG_INF) for kh in KH]
    s_w = [_dot(kw_ref[pl.ds(w0, win_len), ksl[kh]], q[kh]) for kh in KH]
    e_c = [jnp.where(valid_c, jnp.exp2(s_c[kh] - jnp.max(s_c[kh], axis=0, keepdims=True)), 0.0) for kh in KH]
    l_c = [jnp.sum(e_c[kh], axis=0, keepdims=True) for kh in KH]
    p_c = [e_c[kh] * (1.0 / jnp.where(l_c[kh] > 0.0, l_c[kh], 1.0)) for kh in KH]
    o_cmp = [_dot(vct_ref[kh], p_c[kh].astype(MXU_DTYPE)) for kh in KH]

    val = []
    for kh in KH:
        pc = p_c[kh]
        p_sum = pc[:, 0:QB] + pc[:, QB:2 * QB] + pc[:, 2 * QB:3 * QB] + pc[:, 3 * QB:4 * QB]
        p_hi = p_sum.astype(MXU_DTYPE)
        p_lo = (p_sum - p_hi.astype(f32)).astype(MXU_DTYPE)
        imp = _dot(ovl, p_hi) + _dot(ovl, p_lo)
        val.append(jnp.where(future, -jnp.inf, imp + jnp.where(forced, FORCE_BONUS, 0.0)))

    win_problems = []
    for kh in KH:
        v_w = vwt_ref[ksl[kh], pl.ds(w0, win_len)]
        if full_window:
            parts = [(jnp.where(w_head_ok, s_w[kh][:QB], NEG_INF), v_w[:, :QB]),
                     (s_w[kh][QB:WINDOW], v_w[:, QB:WINDOW]),
                     (jnp.where(w_tail_ok, s_w[kh][WINDOW:], NEG_INF), v_w[:, WINDOW:])]
        else:
            parts = [(jnp.where(wvalid, s_w[kh], NEG_INF), v_w)]
        win_problems.append(parts)

    rank = [jnp.zeros(val[kh].shape, f32) for kh in KH]
    for i in range(min(n_slc, W // SLC_LEN)):
        for kh in KH:
            vi = val[kh][i:i + 1, :]
            rank[kh] = rank[kh] + jnp.where((vi > val[kh]) | ((vi == val[kh]) & (jblk > i)), 1.0, 0.0)

    s_s = []
    for kh in KH:
        bias = jnp.where((rank[kh] < n_top) & (jblk < n_slc), 0.0, -BLOCK_BIAS).astype(MXU_DTYPE)
        q_sel = jnp.concatenate([q[kh][:BIAS_COL], _lane_tile(bias, G)], axis=0)
        s_s.append(_dot(ks_ref[:W, ksl[kh]], q_sel))
    o_win = _softmax_pv_staged(win_problems)
    slc_problems = []
    for kh in KH:
        parts = [(jnp.where(causal_tail, s_s[kh][past:], NEG_INF), vst_ref[ksl[kh], past:W])]
        if past:
            parts.append((s_s[kh][:past], vst_ref[ksl[kh], :past]))
        slc_problems.append(parts)
    o_slc = _softmax_pv_staged(slc_problems)

    for kh in KH:
        inv_s = 1.0 / o_slc[kh][HEAD_DIM:HEAD_DIM + 1]
        inv_w = 1.0 / o_win[kh][HEAD_DIM:HEAD_DIM + 1]
        heads = []
        for g in range(G):
            h = kh * G + g
            ls = slice(g * QB, (g + 1) * QB)
            g0 = gwt_ref[3 * h + 0:3 * h + 1, :]
            g1 = gwt_ref[3 * h + 1:3 * h + 2, :] * inv_s[:, ls]
            g2 = gwt_ref[3 * h + 2:3 * h + 3, :] * inv_w[:, ls]
            heads.append(g0 * o_cmp[kh][:HEAD_DIM, ls] + g1 * o_slc[kh][:HEAD_DIM, ls]
                         + g2 * o_win[kh][:HEAD_DIM, ls])
        for g2_ in range(G // 2):
            pair = jnp.concatenate([heads[2 * g2_], heads[2 * g2_ + 1]], axis=0)
            col = (kh * G + 2 * g2_) * HEAD_DIM
            o_ref[:, col:col + 2 * HEAD_DIM] = pair.T.astype(o_ref.dtype)


def _nsa_kernel(gwt_ref, qbt_ref, ks_ref, kw_ref, vst_ref, vwt_ref, kc_ref, vct_ref, ovl_ref, o_ref,
                *, seq_len, classes):
    qblk = pl.program_id(1)
    for lo, hi, W in classes:
        @pl.when((qblk >= lo) & (qblk < hi))
        def _(W=W, lo=lo):
            _nsa_body(W, lo, gwt_ref, qbt_ref, ks_ref, kw_ref, vst_ref, vwt_ref, kc_ref, vct_ref, ovl_ref, o_ref,
                      seq_len=seq_len)


def _nsa(B, S, gw_t, qb_t, ks, kw, vs_t, vw_t, kc, vc_t, ovl):
    nq = S // Q_BLOCK
    nc = kc.shape[2]
    kvw = NSA_KV_HEADS * LANES
    fblk = lambda r: pl.BlockSpec((r, Q_BLOCK), lambda b, i: (0, b * nq + i))
    tseq = lambda w: pl.BlockSpec((S, w), lambda b, i: (b, 0))
    fseq = lambda r: pl.BlockSpec((r, S), lambda b, i: (0, b))
    return pl.pallas_call(
        functools.partial(_nsa_kernel, seq_len=S, classes=_width_classes(nq, NSA_WIDTH_CLASSES)),
        grid=(B, nq),
        in_specs=[fblk(LANES), fblk(NSA_HEADS * LANES), tseq(kvw), tseq(kvw), fseq(kvw), fseq(kvw),
                  pl.BlockSpec((None, NSA_KV_HEADS, nc, LANES), lambda b, i: (b, 0, 0, 0)),
                  pl.BlockSpec((None, NSA_KV_HEADS, LANES, nc), lambda b, i: (b, 0, 0, 0)),
                  pl.BlockSpec(ovl.shape, lambda b, i: (0, 0))],
        out_specs=pl.BlockSpec((Q_BLOCK, WIDTH_B), lambda b, i: (b * nq + i, 0)),
        out_shape=jax.ShapeDtypeStruct((B * S, WIDTH_B), MXU_DTYPE),
        compiler_params=_params(("parallel", "arbitrary")),
        name="nsa",
    )(gw_t, qb_t, ks, kw, vs_t, vw_t, kc, vc_t, ovl)


def _merge_ffn_kernel(x_ref, oa_ref, ob_ref, gmix_ref, wg_ref, wua_ref, wub_ref, wo_ref,
                      gmlp_ref, w1_ref, w2_ref, o_ref, *, chunk):
    x = x_ref[...]
    D = x.shape[1]
    h = _rms(x, gmix_ref[...]).astype(MXU_DTYPE)
    ga = jax.nn.sigmoid(_dot(h, wg_ref[:, :D]))
    gb = jax.nn.sigmoid(_dot(h, wg_ref[:, D:]))
    merged = ga * _dot(oa_ref[...], wua_ref[...]) + gb * _dot(ob_ref[...], wub_ref[...])
    x1 = x + _dot(merged.astype(MXU_DTYPE), wo_ref[...])
    h2 = _rms(x1, gmlp_ref[...]).astype(MXU_DTYPE)
    acc = x1
    for c in range(w1_ref.shape[1] // chunk):
        u = jnp.maximum(_dot(h2, w1_ref[:, c * chunk:(c + 1) * chunk]), 0.0)
        acc = acc + _dot((u * u).astype(MXU_DTYPE), w2_ref[c * chunk:(c + 1) * chunk, :])
    o_ref[...] = acc


def _merge_ffn(x2, oa, ob, g_mix, w_gate, w_up_a, w_up_b, w_out, g_mlp, w1, w2, tm):
    T, D = x2.shape
    row = lambda w: pl.BlockSpec((tm, w), lambda i: (i, 0))
    full = lambda a: pl.BlockSpec(a.shape, lambda i: (0,) * a.ndim, pipeline_mode=pl.Buffered(1))
    ins = (x2, oa, ob, g_mix, w_gate, w_up_a, w_up_b, w_out, g_mlp, w1, w2)
    return pl.pallas_call(
        functools.partial(_merge_ffn_kernel, chunk=1024),
        grid=(T // tm,),
        in_specs=[row(D), row(WIDTH_A), row(WIDTH_B)] + [full(a) for a in ins[3:]],
        out_specs=row(D),
        out_shape=jax.ShapeDtypeStruct((T, D), jnp.float32),
        compiler_params=_params(("parallel",)),
        name="merge_ffn",
    )(*ins)


def _layer(x, g_mix, w_in, g_q_a, g_k_a, g_lat_a, w_uk_a, w_uv_a, g_q_b, g_kc_b, g_ks_b, g_kw_b,
           pe_cmp_b, phi_k1_b, phi_k2_b, phi_v1_b, phi_v2_b, w_up_a, w_up_b, w_out, g_mlp, w_ff1, w_ff2):
    B, S, D = x.shape
    T = B * S
    f32 = jnp.float32
    mx = MXU_DTYPE
    n_slc = S // SLC_LEN
    assert n_slc <= MAX_SLC_BLOCKS and S % (DSA_WIDTH_CLASSES * TIE_CHUNK) == 0 and S >= WINDOW + Q_BLOCK
    col_sizes = (WIDTH_A, DSA_LATENT, IDX_HEADS * IDX_DIM, IDX_DIM, IDX_HEADS,
                 WIDTH_B, 6 * NSA_KV_W, 3 * NSA_HEADS, 2 * D)
    offs = np.cumsum((0,) + col_sizes)
    w_qa, w_ca, w_qi, w_ki, w_wi, w_qb, w_kvb, w_gb, w_gm = [
        w_in[:, offs[i]:offs[i + 1]] for i in range(len(col_sizes))]
    w_kc, w_vc, w_ks, w_vs, w_kw, w_vw = [w_kvb[:, j * NSA_KV_W:(j + 1) * NSA_KV_W] for j in range(6)]
    w_tok = jnp.concatenate(
        [w_ki, jnp.zeros((D, LANES - IDX_DIM), f32), w_ca, w_kc, w_vc, w_ks, w_kw], axis=1).astype(mx)
    assert w_tok.shape[1] == _T_COLS
    gw_pad = jnp.zeros((D, LANES - GATE_ROWS - IDX_HEADS), f32)
    w_feat = jnp.concatenate(
        [w_qi, w_gb, w_wi, gw_pad, w_qa, w_qb, w_vs, w_vw], axis=1).T.astype(mx)
    assert w_feat.shape[0] == _F_ROWS
    w_gate = w_gm.astype(mx)
    zpad = jnp.zeros((DSA_LATENT, LANES - HEAD_DIM), f32)
    w_uk = jnp.concatenate([w_uk_a, zpad], axis=1).astype(mx)
    w_uv_t = jnp.concatenate([w_uv_a, zpad], axis=1).T.astype(mx)
    row = lambda g: g.reshape(1, -1).astype(f32)
    rpad = lambda g: jnp.concatenate([row(g), jnp.zeros((1, LANES - HEAD_DIM), f32)], axis=1)
    colv = lambda g: g.reshape(-1, 1).astype(f32)

    x2 = x.reshape(T, D)
    (ki, ka, va_t, kcv, ks, kw, vs_t, vw_t, qi_t, gw_t, qa_t, qb_t) = _proj(
        x2, S, row(g_mix), w_tok, w_feat, row(g_lat_a), w_uk, w_uv_t, rpad(g_k_a), rpad(g_ks_b), rpad(g_kw_b),
        colv(g_q_a), colv(g_q_b), tm=512)

    n_chunk = S // CMP_STRIDE
    n_cmp = (S - CMP_LEN) // CMP_STRIDE + 1
    xc = kcv.reshape(2 * NSA_KV_HEADS, B, S, HEAD_DIM)
    phi1 = jnp.stack([phi_k1_b, phi_v1_b]).astype(mx)
    hpad = jnp.zeros((CMP_HIDDEN, LANES - HEAD_DIM), f32)
    w2k = jnp.concatenate([phi_k2_b, hpad], axis=1).astype(mx)
    w2v_t = jnp.concatenate([phi_v2_b, hpad], axis=1).T.astype(mx)
    kc, vc_t = _compress(xc, pe_cmp_b.astype(f32), phi1, w2k, w2v_t, rpad(g_kc_b))

    c_i = np.arange(n_chunk)[None, :] * CMP_STRIDE
    j_i = np.arange(MAX_SLC_BLOCKS)[:, None] * SLC_LEN
    ovl = (c_i < j_i + SLC_LEN) & (c_i + CMP_LEN > j_i) & (c_i < n_cmp * CMP_STRIDE) & (j_i < S)
    ovl = jnp.asarray(ovl, f32).astype(mx)

    o_a = _dsa(B, S, qi_t, gw_t, qa_t, ki, ka, va_t)
    o_b = _nsa(B, S, gw_t, qb_t, ks, kw, vs_t, vw_t, kc, vc_t, ovl)

    out = _merge_ffn(x2, o_a, o_b, row(g_mix), w_gate, w_up_a.astype(mx), w_up_b.astype(mx), w_out.astype(mx),
                     row(g_mlp), w_ff1.astype(mx), w_ff2.astype(mx), tm=512)
    return out.reshape(B, S, D)


def kernel(x, g_mix, w_in, g_q_a, g_k_a, g_lat_a, w_uk_a, w_uv_a, g_q_b, g_kc_b, g_ks_b, g_kw_b, pe_cmp_b,
           phi_k1_b, phi_k2_b, phi_v1_b, phi_v2_b, w_up_a, w_up_b, w_out, g_mlp, w_ff1, w_ff2):
    params = (g_mix, w_in, g_q_a, g_k_a, g_lat_a, w_uk_a, w_uv_a, g_q_b, g_kc_b, g_ks_b, g_kw_b, pe_cmp_b,
              phi_k1_b, phi_k2_b, phi_v1_b, phi_v2_b, w_up_a, w_up_b, w_out, g_mlp, w_ff1, w_ff2)
    for l in range(g_mix.shape[0]):
        x = _layer(x, *[p[l] for p in params])
    return x
```
